```python
import math
import jax
import jax.numpy as jnp
from jax import lax
import numpy as np

D_MODEL = 2048
BATCH = 4
SEQ = 4096
DEPTH = 4

D_FF = 11 * D_MODEL // 4
CHUNK = 128
A_WIDTH = D_MODEL // 2
A_GROUPS = 8
A_GROUP_CH = A_WIDTH // A_GROUPS
B_HEAD_DIM = 64
B_WIDTH = D_MODEL // 2
B_HEADS = B_WIDTH // (2 * B_HEAD_DIM)
C_INNER = D_MODEL
C_HEAD_DIM = 64
C_HEADS = C_INNER // C_HEAD_DIM
C_GROUPS = 4
C_STATE = 128
C_CONV = 4
C_CONV_CH = C_INNER + 2 * C_GROUPS * C_STATE
N_BRANCH = 3
REL_BUCKETS = 32
REL_MAX_DIST = 128
REL_EXACT = REL_BUCKETS // 2
IN_SPLITS = (A_WIDTH, A_WIDTH, B_WIDTH, B_WIDTH, B_WIDTH, C_INNER, C_CONV_CH, C_HEADS, N_BRANCH * D_MODEL)
D_IN = sum(IN_SPLITS)
EPS = 1e-6

kernel_name = 'hybrid_sgu_diffattn_ssd_macaron'


def rmsnorm(x, g):
    xf = x.astype(jnp.float32)
    y = xf * lax.rsqrt(jnp.mean(xf * xf, axis=-1, keepdims=True) + EPS)
    return (y * g.astype(jnp.float32)).astype(x.dtype)


def layernorm(x, g, b):
    xf = x.astype(jnp.float32)
    mu = jnp.mean(xf, axis=-1, keepdims=True)
    var = jnp.mean(jnp.square(xf - mu), axis=-1, keepdims=True)
    y = (xf - mu) * lax.rsqrt(var + EPS)
    return (y * g.astype(jnp.float32) + b.astype(jnp.float32)).astype(x.dtype)


def swiglu_ffn(x, wi, wo):
    gate, up = jnp.split(x @ wi, 2, axis=-1)
    return (jax.nn.silu(gate) * up) @ wo


def chunked_sgu(a_u, a_v, ln_g, ln_b, w_s, b_s):
    u = jax.nn.gelu(a_u)
    v = layernorm(jax.nn.gelu(a_v), ln_g, ln_b)
    b_, s_, _ = v.shape
    vc = v.reshape(b_, s_ // CHUNK, CHUNK, A_GROUPS, A_GROUP_CH)
    w = w_s * jnp.tril(jnp.ones((CHUNK, CHUNK), w_s.dtype))
    mixed = jnp.einsum('bnsgc,gts->bntgc', vc, w) + b_s.T[:, :, None]
    return u * mixed.reshape(b_, s_, A_WIDTH)


def t5_bucket(dist):
    n = jnp.maximum(dist, 0)
    nf = jnp.maximum(n, 1).astype(jnp.float32)
    large = REL_EXACT + (jnp.log(nf / REL_EXACT) / math.log(REL_MAX_DIST / REL_EXACT)
                         * (REL_BUCKETS - REL_EXACT)).astype(jnp.int32)
    large = jnp.minimum(large, REL_BUCKETS - 1)
    return jnp.where(n < REL_EXACT, n, large)


def diff_attention(q, k, v, lam, rel_bias):
    b_, s_, h_, _, dh = q.shape
    nb = s_ // CHUNK
    scale = dh ** -0.5
    k1 = k[..., 0, :]
    k2 = k[..., 1, :]
    qb = q.reshape(b_, nb, CHUNK, h_, 2, dh).swapaxes(0, 1)
    kpos = jnp.arange(s_)
    table = rel_bias.astype(jnp.float32)

    def block(args):
        qblk, i = args
        qpos = i * CHUNK + jnp.arange(CHUNK)
        dist = qpos[:, None] - kpos[None, :]
        bias = table[t5_bucket(dist)].transpose(2, 0, 1)
        causal = dist >= 0

        def probs(qq, kk):
            s = jnp.einsum('bqhd,bkhd->bhqk', qq, kk).astype(jnp.float32) * scale + bias
            return jax.nn.softmax(jnp.where(causal, s, -jnp.inf), axis=-1)

        p = probs(qblk[..., 0, :], k1) - lam * probs(qblk[..., 1, :], k2)
        return jnp.einsum('bhqk,bkhe->bqhe', p.astype(v.dtype), v)

    out = lax.map(block, (qb, jnp.arange(nb)))
    return out.swapaxes(0, 1).reshape(b_, s_, h_, 2 * dh)


def ssd(x, dt, A, Bm, Cm):
    f32 = jnp.float32
    b_, s_, h_, p_ = x.shape
    g_, n_ = Bm.shape[2], Bm.shape[3]
    r_ = h_ // g_
    c_ = s_ // CHUNK
    xd = (x.astype(f32) * dt[..., None]).reshape(b_, c_, CHUNK, g_, r_, p_)
    a_cs = jnp.cumsum((dt * A).reshape(b_, c_, CHUNK, g_, r_), axis=2)
    Bc = Bm.astype(f32).reshape(b_, c_, CHUNK, g_, n_)
    Cc = Cm.astype(f32).reshape(b_, c_, CHUNK, g_, n_)
    mask = jnp.tril(jnp.ones((CHUNK, CHUNK), bool))[None, None, :, :, None, None]
    seg = a_cs[:, :, :, None] - a_cs[:, :, None, :]
    decay = jnp.exp(jnp.where(mask, seg, -jnp.inf))
    cb = jnp.einsum('bclgn,bcsgn->bclsg', Cc, Bc)
    y_diag = jnp.einsum('bclsgr,bcsgrp->bclgrp', cb[..., None] * decay, xd)
    decay_states = jnp.exp(a_cs[:, :, -1:] - a_cs)
    states = jnp.einsum('bcsgn,bcsgrp->bcgrpn', Bc, xd * decay_states[..., None])
    chunk_decay = jnp.exp(a_cs[:, :, -1])

    def step(h, inp):
        st, dec = inp
        return h * dec[..., None, None] + st, h

    _, prev = lax.scan(step, jnp.zeros_like(states[:, 0]),
                       (jnp.moveaxis(states, 1, 0), jnp.moveaxis(chunk_decay, 1, 0)))
    prev = jnp.moveaxis(prev, 0, 1)
    y_off = jnp.einsum('bclgn,bcgrpn->bclgrp', Cc, prev) * jnp.exp(a_cs)[..., None]
    return (y_diag + y_off).reshape(b_, s_, h_, p_)


def mamba2_mixer(z, xbc, dt_raw, conv_w, conv_b, dt_bias, a_log, d_skip, norm_g):
    xbc = lax.conv_general_dilated(xbc, conv_w.astype(xbc.dtype)[:, None, :], (1,), [(C_CONV - 1, 0)],
                                   dimension_numbers=('NWC', 'WIO', 'NWC'),
                                   feature_group_count=C_CONV_CH)
    xbc = jax.nn.silu(xbc + conv_b)
    xs, Bm, Cm = jnp.split(xbc, [C_INNER, C_INNER + C_GROUPS * C_STATE], axis=-1)
    b_, s_, _ = xs.shape
    xs = xs.reshape(b_, s_, C_HEADS, C_HEAD_DIM)
    Bm = Bm.reshape(b_, s_, C_GROUPS, C_STATE)
    Cm = Cm.reshape(b_, s_, C_GROUPS, C_STATE)
    dt = jax.nn.softplus(dt_raw.astype(jnp.float32) + dt_bias.astype(jnp.float32))
    A = -jnp.exp(a_log.astype(jnp.float32))
    y = ssd(xs, dt, A, Bm, Cm) + d_skip.astype(jnp.float32)[:, None] * xs.astype(jnp.float32)
    y = y.reshape(b_, s_, C_INNER).astype(z.dtype)
    return rmsnorm(y * jax.nn.silu(z), norm_g)


def hybrid_mixer(xn, rel_bias, w_in, sgu_ln_g, sgu_ln_b, sgu_w, sgu_b, diff_lambda, diff_subln,
                 conv_w, conv_b, dt_bias, a_log, d_skip, ssm_norm, w_pa, w_pb, w_pc, w_out, lam_init):
    b_, s_, _ = xn.shape
    split_idx = [int(i) for i in np.cumsum(IN_SPLITS)[:-1]]
    a_u, a_v, q, k, v, z, xbc, dt_raw, gate_logits = jnp.split(xn @ w_in, split_idx, axis=-1)
    y_a = chunked_sgu(a_u, a_v, sgu_ln_g, sgu_ln_b, sgu_w, sgu_b)
    lq1, lk1, lq2, lk2 = diff_lambda.astype(jnp.float32)
    lam = jnp.exp(jnp.sum(lq1 * lk1)) - jnp.exp(jnp.sum(lq2 * lk2)) + lam_init
    attn = diff_attention(q.reshape(b_, s_, B_HEADS, 2, B_HEAD_DIM),
                          k.reshape(b_, s_, B_HEADS, 2, B_HEAD_DIM),
                          v.reshape(b_, s_, B_HEADS, 2 * B_HEAD_DIM), lam, rel_bias)
    y_b = (rmsnorm(attn, diff_subln) * (1.0 - lam_init)).reshape(b_, s_, B_WIDTH)
    y_c = mamba2_mixer(z, xbc, dt_raw, conv_w, conv_b, dt_bias, a_log, d_skip, ssm_norm)
    gates = jax.nn.sigmoid(gate_logits.astype(jnp.float32)).astype(xn.dtype).reshape(b_, s_, N_BRANCH, D_MODEL)
    merged = (gates[..., 0, :] * (y_a @ w_pa) + gates[..., 1, :] * (y_b @ w_pb)
              + gates[..., 2, :] * (y_c @ w_pc))
    return merged @ w_out


def setup_inputs(seed: int = 0) -> dict:
    key = jax.random.key(seed)
    ks = iter(jax.random.split(key, 32))
    f32 = jnp.float32
    L = DEPTH

    def nrm(shape, scale):
        return jax.random.normal(next(ks), shape, f32) * scale

    def gain(shape):
        return 1.0 + 0.02 * jax.random.normal(next(ks), shape, f32)

    x = nrm((BATCH, SEQ, D_MODEL), 1.0)
    rel_bias = nrm((REL_BUCKETS, B_HEADS), 0.5)
    final_norm = gain((D_MODEL,))
    ffn1_norm = gain((L, D_MODEL))
    ffn1_wi = nrm((L, D_MODEL, 2 * D_FF), D_MODEL ** -0.5)
    ffn1_wo = nrm((L, D_FF, D_MODEL), D_FF ** -0.5)
    mix_norm = gain((L, D_MODEL))
    w_in = nrm((L, D_MODEL, D_IN), D_MODEL ** -0.5)
    sgu_ln_g = gain((L, A_WIDTH))
    sgu_ln_b = nrm((L, A_WIDTH), 0.02)
    sgu_w = nrm((L, A_GROUPS, CHUNK, CHUNK), CHUNK ** -0.5)
    sgu_b = gain((L, A_GROUPS, CHUNK))
    diff_lambda = nrm((L, 4, B_HEAD_DIM), 0.1)
    diff_subln = gain((L, 2 * B_HEAD_DIM))
    conv_w = nrm((L, C_CONV, C_CONV_CH), C_CONV ** -0.5)
    conv_b = nrm((L, C_CONV_CH), 0.02)
    u = jax.random.uniform(next(ks), (L, C_HEADS), f32)
    dt0 = jnp.exp(u * (math.log(0.1) - math.log(0.001)) + math.log(0.001))
    dt_bias = dt0 + jnp.log(-jnp.expm1(-dt0))
    a_log = jnp.log(jax.random.uniform(next(ks), (L, C_HEADS), f32, minval=1.0, maxval=16.0))
    d_skip = gain((L, C_HEADS))
    ssm_norm = gain((L, C_INNER))
    w_pa = nrm((L, A_WIDTH, D_MODEL), A_WIDTH ** -0.5)
    w_pb = nrm((L, B_WIDTH, D_MODEL), B_WIDTH ** -0.5)
    w_pc = nrm((L, C_INNER, D_MODEL), C_INNER ** -0.5)
    w_out = nrm((L, D_MODEL, D_MODEL), D_MODEL ** -0.5)
    ffn2_norm = gain((L, D_MODEL))
    ffn2_wi = nrm((L, D_MODEL, 2 * D_FF), D_MODEL ** -0.5)
    ffn2_wo = nrm((L, D_FF, D_MODEL), D_FF ** -0.5)
    return {'x': x, 'rel_bias': rel_bias, 'final_norm': final_norm,
            'ffn1_norm': ffn1_norm, 'ffn1_wi': ffn1_wi, 'ffn1_wo': ffn1_wo,
            'mix_norm': mix_norm, 'w_in': w_in,
            'sgu_ln_g': sgu_ln_g, 'sgu_ln_b': sgu_ln_b, 'sgu_w': sgu_w, 'sgu_b': sgu_b,
            'diff_lambda': diff_lambda, 'diff_subln': diff_subln,
            'conv_w': conv_w, 'conv_b': conv_b, 'dt_bias': dt_bias, 'a_log': a_log,
            'd_skip': d_skip, 'ssm_norm': ssm_norm,
            'w_pa': w_pa, 'w_pb': w_pb, 'w_pc': w_pc, 'w_out': w_out,
            'ffn2_norm': ffn2_norm, 'ffn2_wi': ffn2_wi, 'ffn2_wo': ffn2_wo}


def reference(x, rel_bias, final_norm, ffn1_norm, ffn1_wi, ffn1_wo, mix_norm, w_in,
              sgu_ln_g, sgu_ln_b, sgu_w, sgu_b, diff_lambda, diff_subln,
              conv_w, conv_b, dt_bias, a_log, d_skip, ssm_norm,
              w_pa, w_pb, w_pc, w_out, ffn2_norm, ffn2_wi, ffn2_wo):
    h = x
    for l in range(DEPTH):
        lam_init = 0.8 - 0.6 * math.exp(-0.3 * l)
        h = h + 0.5 * swiglu_ffn(rmsnorm(h, ffn1_norm[l]), ffn1_wi[l], ffn1_wo[l])
        h = h + hybrid_mixer(rmsnorm(h, mix_norm[l]), rel_bias, w_in[l],
                             sgu_ln_g[l], sgu_ln_b[l], sgu_w[l], sgu_b[l],
                             diff_lambda[l], diff_subln[l],
                             conv_w[l], conv_b[l], dt_bias[l], a_log[l], d_skip[l], ssm_norm[l],
                             w_pa[l], w_pb[l], w_pc[l], w_out[l], lam_init)
        h = h + 0.5 * swiglu_ffn(rmsnorm(h, ffn2_norm[l]), ffn2_wi[l], ffn2_wo[l])
    return rmsnorm(h, final_norm)
```

```python
import functools
import math

import numpy as np
import jax
import jax.numpy as jnp
from jax import lax
from jax.experimental import pallas as pl
from jax.experimental.pallas import tpu as pltpu

F32 = jnp.float32
BF16 = jnp.bfloat16

EPS = 1e-6
CHUNK = 128
LANES = 128
A_GROUPS = 8
B_HEAD_DIM = 64
C_HEAD_DIM = 64
C_GROUPS = 4
C_STATE = 128
C_CONV = 4
REL_BUCKETS = 32
REL_MAX_DIST = 128
REL_EXACT = REL_BUCKETS // 2
NEG_BIG = -1e30
VMEM_LIMIT_V7X = 56 * 1024 * 1024

OFF_XBC, OFF_AU, OFF_Z, OFF_AV, OFF_Q, OFF_K, OFF_V, OFF_GATE = 0, 3072, 4096, 6144, 7168, 8192, 9216, 10240


def _cparams(n_axes):
    return pltpu.CompilerParams(dimension_semantics=("arbitrary",) * n_axes,
                                vmem_limit_bytes=VMEM_LIMIT_V7X)


def _rmsnorm_f32(x, g):
    ms = jnp.mean(x * x, axis=-1, keepdims=True)
    return (x * lax.rsqrt(ms + EPS)) * g


def _ffn_kernel(x_ref, g_ref, wg_ref, wu_ref, wo_ref, o_ref, xn_ref):
    j = pl.program_id(1)

    @pl.when(j == 0)
    def _():
        xn_ref[...] = _rmsnorm_f32(x_ref[...], g_ref[...]).astype(BF16)
        o_ref[...] = jnp.zeros_like(o_ref)

    xn = xn_ref[...]
    gate = jnp.dot(xn, wg_ref[...], preferred_element_type=F32)
    up = jnp.dot(xn, wu_ref[...], preferred_element_type=F32)
    act = (gate * jax.nn.sigmoid(gate) * up).astype(BF16)
    o_ref[...] += jnp.dot(act, wo_ref[...], preferred_element_type=F32)

    @pl.when(j == pl.num_programs(1) - 1)
    def _():
        o_ref[...] = x_ref[...] + 0.5 * o_ref[...]


def _ffn(h, g, wi, wo, *, bm, bf):
    t, d = h.shape
    d_ff = wo.shape[0]
    nj = d_ff // bf
    return pl.pallas_call(
        _ffn_kernel,
        out_shape=jax.ShapeDtypeStruct((t, d), F32),
        grid=(t // bm, nj),
        in_specs=[
            pl.BlockSpec((bm, d), lambda i, j: (i, 0)),
            pl.BlockSpec((1, d), lambda i, j: (0, 0)),
            pl.BlockSpec((d, bf), lambda i, j: (0, j)),
            pl.BlockSpec((d, bf), lambda i, j: (0, j + nj)),
            pl.BlockSpec((bf, d), lambda i, j: (j, 0)),
        ],
        out_specs=pl.BlockSpec((bm, d), lambda i, j: (i, 0)),
        scratch_shapes=[pltpu.VMEM((bm, d), BF16)],
        compiler_params=_cparams(2),
        name="ffn",
    )(h, g, wi, wi, wo)


def _inproj_kernel(x_ref, g_ref, w_ref, wdt_ref, o_ref, dt_ref, xn_ref, *, nj_plain):
    j = pl.program_id(1)

    @pl.when(j == 0)
    def _():
        xn = _rmsnorm_f32(x_ref[...], g_ref[...]).astype(BF16)
        xn_ref[...] = xn
        dt_ref[...] = jnp.dot(xn, wdt_ref[...], preferred_element_type=F32)

    acc = jnp.dot(xn_ref[...], w_ref[...], preferred_element_type=F32)

    @pl.when(j < nj_plain)
    def _():
        o_ref[...] = acc.astype(BF16)

    @pl.when(j >= nj_plain)
    def _():
        o_ref[...] = jax.nn.sigmoid(acc).astype(BF16)


def _inproj(h, g, w_main, w_dt, *, bm, bn):
    t, d = h.shape
    n = w_main.shape[1]
    return pl.pallas_call(
        functools.partial(_inproj_kernel, nj_plain=OFF_GATE // bn),
        out_shape=(jax.ShapeDtypeStruct((t, n), BF16), jax.ShapeDtypeStruct((t, LANES), F32)),
        grid=(t // bm, n // bn),
        in_specs=[
            pl.BlockSpec((bm, d), lambda i, j: (i, 0)),
            pl.BlockSpec((1, d), lambda i, j: (0, 0)),
            pl.BlockSpec((d, bn), lambda i, j: (0, j)),
            pl.BlockSpec((d, LANES), lambda i, j: (0, 0)),
        ],
        out_specs=(pl.BlockSpec((bm, bn), lambda i, j: (i, j)),
                   pl.BlockSpec((bm, LANES), lambda i, j: (i, 0))),
        scratch_shapes=[pltpu.VMEM((bm, d), BF16)],
        compiler_params=_cparams(2),
        name="inproj",
    )(h, g, w_main, w_dt)


def _sgu_kernel(u_ref, v_ref, lng_ref, lnb_ref, w_ref, bsb_ref, o_ref):
    rows = u_ref.shape[0]
    u = jax.nn.gelu(u_ref[...].astype(F32))
    v = jax.nn.gelu(v_ref[...].astype(F32))
    mu = jnp.mean(v, axis=-1, keepdims=True)
    var = jnp.mean(jnp.square(v - mu), axis=-1, keepdims=True)
    vn = (((v - mu) * lax.rsqrt(var + EPS)) * lng_ref[...] + lnb_ref[...]).astype(BF16)
    r_i = lax.broadcasted_iota(jnp.int32, (CHUNK, CHUNK), 0)
    c_i = lax.broadcasted_iota(jnp.int32, (CHUNK, CHUNK), 1)
    tri = r_i >= c_i
    for g in range(A_GROUPS):
        wm = jnp.where(tri, w_ref[g], 0.0).astype(BF16)
        cs = slice(g * LANES, (g + 1) * LANES)
        for c in range(rows // CHUNK):
            rs = slice(c * CHUNK, (c + 1) * CHUNK)
            mixed = jnp.dot(wm, vn[rs, cs], preferred_element_type=F32) + bsb_ref[:, cs]
            o_ref[rs, cs] = (u[rs, cs] * mixed).astype(BF16)


def _sgu(proj, ln_g, ln_b, w_s, bsb, *, rows):
    t = proj.shape[0]
    aw = ln_g.shape[1]
    return pl.pallas_call(
        _sgu_kernel,
        out_shape=jax.ShapeDtypeStruct((t, aw), BF16),
        grid=(t // rows,),
        in_specs=[
            pl.BlockSpec((rows, aw), lambda i: (i, OFF_AU // aw)),
            pl.BlockSpec((rows, aw), lambda i: (i, OFF_AV // aw)),
            pl.BlockSpec((1, aw), lambda i: (0, 0)),
            pl.BlockSpec((1, aw), lambda i: (0, 0)),
            pl.BlockSpec((A_GROUPS, CHUNK, CHUNK), lambda i: (0, 0, 0)),
            pl.BlockSpec((CHUNK, aw), lambda i: (0, 0)),
        ],
        out_specs=pl.BlockSpec((rows, aw), lambda i: (i, 0)),
        compiler_params=_cparams(1),
        name="sgu",
    )(proj, proj, ln_g, ln_b, w_s, bsb)


def _t5_bucket_np(dist):
    n = np.maximum(dist, 0)
    nf = np.maximum(n, 1).astype(np.float64)
    large = REL_EXACT + (np.log(nf / REL_EXACT) / math.log(REL_MAX_DIST / REL_EXACT)
                         * (REL_BUCKETS - REL_EXACT)).astype(np.int32)
    large = np.minimum(large, REL_BUCKETS - 1)
    return np.where(n < REL_EXACT, n, large).astype(np.int32)


def _bucket_tiles(bq):
    r = np.arange(bq)[:, None]
    c = np.arange(bq)[None, :]
    diag = np.where(r - c >= 0, _t5_bucket_np(r - c), REL_BUCKETS)
    prev = _t5_bucket_np(bq + r - c)
    assert _t5_bucket_np(np.array([bq + 1]))[0] == REL_BUCKETS - 1
    return np.stack([diag, prev]).astype(np.int32)


def _attn_kernel(rb_ref, bkt_ref, lam_ref, sub_ref, q_ref, k_ref, v_ref, o_ref,
                 bias_ref, m_ref, l_ref, acc_ref, *, lam_init):
    bq = q_ref.shape[0]
    hd = pl.program_id(1)
    qi = pl.program_id(2)

    @pl.when(qi == 0)
    def _():
        for t in range(2):
            bt = bkt_ref[t]
            tile = jnp.where(bt == REL_BUCKETS, NEG_BIG, 0.0).astype(F32)
            for b in range(REL_BUCKETS):
                tile = jnp.where(bt == b, rb_ref[b, hd], tile)
            bias_ref[t] = tile
        bias_ref[2] = jnp.full((bq, bq), rb_ref[REL_BUCKETS - 1, hd], F32)

    lane = lax.broadcasted_iota(jnp.int32, (bq, 2 * B_HEAD_DIM), 1)
    qs = q_ref[...] * jnp.asarray(B_HEAD_DIM ** -0.5, BF16)
    zero = jnp.zeros_like(qs)
    qq = jnp.concatenate([jnp.where(lane < B_HEAD_DIM, qs, zero),
                          jnp.where(lane >= B_HEAD_DIM, qs, zero)], axis=0)
    m_ref[...] = jnp.full(m_ref.shape, NEG_BIG, F32)
    l_ref[...] = jnp.zeros(l_ref.shape, F32)
    acc_ref[...] = jnp.zeros(acc_ref.shape, F32)

    def body(j, carry):
        start = pl.multiple_of(j * bq, bq)
        kb = k_ref[pl.ds(start, bq), :]
        vb = v_ref[pl.ds(start, bq), :]
        s = lax.dot_general(qq, kb, (((1,), (1,)), ((), ())), preferred_element_type=F32)
        bias = bias_ref[jnp.minimum(qi - j, 2)]
        s = s + jnp.concatenate([bias, bias], axis=0)
        m_prev = m_ref[...]
        m_next = jnp.maximum(m_prev, jnp.max(s, axis=1, keepdims=True))
        alpha = jnp.exp(m_prev - m_next)
        p = jnp.exp(s - jnp.concatenate([m_next] * (bq // LANES), axis=1))
        l_ref[...] = alpha * l_ref[...] + jnp.sum(p, axis=1, keepdims=True)
        acc_ref[...] = alpha * acc_ref[...] + jnp.dot(p.astype(BF16), vb, preferred_element_type=F32)
        m_ref[...] = m_next
        return carry

    lax.fori_loop(0, qi + 1, body, 0)

    lam_p = lam_ref[...]
    lam = (jnp.exp(jnp.sum(lam_p[0:1] * lam_p[1:2], axis=1, keepdims=True))
           - jnp.exp(jnp.sum(lam_p[2:3] * lam_p[3:4], axis=1, keepdims=True)) + lam_init)
    o1 = acc_ref[0:bq, :] / l_ref[0:bq, :]
    o2 = acc_ref[bq:2 * bq, :] / l_ref[bq:2 * bq, :]
    attn = o1 - lam * o2
    o_ref[...] = (_rmsnorm_f32(attn, sub_ref[...]) * (1.0 - lam_init)).astype(BF16)


def _attn(proj, rel_bias, buckets, lam_p, subln, *, batch, seq, bq, lam_init):
    t = proj.shape[0]
    hw = 2 * B_HEAD_DIM
    heads = (OFF_K - OFF_Q) // hw
    nq = seq // bq
    return pl.pallas_call(
        functools.partial(_attn_kernel, lam_init=lam_init),
        out_shape=jax.ShapeDtypeStruct((t, heads * hw), BF16),
        grid=(batch, heads, nq),
        in_specs=[
            pl.BlockSpec(memory_space=pltpu.SMEM),
            pl.BlockSpec((2, bq, bq), lambda b, h, i: (0, 0, 0)),
            pl.BlockSpec((4, B_HEAD_DIM), lambda b, h, i: (0, 0)),
            pl.BlockSpec((1, hw), lambda b, h, i: (0, 0)),
            pl.BlockSpec((bq, hw), lambda b, h, i: (b * nq + i, OFF_Q // hw + h)),
            pl.BlockSpec((seq, hw), lambda b, h, i: (b, OFF_K // hw + h)),
            pl.BlockSpec((seq, hw), lambda b, h, i: (b, OFF_V // hw + h)),
        ],
        out_specs=pl.BlockSpec((bq, hw), lambda b, h, i: (b * nq + i, h)),
        scratch_shapes=[pltpu.VMEM((3, bq, bq), F32), pltpu.VMEM((2 * bq, LANES), F32),
                        pltpu.VMEM((2 * bq, LANES), F32), pltpu.VMEM((2 * bq, hw), F32)],
        compiler_params=_cparams(3),
        name="diff_attn",
    )(rel_bias, buckets, lam_p, subln, proj, proj, proj)


def _split_dot(x, e_bf16, passes):
    out = None
    r = x
    for _ in range(passes):
        hi = r.astype(BF16)
        part = jnp.dot(hi, e_bf16, preferred_element_type=F32)
        out = part if out is None else out + part
        r = r - hi.astype(F32)
    return out


def _mamba_kernel(xbc_ref, z_ref, dt_ref, cw_ref, cb_ref, dtb_ref, alog_ref, dsk_ref, ng_ref, e_ref,
                  o_ref, xpad_ref, st_ref):
    c_inner = z_ref.shape[1]
    gw = c_inner // C_GROUPS
    c = pl.program_id(1)

    @pl.when(c == 0)
    def _():
        xpad_ref[0:8, :] = jnp.zeros((8, xpad_ref.shape[1]), F32)
        st_ref[...] = jnp.zeros_like(st_ref)

    @pl.when(c > 0)
    def _():
        xpad_ref[0:8, :] = xpad_ref[CHUNK:CHUNK + 8, :]

    xpad_ref[8:8 + CHUNK, :] = xbc_ref[...].astype(F32)
    conv = cb_ref[...]
    for k in range(C_CONV):
        conv = conv + cw_ref[k:k + 1, :] * xpad_ref[pl.ds(8 - (C_CONV - 1) + k, CHUNK), :]
    xbc = conv * jax.nn.sigmoid(conv)
    xs = xbc[:, :c_inner]
    bm = xbc[:, c_inner:c_inner + C_GROUPS * C_STATE]
    cm = xbc[:, c_inner + C_GROUPS * C_STATE:]

    dt = jax.nn.softplus(dt_ref[...] + dtb_ref[...])
    a = dt * (-jnp.exp(alog_ref[...]))
    r_i = lax.broadcasted_iota(jnp.int32, (CHUNK, CHUNK), 0)
    c_i = lax.broadcasted_iota(jnp.int32, (CHUNK, CHUNK), 1)
    tri = r_i >= c_i
    lower = tri.astype(F32)
    upper = (r_i <= c_i).astype(F32)
    hp = lax.Precision.HIGHEST
    a_cs = jnp.dot(lower, a, precision=hp, preferred_element_type=F32)
    a_cs_t = jnp.dot(a.T, upper, precision=hp, preferred_element_type=F32)

    e = e_ref[...]
    dt_x = _split_dot(dt, e, 2)
    acs_x = _split_dot(a_cs, e, 3)
    last_x = acs_x[CHUNK - 1:CHUNK, :]
    xd = xs * dt_x
    xd_b = xd.astype(BF16)
    xdd_b = (xd * jnp.exp(last_x - acs_x)).astype(BF16)
    prev = st_ref[...]
    prev_b = prev.astype(BF16)
    eacs = jnp.exp(acs_x)
    lane_lo = lax.broadcasted_iota(jnp.int32, (CHUNK, LANES), 1) < C_HEAD_DIM
    heads_per_group = gw // C_HEAD_DIM

    y_parts = []
    st_parts = []
    for g in range(C_GROUPS):
        gs = slice(g * gw, (g + 1) * gw)
        b_g = bm[:, g * C_STATE:(g + 1) * C_STATE]
        c_g = cm[:, g * C_STATE:(g + 1) * C_STATE].astype(BF16)
        cb = lax.dot_general(c_g, b_g.astype(BF16), (((1,), (1,)), ((), ())), preferred_element_type=F32)
        st_parts.append(jnp.dot(b_g.T.astype(BF16), xdd_b[:, gs], preferred_element_type=F32))
        y_off = jnp.dot(c_g, prev_b[:, gs], preferred_element_type=F32) * eacs[:, gs]
        tiles = []
        for pr in range(heads_per_group // 2):
            t_idx = g * (heads_per_group // 2) + pr
            xt = xd_b[:, t_idx * LANES:(t_idx + 1) * LANES]
            res = []
            for half in range(2):
                hd = 2 * t_idx + half
                seg = a_cs[:, hd:hd + 1] - a_cs_t[hd:hd + 1, :]
                decay = jnp.where(tri, jnp.exp(jnp.minimum(seg, 0.0)), 0.0)
                res.append(jnp.dot((cb * decay).astype(BF16), xt, preferred_element_type=F32))
            tiles.append(jnp.where(lane_lo, res[0], res[1]))
        y_parts.append(jnp.concatenate(tiles, axis=1) + y_off)

    st_ref[...] = prev * jnp.exp(last_x) + jnp.concatenate(st_parts, axis=1)
    y = jnp.concatenate(y_parts, axis=1) + dsk_ref[...] * xs
    zf = z_ref[...].astype(F32)
    gated = y * (zf * jax.nn.sigmoid(zf))
    o_ref[...] = _rmsnorm_f32(gated, ng_ref[...]).astype(BF16)


def _mamba(proj, dt_raw, conv_w, conv_b, dt_bias, a_log, d_skip_x, norm_g, expand, *, batch, seq):
    t = proj.shape[0]
    c_inner = norm_g.shape[1]
    conv_ch = conv_w.shape[1]
    nc = seq // CHUNK
    row = lambda b, c: (b * nc + c, 0)
    const = lambda b, c: (0, 0)
    return pl.pallas_call(
        _mamba_kernel,
        out_shape=jax.ShapeDtypeStruct((t, c_inner), BF16),
        grid=(batch, nc),
        in_specs=[
            pl.BlockSpec((CHUNK, conv_ch), lambda b, c: (b * nc + c, OFF_XBC // conv_ch)),
            pl.BlockSpec((CHUNK, c_inner), lambda b, c: (b * nc + c, OFF_Z // c_inner)),
            pl.BlockSpec((CHUNK, LANES), row),
            pl.BlockSpec((C_CONV, conv_ch), const),
            pl.BlockSpec((1, conv_ch), const),
            pl.BlockSpec((1, LANES), const),
            pl.BlockSpec((1, LANES), const),
            pl.BlockSpec((1, c_inner), const),
            pl.BlockSpec((1, c_inner), const),
            pl.BlockSpec((LANES, c_inner), const),
        ],
        out_specs=pl.BlockSpec((CHUNK, c_inner), row),
        scratch_shapes=[pltpu.VMEM((CHUNK + 8, conv_ch), F32), pltpu.VMEM((C_STATE, c_inner), F32)],
        compiler_params=_cparams(2),
        name="mamba_ssd",
    )(proj, proj, dt_raw, conv_w, conv_b, dt_bias, a_log, d_skip_x, norm_g, expand)


def _merge_kernel(ya_ref, yb_ref, yc_ref, g0_ref, g1_ref, g2_ref, wa_ref, wb_ref, wc_ref, o_ref):
    pa = jnp.dot(ya_ref[...], wa_ref[...], preferred_element_type=F32)
    pb = jnp.dot(yb_ref[...], wb_ref[...], preferred_element_type=F32)
    pc = jnp.dot(yc_ref[...], wc_ref[...], preferred_element_type=F32)
    o_ref[...] = (g0_ref[...].astype(F32) * pa + g1_ref[...].astype(F32) * pb
                  + g2_ref[...].astype(F32) * pc).astype(BF16)


def _merge(ya, yb, yc, proj, w_pa, w_pb, w_pc, *, bm, bn):
    t = ya.shape[0]
    d = w_pa.shape[1]
    gate_spec = lambda k: pl.BlockSpec((bm, bn), lambda i, j: (i, (OFF_GATE + k * d) // bn + j))
    return pl.pallas_call(
        _merge_kernel,
        out_shape=jax.ShapeDtypeStruct((t, d), BF16),
        grid=(t // bm, d // bn),
        in_specs=[
            pl.BlockSpec((bm, ya.shape[1]), lambda i, j: (i, 0)),
            pl.BlockSpec((bm, yb.shape[1]), lambda i, j: (i, 0)),
            pl.BlockSpec((bm, yc.shape[1]), lambda i, j: (i, 0)),
            gate_spec(0), gate_spec(1), gate_spec(2),
            pl.BlockSpec((w_pa.shape[0], bn), lambda i, j: (0, j)),
            pl.BlockSpec((w_pb.shape[0], bn), lambda i, j: (0, j)),
            pl.BlockSpec((w_pc.shape[0], bn), lambda i, j: (0, j)),
        ],
        out_specs=pl.BlockSpec((bm, bn), lambda i, j: (i, j)),
        compiler_params=_cparams(2),
        name="merge",
    )(ya, yb, yc, proj, proj, proj, w_pa, w_pb, w_pc)


def _outproj_kernel(m_ref, w_ref, h_ref, o_ref):
    o_ref[...] = h_ref[...] + jnp.dot(m_ref[...], w_ref[...], preferred_element_type=F32)


def _outproj(merged, w_out, h, *, bm, bn):
    t, d = h.shape
    return pl.pallas_call(
        _outproj_kernel,
        out_shape=jax.ShapeDtypeStruct((t, d), F32),
        grid=(t // bm, d // bn),
        in_specs=[
            pl.BlockSpec((bm, merged.shape[1]), lambda i, j: (i, 0)),
            pl.BlockSpec((merged.shape[1], bn), lambda i, j: (0, j)),
            pl.BlockSpec((bm, bn), lambda i, j: (i, j)),
        ],
        out_specs=pl.BlockSpec((bm, bn), lambda i, j: (i, j)),
        compiler_params=_cparams(2),
        name="outproj",
    )(merged, w_out, h)


def _final_norm_kernel(x_ref, g_ref, o_ref):
    o_ref[...] = _rmsnorm_f32(x_ref[...], g_ref[...])


def _final_norm(h, g, *, bm):
    t, d = h.shape
    return pl.pallas_call(
        _final_norm_kernel,
        out_shape=jax.ShapeDtypeStruct((t, d), F32),
        grid=(t // bm,),
        in_specs=[pl.BlockSpec((bm, d), lambda i: (i, 0)), pl.BlockSpec((1, d), lambda i: (0, 0))],
        out_specs=pl.BlockSpec((bm, d), lambda i: (i, 0)),
        compiler_params=_cparams(1),
        name="final_norm",
    )(h, g)


def kernel(x, rel_bias, final_norm, ffn1_norm, ffn1_wi, ffn1_wo, mix_norm, w_in, sgu_ln_g, sgu_ln_b, sgu_w, sgu_b, diff_lambda, diff_subln, conv_w, conv_b, dt_bias, a_log, d_skip, ssm_norm, w_pa, w_pb, w_pc, w_out, ffn2_norm, ffn2_wi, ffn2_wo):
    batch, seq, d = x.shape
    depth = ffn1_wi.shape[0]
    t = batch * seq
    heads_c = dt_bias.shape[1]
    a_width = sgu_ln_g.shape[1]
    c_inner = ssm_norm.shape[1]
    conv_ch = conv_w.shape[2]
    d_ff = ffn1_wo.shape[1]

    bm = min(1024, t)
    bf = 512
    bq = 256
    assert t % bm == 0 and d_ff % bf == 0 and seq % bq == 0 and bq % CHUNK == 0
    assert w_in.shape[2] == OFF_GATE + heads_c + 3 * d and conv_ch == OFF_AU

    o_au, o_av, o_q, o_z, o_xbc = 0, a_width, 2 * a_width, 5 * a_width, 5 * a_width + c_inner
    o_dt = o_xbc + conv_ch
    o_g = o_dt + heads_c
    w_main = jnp.concatenate(
        [w_in[:, :, o_xbc:o_dt], w_in[:, :, o_au:o_av], w_in[:, :, o_z:o_xbc], w_in[:, :, o_av:o_q],
         w_in[:, :, o_q:o_z], w_in[:, :, o_g:]], axis=2).astype(BF16)
    w_dt = jnp.pad(w_in[:, :, o_dt:o_g], ((0, 0), (0, 0), (0, LANES - heads_c))).astype(BF16)
    wi1, wo1 = ffn1_wi.astype(BF16), ffn1_wo.astype(BF16)
    wi2, wo2 = ffn2_wi.astype(BF16), ffn2_wo.astype(BF16)
    wpa, wpb, wpc, wout = w_pa.astype(BF16), w_pb.astype(BF16), w_pc.astype(BF16), w_out.astype(BF16)

    bsb = jnp.repeat(jnp.swapaxes(sgu_b, 1, 2), a_width // A_GROUPS, axis=2)
    pad_h = ((0, 0), (0, LANES - heads_c))
    dtb = jnp.pad(dt_bias, pad_h)
    alog = jnp.pad(a_log, pad_h)
    dsk_x = jnp.repeat(d_skip, C_HEAD_DIM, axis=1)
    expand = np.zeros((LANES, c_inner), np.float32)
    expand[np.arange(c_inner) // C_HEAD_DIM, np.arange(c_inner)] = 1.0
    expand = jnp.asarray(expand, BF16)
    buckets = jnp.asarray(_bucket_tiles(bq))

    h = x.reshape(t, d)
    for l in range(depth):
        lam_init = 0.8 - 0.6 * math.exp(-0.3 * l)
        h = _ffn(h, ffn1_norm[l][None], wi1[l], wo1[l], bm=bm, bf=bf)
        proj, dt_raw = _inproj(h, mix_norm[l][None], w_main[l], w_dt[l], bm=bm, bn=1024)
        ya = _sgu(proj, sgu_ln_g[l][None], sgu_ln_b[l][None], sgu_w[l], bsb[l], rows=2 * CHUNK)
        yb = _attn(proj, rel_bias, buckets, diff_lambda[l], diff_subln[l][None],
                   batch=batch, seq=seq, bq=bq, lam_init=lam_init)
        yc = _mamba(proj, dt_raw, conv_w[l], conv_b[l][None], dtb[l][None], alog[l][None],
                    dsk_x[l][None], ssm_norm[l][None], expand, batch=batch, seq=seq)
        merged = _merge(ya, yb, yc, proj, wpa[l], wpb[l], wpc[l], bm=bm, bn=512)
        h = _outproj(merged, wout[l], h, bm=bm, bn=1024)
        h = _ffn(h, ffn2_norm[l][None], wi2[l], wo2[l], bm=bm, bf=bf)
    return _final_norm(h, final_norm[None], bm=bm).reshape(batch, seq, d)
```

```python
import functools
import math

import numpy as np
import jax
import jax.numpy as jnp
from jax import lax
from jax.experimental import pallas as pl
from jax.experimental.pallas import tpu as pltpu

F32 = jnp.float32
BF16 = jnp.bfloat16

EPS = 1e-6
CHUNK = 128
LANES = 128
A_GROUPS = 8
B_HEAD_DIM = 64
C_HEAD_DIM = 64
C_GROUPS = 4
C_STATE = 128
C_CONV = 4
REL_BUCKETS = 32
REL_MAX_DIST = 128
REL_EXACT = REL_BUCKETS // 2
NEG_BIG = -1e30
LOG2E = math.log2(math.e)
VMEM_LIMIT_V7X = 56 * 1024 * 1024

OFF_XBC, OFF_AU, OFF_Z, OFF_AV, OFF_Q, OFF_K, OFF_V, OFF_GATE = 0, 3072, 4096, 6144, 7168, 8192, 9216, 10240


def _cparams(n_axes):
    return pltpu.CompilerParams(dimension_semantics=("arbitrary",) * n_axes,
                                vmem_limit_bytes=VMEM_LIMIT_V7X)


def _rmsnorm_f32(x, g):
    ms = jnp.mean(x * x, axis=-1, keepdims=True)
    return (x * lax.rsqrt(ms + EPS)) * g


def _ffn_kernel(x_ref, g_ref, wg_ref, wu_ref, wo_ref, o_ref, xn_ref):
    j = pl.program_id(1)

    @pl.when(j == 0)
    def _():
        xn_ref[...] = _rmsnorm_f32(x_ref[...], g_ref[...]).astype(BF16)
        o_ref[...] = jnp.zeros_like(o_ref)

    xn = xn_ref[...]
    gate = jnp.dot(xn, wg_ref[...], preferred_element_type=F32)
    up = jnp.dot(xn, wu_ref[...], preferred_element_type=F32)
    act = (gate * jax.nn.sigmoid(gate) * up).astype(BF16)
    o_ref[...] += jnp.dot(act, wo_ref[...], preferred_element_type=F32)

    @pl.when(j == pl.num_programs(1) - 1)
    def _():
        o_ref[...] = x_ref[...] + 0.5 * o_ref[...]


def _ffn(h, g, wi, wo, layer, *, bm, bf):
    t, d = h.shape
    d_ff = wo.shape[1]
    nj = d_ff // bf
    return pl.pallas_call(
        _ffn_kernel,
        out_shape=jax.ShapeDtypeStruct((t, d), F32),
        grid=(t // bm, nj),
        in_specs=[
            pl.BlockSpec((bm, d), lambda i, j: (i, 0)),
            pl.BlockSpec((1, d), lambda i, j: (0, 0)),
            pl.BlockSpec((None, d, bf), lambda i, j: (layer, 0, j)),
            pl.BlockSpec((None, d, bf), lambda i, j: (layer, 0, j + nj)),
            pl.BlockSpec((None, bf, d), lambda i, j: (layer, j, 0)),
        ],
        out_specs=pl.BlockSpec((bm, d), lambda i, j: (i, 0)),
        scratch_shapes=[pltpu.VMEM((bm, d), BF16)],
        compiler_params=_cparams(2),
        name="ffn",
    )(h, g, wi, wi, wo)


def _inproj_kernel(x_ref, g_ref, w_ref, wdt_ref, o_ref, dt_ref, xn_ref, *, nj_plain):
    j = pl.program_id(1)

    @pl.when(j == 0)
    def _():
        xn = _rmsnorm_f32(x_ref[...], g_ref[...]).astype(BF16)
        xn_ref[...] = xn
        dt_ref[...] = jnp.dot(xn, wdt_ref[...], preferred_element_type=F32)

    acc = jnp.dot(xn_ref[...], w_ref[...], preferred_element_type=F32)

    @pl.when(j < nj_plain)
    def _():
        o_ref[...] = acc.astype(BF16)

    @pl.when(j >= nj_plain)
    def _():
        o_ref[...] = jax.nn.sigmoid(acc).astype(BF16)


def _inproj(h, g, w_main, w_dt, layer, *, bm, bn):
    t, d = h.shape
    n = w_main.shape[2]
    return pl.pallas_call(
        functools.partial(_inproj_kernel, nj_plain=OFF_GATE // bn),
        out_shape=(jax.ShapeDtypeStruct((t, n), BF16), jax.ShapeDtypeStruct((t, LANES), F32)),
        grid=(t // bm, n // bn),
        in_specs=[
            pl.BlockSpec((bm, d), lambda i, j: (i, 0)),
            pl.BlockSpec((1, d), lambda i, j: (0, 0)),
            pl.BlockSpec((None, d, bn), lambda i, j: (layer, 0, j)),
            pl.BlockSpec((None, d, LANES), lambda i, j: (layer, 0, 0)),
        ],
        out_specs=(pl.BlockSpec((bm, bn), lambda i, j: (i, j)),
                   pl.BlockSpec((bm, LANES), lambda i, j: (i, 0))),
        scratch_shapes=[pltpu.VMEM((bm, d), BF16)],
        compiler_params=_cparams(2),
        name="inproj",
    )(h, g, w_main, w_dt)


def _sgu_kernel(u_ref, v_ref, lng_ref, lnb_ref, w_ref, bsb_ref, o_ref):
    rows = u_ref.shape[0]
    u = jax.nn.gelu(u_ref[...].astype(F32))
    v = jax.nn.gelu(v_ref[...].astype(F32))
    mu = jnp.mean(v, axis=-1, keepdims=True)
    var = jnp.mean(jnp.square(v - mu), axis=-1, keepdims=True)
    vn = (((v - mu) * lax.rsqrt(var + EPS)) * lng_ref[...] + lnb_ref[...]).astype(BF16)
    r_i = lax.broadcasted_iota(jnp.int32, (CHUNK, CHUNK), 0)
    c_i = lax.broadcasted_iota(jnp.int32, (CHUNK, CHUNK), 1)
    tri = r_i >= c_i
    for g in range(A_GROUPS):
        wm = jnp.where(tri, w_ref[g], 0.0).astype(BF16)
        cs = slice(g * LANES, (g + 1) * LANES)
        for c in range(rows // CHUNK):
            rs = slice(c * CHUNK, (c + 1) * CHUNK)
            mixed = jnp.dot(wm, vn[rs, cs], preferred_element_type=F32) + bsb_ref[:, cs]
            o_ref[rs, cs] = (u[rs, cs] * mixed).astype(BF16)


def _sgu(proj, ln_g, ln_b, w_s, bsb, *, rows):
    t = proj.shape[0]
    aw = ln_g.shape[1]
    return pl.pallas_call(
        _sgu_kernel,
        out_shape=jax.ShapeDtypeStruct((t, aw), BF16),
        grid=(t // rows,),
        in_specs=[
            pl.BlockSpec((rows, aw), lambda i: (i, OFF_AU // aw)),
            pl.BlockSpec((rows, aw), lambda i: (i, OFF_AV // aw)),
            pl.BlockSpec((1, aw), lambda i: (0, 0)),
            pl.BlockSpec((1, aw), lambda i: (0, 0)),
            pl.BlockSpec((A_GROUPS, CHUNK, CHUNK), lambda i: (0, 0, 0)),
            pl.BlockSpec((CHUNK, aw), lambda i: (0, 0)),
        ],
        out_specs=pl.BlockSpec((rows, aw), lambda i: (i, 0)),
        compiler_params=_cparams(1),
        name="sgu",
    )(proj, proj, ln_g, ln_b, w_s, bsb)


def _t5_bucket_np(dist):
    n = np.maximum(dist, 0)
    nf = np.maximum(n, 1).astype(np.float64)
    large = REL_EXACT + (np.log(nf / REL_EXACT) / math.log(REL_MAX_DIST / REL_EXACT)
                         * (REL_BUCKETS - REL_EXACT)).astype(np.int32)
    large = np.minimum(large, REL_BUCKETS - 1)
    return np.where(n < REL_EXACT, n, large).astype(np.int32)


def _bucket_tiles(bq, bk):
    last_bucket_from = int(np.argmax(_t5_bucket_np(np.arange(4 * REL_MAX_DIST)) == REL_BUCKETS - 1))
    n_tiles = -(-(bk - 1 + last_bucket_from) // bq)
    c = np.arange(bk)[:, None]
    r = np.arange(bq)[None, :]
    tiles = []
    for t in range(n_tiles):
        dist = t * bq + r - c
        tiles.append(np.where(dist >= 0, _t5_bucket_np(dist), REL_BUCKETS))
    assert _t5_bucket_np(np.array([n_tiles * bq - (bk - 1)]))[0] == REL_BUCKETS - 1
    return np.stack(tiles).astype(np.int32)


BIAS_ROWS = 64


def _bias_tiles_kernel(rb_ref, bkt_ref, o_ref, *, present):
    hd = pl.program_id(0)
    n_bias, bk, bq = o_ref.shape
    for t in range(n_bias - 1):
        for rc in range(bk // BIAS_ROWS):
            rs = slice(rc * BIAS_ROWS, (rc + 1) * BIAS_ROWS)
            bt = bkt_ref[t, rs, :]
            tile = jnp.where(bt == REL_BUCKETS, NEG_BIG, 0.0).astype(F32)
            for b in present[t][rc]:
                tile = jnp.where(bt == b, rb_ref[b, hd] * LOG2E, tile)
            o_ref[t, rs, :] = tile
    o_ref[n_bias - 1] = jnp.full((bk, bq), rb_ref[REL_BUCKETS - 1, hd] * LOG2E, F32)


def _bias_tiles(rel_bias, buckets_np):
    n_tiles, bk, bq = buckets_np.shape
    heads = rel_bias.shape[1]
    present = tuple(tuple(tuple(int(b) for b in np.unique(buckets_np[t, rc * BIAS_ROWS:(rc + 1) * BIAS_ROWS])
                                if b < REL_BUCKETS)
                          for rc in range(bk // BIAS_ROWS)) for t in range(n_tiles))
    return pl.pallas_call(
        functools.partial(_bias_tiles_kernel, present=present),
        out_shape=jax.ShapeDtypeStruct((heads, n_tiles + 1, bk, bq), F32),
        grid=(heads,),
        in_specs=[pl.BlockSpec(memory_space=pltpu.SMEM),
                  pl.BlockSpec((n_tiles, bk, bq), lambda h: (0, 0, 0))],
        out_specs=pl.BlockSpec((None, n_tiles + 1, bk, bq), lambda h: (h, 0, 0, 0)),
        compiler_params=_cparams(1),
        name="bias_tiles",
    )(rel_bias, jnp.asarray(buckets_np))


def _attn_kernel(lam_ref, sub_ref, bias_ref, q_ref, k_ref, v_ref, o_ref,
                 vt_ref, s_ref, p_ref, m_ref, l_ref, al_ref, acc_ref, *, lam_init):
    bq = q_ref.shape[0]
    bk = s_ref.shape[0]
    seq = k_ref.shape[0]
    n_bias = bias_ref.shape[0]
    qi = pl.program_id(2)

    @pl.when(qi == 0)
    def _():
        for c in range(seq // LANES):
            cs = slice(c * LANES, (c + 1) * LANES)
            vt_ref[:, cs] = v_ref[cs, :].astype(F32).T.astype(BF16)

    lane = lax.broadcasted_iota(jnp.int32, (bq, 2 * B_HEAD_DIM), 1)
    qs = (q_ref[...].astype(F32) * (B_HEAD_DIM ** -0.5 * LOG2E)).astype(BF16)
    zero = jnp.zeros_like(qs)
    qq = jnp.concatenate([jnp.where(lane < B_HEAD_DIM, qs, zero),
                          jnp.where(lane >= B_HEAD_DIM, qs, zero)], axis=0)
    n_blocks = (qi * bq) // bk + 1

    def scores(j):
        kb = k_ref[pl.ds(pl.multiple_of(j * bk, bk), bk), :]
        s = lax.dot_general(kb, qq, (((1,), (1,)), ((), ())), preferred_element_type=F32)
        bias = bias_ref[jnp.minimum((qi * bq - j * bk) // bq, n_bias - 1)]
        return s + jnp.concatenate([bias, bias], axis=1)

    def weighted_values(j):
        vt = vt_ref[:, pl.ds(pl.multiple_of(j * bk, bk), bk)]
        acc_ref[...] = al_ref[...] * acc_ref[...] + jnp.dot(vt, p_ref[...], preferred_element_type=F32)

    m_ref[...] = jnp.full(m_ref.shape, NEG_BIG, F32)
    l_ref[...] = jnp.zeros(l_ref.shape, F32)
    al_ref[...] = jnp.ones(al_ref.shape, F32)
    acc_ref[...] = jnp.zeros(acc_ref.shape, F32)
    p_ref[...] = jnp.zeros(p_ref.shape, BF16)
    s_ref[...] = scores(0)

    def body(j, carry):
        weighted_values(jnp.maximum(j - 1, 0))
        s = s_ref[...]
        m_prev = m_ref[...]
        m_next = jnp.maximum(m_prev, jnp.max(s, axis=0, keepdims=True))
        alpha = jnp.exp2(m_prev - m_next)
        p = jnp.exp2(s - m_next)
        l_ref[...] = alpha * l_ref[...] + jnp.sum(p, axis=0, keepdims=True)
        m_ref[...] = m_next
        al_ref[...] = alpha
        p_ref[...] = p.astype(BF16)
        s_ref[...] = scores(jnp.minimum(j + 1, n_blocks - 1))
        return carry

    lax.fori_loop(0, n_blocks, body, 0)
    weighted_values(n_blocks - 1)

    lam_p = lam_ref[...]
    lam = (jnp.exp(jnp.sum(lam_p[0:1] * lam_p[1:2], axis=1, keepdims=True))
           - jnp.exp(jnp.sum(lam_p[2:3] * lam_p[3:4], axis=1, keepdims=True)) + lam_init)
    o = acc_ref[...] * (1.0 / l_ref[...])
    attn = o[:, :bq] - lam * o[:, bq:]
    ms = jnp.mean(attn * attn, axis=0, keepdims=True)
    y = ((attn * lax.rsqrt(ms + EPS)) * sub_ref[...]) * (1.0 - lam_init)
    o_ref[...] = y.T.astype(BF16)


def _attn(proj, bias_tiles, lam_p, subln_t, *, batch, seq, bq, lam_init):
    t = proj.shape[0]
    hw = 2 * B_HEAD_DIM
    heads, n_bias, bk, _ = bias_tiles.shape
    nq = seq // bq
    return pl.pallas_call(
        functools.partial(_attn_kernel, lam_init=lam_init),
        out_shape=jax.ShapeDtypeStruct((t, heads * hw), BF16),
        grid=(batch, heads, nq),
        in_specs=[
            pl.BlockSpec((4, B_HEAD_DIM), lambda b, h, i: (0, 0)),
            pl.BlockSpec((hw, bq), lambda b, h, i: (0, 0)),
            pl.BlockSpec((None, n_bias, bk, bq), lambda b, h, i: (h, 0, 0, 0)),
            pl.BlockSpec((bq, hw), lambda b, h, i: (b * nq + i, OFF_Q // hw + h)),
            pl.BlockSpec((seq, hw), lambda b, h, i: (b, OFF_K // hw + h)),
            pl.BlockSpec((seq, hw), lambda b, h, i: (b, OFF_V // hw + h)),
        ],
        out_specs=pl.BlockSpec((bq, hw), lambda b, h, i: (b * nq + i, h)),
        scratch_shapes=[pltpu.VMEM((hw, seq), BF16),
                        pltpu.VMEM((bk, 2 * bq), F32),
                        pltpu.VMEM((bk, 2 * bq), BF16),
                        pltpu.VMEM((1, 2 * bq), F32), pltpu.VMEM((1, 2 * bq), F32),
                        pltpu.VMEM((1, 2 * bq), F32), pltpu.VMEM((hw, 2 * bq), F32)],
        compiler_params=_cparams(3),
        name="diff_attn",
    )(lam_p, subln_t, bias_tiles, proj, proj, proj)


def _split_dot(x, e_bf16, passes):
    out = None
    r = x
    for _ in range(passes):
        hi = r.astype(BF16)
        part = jnp.dot(hi, e_bf16, preferred_element_type=F32)
        out = part if out is None else out + part
        r = r - hi.astype(F32)
    return out


def _mamba_kernel(xbc_ref, z_ref, dt_ref, cw_ref, cb_ref, dtb_ref, alog_ref, dsk_ref, ng_ref, e_ref,
                  o_ref, xpad_ref, st_ref):
    c_inner = z_ref.shape[1]
    gw = c_inner // C_GROUPS
    c = pl.program_id(1)

    @pl.when(c == 0)
    def _():
        xpad_ref[0:8, :] = jnp.zeros((8, xpad_ref.shape[1]), F32)
        st_ref[...] = jnp.zeros_like(st_ref)

    @pl.when(c > 0)
    def _():
        xpad_ref[0:8, :] = xpad_ref[CHUNK:CHUNK + 8, :]

    xpad_ref[8:8 + CHUNK, :] = xbc_ref[...].astype(F32)
    conv = cb_ref[...]
    for k in range(C_CONV):
        conv = conv + cw_ref[k:k + 1, :] * xpad_ref[pl.ds(8 - (C_CONV - 1) + k, CHUNK), :]
    xbc = conv * jax.nn.sigmoid(conv)
    xs = xbc[:, :c_inner]
    bm = xbc[:, c_inner:c_inner + C_GROUPS * C_STATE]
    cm = xbc[:, c_inner + C_GROUPS * C_STATE:]

    dt = jax.nn.softplus(dt_ref[...] + dtb_ref[...])
    a = dt * (-jnp.exp(alog_ref[...]))
    r_i = lax.broadcasted_iota(jnp.int32, (CHUNK, CHUNK), 0)
    c_i = lax.broadcasted_iota(jnp.int32, (CHUNK, CHUNK), 1)
    tri = r_i >= c_i
    lower = tri.astype(F32)
    upper = (r_i <= c_i).astype(F32)
    hp = lax.Precision.HIGHEST
    a_cs = jnp.dot(lower, a, precision=hp, preferred_element_type=F32)
    a_cs_t = jnp.dot(a.T, upper, precision=hp, preferred_element_type=F32)

    e = e_ref[...]
    dt_x = _split_dot(dt, e, 2)
    acs_x = _split_dot(a_cs, e, 3)
    last_x = acs_x[CHUNK - 1:CHUNK, :]
    xd = xs * dt_x
    xd_b = xd.astype(BF16)
    xdd_b = (xd * jnp.exp(last_x - acs_x)).astype(BF16)
    prev = st_ref[...]
    prev_b = prev.astype(BF16)
    eacs = jnp.exp(acs_x)
    lane_lo = lax.broadcasted_iota(jnp.int32, (CHUNK, LANES), 1) < C_HEAD_DIM
    heads_per_group = gw // C_HEAD_DIM

    y_parts = []
    st_parts = []
    for g in range(C_GROUPS):
        gs = slice(g * gw, (g + 1) * gw)
        b_g = bm[:, g * C_STATE:(g + 1) * C_STATE]
        c_g = cm[:, g * C_STATE:(g + 1) * C_STATE].astype(BF16)
        cb = lax.dot_general(c_g, b_g.astype(BF16), (((1,), (1,)), ((), ())), preferred_element_type=F32)
        st_parts.append(jnp.dot(b_g.T.astype(BF16), xdd_b[:, gs], preferred_element_type=F32))
        y_off = jnp.dot(c_g, prev_b[:, gs], preferred_element_type=F32) * eacs[:, gs]
        tiles = []
        for pr in range(heads_per_group // 2):
            t_idx = g * (heads_per_group // 2) + pr
            xt = xd_b[:, t_idx * LANES:(t_idx + 1) * LANES]
            res = []
            for half in range(2):
                hd = 2 * t_idx + half
                seg = a_cs[:, hd:hd + 1] - a_cs_t[hd:hd + 1, :]
                decay = jnp.where(tri, jnp.exp(jnp.minimum(seg, 0.0)), 0.0)
                res.append(jnp.dot((cb * decay).astype(BF16), xt, preferred_element_type=F32))
            tiles.append(jnp.where(lane_lo, res[0], res[1]))
        y_parts.append(jnp.concatenate(tiles, axis=1) + y_off)

    st_ref[...] = prev * jnp.exp(last_x) + jnp.concatenate(st_parts, axis=1)
    y = jnp.concatenate(y_parts, axis=1) + dsk_ref[...] * xs
    zf = z_ref[...].astype(F32)
    gated = y * (zf * jax.nn.sigmoid(zf))
    o_ref[...] = _rmsnorm_f32(gated, ng_ref[...]).astype(BF16)


def _mamba(proj, dt_raw, conv_w, conv_b, dt_bias, a_log, d_skip_x, norm_g, expand, *, batch, seq):
    t = proj.shape[0]
    c_inner = norm_g.shape[1]
    conv_ch = conv_w.shape[1]
    nc = seq // CHUNK
    row = lambda b, c: (b * nc + c, 0)
    const = lambda b, c: (0, 0)
    return pl.pallas_call(
        _mamba_kernel,
        out_shape=jax.ShapeDtypeStruct((t, c_inner), BF16),
        grid=(batch, nc),
        in_specs=[
            pl.BlockSpec((CHUNK, conv_ch), lambda b, c: (b * nc + c, OFF_XBC // conv_ch)),
            pl.BlockSpec((CHUNK, c_inner), lambda b, c: (b * nc + c, OFF_Z // c_inner)),
            pl.BlockSpec((CHUNK, LANES), row),
            pl.BlockSpec((C_CONV, conv_ch), const),
            pl.BlockSpec((1, conv_ch), const),
            pl.BlockSpec((1, LANES), const),
            pl.BlockSpec((1, LANES), const),
            pl.BlockSpec((1, c_inner), const),
            pl.BlockSpec((1, c_inner), const),
            pl.BlockSpec((LANES, c_inner), const),
        ],
        out_specs=pl.BlockSpec((CHUNK, c_inner), row),
        scratch_shapes=[pltpu.VMEM((CHUNK + 8, conv_ch), F32), pltpu.VMEM((C_STATE, c_inner), F32)],
        compiler_params=_cparams(2),
        name="mamba_ssd",
    )(proj, proj, dt_raw, conv_w, conv_b, dt_bias, a_log, d_skip_x, norm_g, expand)


def _merge_kernel(ya_ref, yb_ref, yc_ref, g0_ref, g1_ref, g2_ref, wa_ref, wb_ref, wc_ref, o_ref):
    pa = jnp.dot(ya_ref[...], wa_ref[...], preferred_element_type=F32)
    pb = jnp.dot(yb_ref[...], wb_ref[...], preferred_element_type=F32)
    pc = jnp.dot(yc_ref[...], wc_ref[...], preferred_element_type=F32)
    o_ref[...] = (g0_ref[...].astype(F32) * pa + g1_ref[...].astype(F32) * pb
                  + g2_ref[...].astype(F32) * pc).astype(BF16)


def _merge(ya, yb, yc, proj, w_pa, w_pb, w_pc, layer, *, bm, bn):
    t = ya.shape[0]
    d = w_pa.shape[2]
    gate_spec = lambda k: pl.BlockSpec((bm, bn), lambda i, j: (i, (OFF_GATE + k * d) // bn + j))
    return pl.pallas_call(
        _merge_kernel,
        out_shape=jax.ShapeDtypeStruct((t, d), BF16),
        grid=(t // bm, d // bn),
        in_specs=[
            pl.BlockSpec((bm, ya.shape[1]), lambda i, j: (i, 0)),
            pl.BlockSpec((bm, yb.shape[1]), lambda i, j: (i, 0)),
            pl.BlockSpec((bm, yc.shape[1]), lambda i, j: (i, 0)),
            gate_spec(0), gate_spec(1), gate_spec(2),
            pl.BlockSpec((None, w_pa.shape[1], bn), lambda i, j: (layer, 0, j)),
            pl.BlockSpec((None, w_pb.shape[1], bn), lambda i, j: (layer, 0, j)),
            pl.BlockSpec((None, w_pc.shape[1], bn), lambda i, j: (layer, 0, j)),
        ],
        out_specs=pl.BlockSpec((bm, bn), lambda i, j: (i, j)),
        compiler_params=_cparams(2),
        name="merge",
    )(ya, yb, yc, proj, proj, proj, w_pa, w_pb, w_pc)


def _outproj_kernel(m_ref, w_ref, h_ref, o_ref):
    o_ref[...] = h_ref[...] + jnp.dot(m_ref[...], w_ref[...], preferred_element_type=F32)


def _outproj(merged, w_out, h, layer, *, bm, bn):
    t, d = h.shape
    return pl.pallas_call(
        _outproj_kernel,
        out_shape=jax.ShapeDtypeStruct((t, d), F32),
        grid=(t // bm, d // bn),
        in_specs=[
            pl.BlockSpec((bm, merged.shape[1]), lambda i, j: (i, 0)),
            pl.BlockSpec((None, merged.shape[1], bn), lambda i, j: (layer, 0, j)),
            pl.BlockSpec((bm, bn), lambda i, j: (i, j)),
        ],
        out_specs=pl.BlockSpec((bm, bn), lambda i, j: (i, j)),
        compiler_params=_cparams(2),
        name="outproj",
    )(merged, w_out, h)


def _final_norm_kernel(x_ref, g_ref, o_ref):
    o_ref[...] = _rmsnorm_f32(x_ref[...], g_ref[...])


def _final_norm(h, g, *, bm):
    t, d = h.shape
    return pl.pallas_call(
        _final_norm_kernel,
        out_shape=jax.ShapeDtypeStruct((t, d), F32),
        grid=(t // bm,),
        in_specs=[pl.BlockSpec((bm, d), lambda i: (i, 0)), pl.BlockSpec((1, d), lambda i: (0, 0))],
        out_specs=pl.BlockSpec((bm, d), lambda i: (i, 0)),
        compiler_params=_cparams(1),
        name="final_norm",
    )(h, g)


def kernel(x, rel_bias, final_norm, ffn1_norm, ffn1_wi, ffn1_wo, mix_norm, w_in, sgu_ln_g, sgu_ln_b, sgu_w, sgu_b, diff_lambda, diff_subln, conv_w, conv_b, dt_bias, a_log, d_skip, ssm_norm, w_pa, w_pb, w_pc, w_out, ffn2_norm, ffn2_wi, ffn2_wo):
    batch, seq, d = x.shape
    depth = ffn1_wi.shape[0]
    t = batch * seq
    heads_c = dt_bias.shape[1]
    a_width = sgu_ln_g.shape[1]
    c_inner = ssm_norm.shape[1]
    conv_ch = conv_w.shape[2]
    d_ff = ffn1_wo.shape[1]

    bm = min(1024, t)
    bf = 512
    bq = 256
    bk = 512
    assert t % bm == 0 and d_ff % bf == 0 and seq % bk == 0 and bk % bq == 0 and bq % LANES == 0
    assert w_in.shape[2] == OFF_GATE + heads_c + 3 * d and conv_ch == OFF_AU

    o_au, o_av, o_q, o_z, o_xbc = 0, a_width, 2 * a_width, 5 * a_width, 5 * a_width + c_inner
    o_dt = o_xbc + conv_ch
    o_g = o_dt + heads_c
    w_main = jnp.concatenate(
        [w_in[:, :, o_xbc:o_dt], w_in[:, :, o_au:o_av], w_in[:, :, o_z:o_xbc], w_in[:, :, o_av:o_q],
         w_in[:, :, o_q:o_z], w_in[:, :, o_g:]], axis=2).astype(BF16)
    w_dt = jnp.pad(w_in[:, :, o_dt:o_g], ((0, 0), (0, 0), (0, LANES - heads_c))).astype(BF16)
    wi1, wo1 = ffn1_wi.astype(BF16), ffn1_wo.astype(BF16)
    wi2, wo2 = ffn2_wi.astype(BF16), ffn2_wo.astype(BF16)
    wpa, wpb, wpc, wout = w_pa.astype(BF16), w_pb.astype(BF16), w_pc.astype(BF16), w_out.astype(BF16)

    bsb = jnp.repeat(jnp.swapaxes(sgu_b, 1, 2), a_width // A_GROUPS, axis=2)
    pad_h = ((0, 0), (0, LANES - heads_c))
    dtb = jnp.pad(dt_bias, pad_h)
    alog = jnp.pad(a_log, pad_h)
    dsk_x = jnp.repeat(d_skip, C_HEAD_DIM, axis=1)
    expand = np.zeros((LANES, c_inner), np.float32)
    expand[np.arange(c_inner) // C_HEAD_DIM, np.arange(c_inner)] = 1.0
    expand = jnp.asarray(expand, BF16)
    bias_tiles = _bias_tiles(rel_bias, _bucket_tiles(bq, bk))
    subln_t = jnp.broadcast_to(diff_subln[:, :, None], diff_subln.shape + (bq,))

    h = x.reshape(t, d)
    for l in range(depth):
        lam_init = 0.8 - 0.6 * math.exp(-0.3 * l)
        h = _ffn(h, ffn1_norm[l][None], wi1, wo1, l, bm=bm, bf=bf)
        proj, dt_raw = _inproj(h, mix_norm[l][None], w_main, w_dt, l, bm=bm, bn=1024)
        ya = _sgu(proj, sgu_ln_g[l][None], sgu_ln_b[l][None], sgu_w[l], bsb[l], rows=2 * CHUNK)
        yb = _attn(proj, bias_tiles, diff_lambda[l], subln_t[l],
                   batch=batch, seq=seq, bq=bq, lam_init=lam_init)
        yc = _mamba(proj, dt_raw, conv_w[l], conv_b[l][None], dtb[l][None], alog[l][None],
                    dsk_x[l][None], ssm_norm[l][None], expand, batch=batch, seq=seq)
        merged = _merge(ya, yb, yc, proj, wpa, wpb, wpc, l, bm=bm, bn=512)
        h = _outproj(merged, wout, h, l, bm=bm, bn=1024)
        h = _ffn(h, ffn2_norm[l][None], wi2, wo2, l, bm=bm, bf=bf)
    return _final_norm(h, final_norm[None], bm=bm).reshape(batch, seq, d)
```

```python
import functools
import math

import numpy as np
import jax
import jax.numpy as jnp
from jax import lax
from jax.experimental import pallas as pl
from jax.experimental.pallas import tpu as pltpu

F32 = jnp.float32
BF16 = jnp.bfloat16

EPS = 1e-6
CHUNK = 128
LANES = 128
A_GROUPS = 8
B_HEAD_DIM = 64
C_HEAD_DIM = 64
C_GROUPS = 4
C_STATE = 128
C_CONV = 4
REL_BUCKETS = 32
REL_MAX_DIST = 128
REL_EXACT = REL_BUCKETS // 2
NEG_BIG = -1e30
LOG2E = math.log2(math.e)
VMEM_LIMIT_V7X = 56 * 1024 * 1024

OFF_XBC, OFF_AU, OFF_Z, OFF_AV, OFF_Q, OFF_K, OFF_V, OFF_GATE = 0, 3072, 4096, 6144, 7168, 8192, 9216, 10240


def _cparams(n_axes):
    return pltpu.CompilerParams(dimension_semantics=("arbitrary",) * n_axes,
                                vmem_limit_bytes=VMEM_LIMIT_V7X)


def _rmsnorm_f32(x, g):
    ms = jnp.mean(x * x, axis=-1, keepdims=True)
    return (x * lax.rsqrt(ms + EPS)) * g


def _ffn_kernel(x_ref, g_ref, wg_ref, wu_ref, wo_ref, o_ref, xn_ref):
    j = pl.program_id(1)

    @pl.when(j == 0)
    def _():
        xn_ref[...] = _rmsnorm_f32(x_ref[...], g_ref[...]).astype(BF16)
        o_ref[...] = jnp.zeros_like(o_ref)

    xn = xn_ref[...]
    gate = jnp.dot(xn, wg_ref[...], preferred_element_type=F32)
    up = jnp.dot(xn, wu_ref[...], preferred_element_type=F32)
    act = (gate * jax.nn.sigmoid(gate) * up).astype(BF16)
    o_ref[...] += jnp.dot(act, wo_ref[...], preferred_element_type=F32)

    @pl.when(j == pl.num_programs(1) - 1)
    def _():
        o_ref[...] = x_ref[...] + 0.5 * o_ref[...]


def _ffn(h, g, wi, wo, layer, *, bm, bf):
    t, d = h.shape
    d_ff = wo.shape[1]
    nj = d_ff // bf
    return pl.pallas_call(
        _ffn_kernel,
        out_shape=jax.ShapeDtypeStruct((t, d), F32),
        grid=(t // bm, nj),
        in_specs=[
            pl.BlockSpec((bm, d), lambda i, j: (i, 0)),
            pl.BlockSpec((1, d), lambda i, j: (0, 0)),
            pl.BlockSpec((None, d, bf), lambda i, j: (layer, 0, j)),
            pl.BlockSpec((None, d, bf), lambda i, j: (layer, 0, j + nj)),
            pl.BlockSpec((None, bf, d), lambda i, j: (layer, j, 0)),
        ],
        out_specs=pl.BlockSpec((bm, d), lambda i, j: (i, 0)),
        scratch_shapes=[pltpu.VMEM((bm, d), BF16)],
        compiler_params=_cparams(2),
        name="ffn",
    )(h, g, wi, wi, wo)


def _inproj_kernel(x_ref, g_ref, w_ref, wdt_ref, o_ref, dt_ref, xn_ref, *, nj_plain):
    j = pl.program_id(1)

    @pl.when(j == 0)
    def _():
        xn = _rmsnorm_f32(x_ref[...], g_ref[...]).astype(BF16)
        xn_ref[...] = xn
        dt_ref[...] = jnp.dot(xn, wdt_ref[...], preferred_element_type=F32)

    acc = jnp.dot(xn_ref[...], w_ref[...], preferred_element_type=F32)

    @pl.when(j < nj_plain)
    def _():
        o_ref[...] = acc.astype(BF16)

    @pl.when(j >= nj_plain)
    def _():
        o_ref[...] = jax.nn.sigmoid(acc).astype(BF16)


def _inproj(h, g, w_main, w_dt, layer, *, bm, bn):
    t, d = h.shape
    n = w_main.shape[2]
    return pl.pallas_call(
        functools.partial(_inproj_kernel, nj_plain=OFF_GATE // bn),
        out_shape=(jax.ShapeDtypeStruct((t, n), BF16), jax.ShapeDtypeStruct((t, LANES), F32)),
        grid=(t // bm, n // bn),
        in_specs=[
            pl.BlockSpec((bm, d), lambda i, j: (i, 0)),
            pl.BlockSpec((1, d), lambda i, j: (0, 0)),
            pl.BlockSpec((None, d, bn), lambda i, j: (layer, 0, j)),
            pl.BlockSpec((None, d, LANES), lambda i, j: (layer, 0, 0)),
        ],
        out_specs=(pl.BlockSpec((bm, bn), lambda i, j: (i, j)),
                   pl.BlockSpec((bm, LANES), lambda i, j: (i, 0))),
        scratch_shapes=[pltpu.VMEM((bm, d), BF16)],
        compiler_params=_cparams(2),
        name="inproj",
    )(h, g, w_main, w_dt)


def _sgu_kernel(u_ref, v_ref, lng_ref, lnb_ref, w_ref, bsb_ref, o_ref):
    rows = u_ref.shape[0]
    u = jax.nn.gelu(u_ref[...].astype(F32))
    v = jax.nn.gelu(v_ref[...].astype(F32))
    mu = jnp.mean(v, axis=-1, keepdims=True)
    var = jnp.mean(jnp.square(v - mu), axis=-1, keepdims=True)
    vn = (((v - mu) * lax.rsqrt(var + EPS)) * lng_ref[...] + lnb_ref[...]).astype(BF16)
    r_i = lax.broadcasted_iota(jnp.int32, (CHUNK, CHUNK), 0)
    c_i = lax.broadcasted_iota(jnp.int32, (CHUNK, CHUNK), 1)
    tri = r_i >= c_i
    for g in range(A_GROUPS):
        wm = jnp.where(tri, w_ref[g], 0.0).astype(BF16)
        cs = slice(g * LANES, (g + 1) * LANES)
        for c in range(rows // CHUNK):
            rs = slice(c * CHUNK, (c + 1) * CHUNK)
            mixed = jnp.dot(wm, vn[rs, cs], preferred_element_type=F32) + bsb_ref[:, cs]
            o_ref[rs, cs] = (u[rs, cs] * mixed).astype(BF16)


def _sgu(proj, ln_g, ln_b, w_s, bsb, *, rows):
    t = proj.shape[0]
    aw = ln_g.shape[1]
    return pl.pallas_call(
        _sgu_kernel,
        out_shape=jax.ShapeDtypeStruct((t, aw), BF16),
        grid=(t // rows,),
        in_specs=[
            pl.BlockSpec((rows, aw), lambda i: (i, OFF_AU // aw)),
            pl.BlockSpec((rows, aw), lambda i: (i, OFF_AV // aw)),
            pl.BlockSpec((1, aw), lambda i: (0, 0)),
            pl.BlockSpec((1, aw), lambda i: (0, 0)),
            pl.BlockSpec((A_GROUPS, CHUNK, CHUNK), lambda i: (0, 0, 0)),
            pl.BlockSpec((CHUNK, aw), lambda i: (0, 0)),
        ],
        out_specs=pl.BlockSpec((rows, aw), lambda i: (i, 0)),
        compiler_params=_cparams(1),
        name="sgu",
    )(proj, proj, ln_g, ln_b, w_s, bsb)


def _t5_bucket_np(dist):
    n = np.maximum(dist, 0)
    nf = np.maximum(n, 1).astype(np.float64)
    large = REL_EXACT + (np.log(nf / REL_EXACT) / math.log(REL_MAX_DIST / REL_EXACT)
                         * (REL_BUCKETS - REL_EXACT)).astype(np.int32)
    large = np.minimum(large, REL_BUCKETS - 1)
    return np.where(n < REL_EXACT, n, large).astype(np.int32)


def _bucket_tiles(bq, bk):
    last_bucket_from = int(np.argmax(_t5_bucket_np(np.arange(4 * REL_MAX_DIST)) == REL_BUCKETS - 1))
    n_tiles = -(-(bk - 1 + last_bucket_from) // bq)
    c = np.arange(bk)[:, None]
    r = np.arange(bq)[None, :]
    tiles = []
    for t in range(n_tiles):
        dist = t * bq + r - c
        tiles.append(np.where(dist >= 0, _t5_bucket_np(dist), REL_BUCKETS))
    assert _t5_bucket_np(np.array([n_tiles * bq - (bk - 1)]))[0] == REL_BUCKETS - 1
    return np.stack(tiles).astype(np.int32)


BIAS_ROWS = 64


def _bias_tiles_kernel(rb_ref, bkt_ref, o_ref, *, present):
    hd = pl.program_id(0)
    n_bias, bk, bq = o_ref.shape
    for t in range(n_bias - 1):
        for rc in range(bk // BIAS_ROWS):
            rs = slice(rc * BIAS_ROWS, (rc + 1) * BIAS_ROWS)
            bt = bkt_ref[t, rs, :]
            tile = jnp.where(bt == REL_BUCKETS, NEG_BIG, 0.0).astype(F32)
            for b in present[t][rc]:
                tile = jnp.where(bt == b, rb_ref[b, hd] * LOG2E, tile)
            o_ref[t, rs, :] = tile
    o_ref[n_bias - 1] = jnp.full((bk, bq), rb_ref[REL_BUCKETS - 1, hd] * LOG2E, F32)


def _bias_tiles(rel_bias, buckets_np):
    n_tiles, bk, bq = buckets_np.shape
    heads = rel_bias.shape[1]
    present = tuple(tuple(tuple(int(b) for b in np.unique(buckets_np[t, rc * BIAS_ROWS:(rc + 1) * BIAS_ROWS])
                                if b < REL_BUCKETS)
                          for rc in range(bk // BIAS_ROWS)) for t in range(n_tiles))
    return pl.pallas_call(
        functools.partial(_bias_tiles_kernel, present=present),
        out_shape=jax.ShapeDtypeStruct((heads, n_tiles + 1, bk, bq), F32),
        grid=(heads,),
        in_specs=[pl.BlockSpec(memory_space=pltpu.SMEM),
                  pl.BlockSpec((n_tiles, bk, bq), lambda h: (0, 0, 0))],
        out_specs=pl.BlockSpec((None, n_tiles + 1, bk, bq), lambda h: (h, 0, 0, 0)),
        compiler_params=_cparams(1),
        name="bias_tiles",
    )(rel_bias, jnp.asarray(buckets_np))


def _attn_kernel(lam_ref, sub_ref, bias_ref, q_ref, k_ref, v_ref, o_ref,
                 vt_ref, qqt_ref, s_ref, p_ref, m_ref, l_ref, al_ref, acc_ref, *, lam_init):
    bq = q_ref.shape[0]
    bk = s_ref.shape[0]
    seq = k_ref.shape[0]
    n_bias = bias_ref.shape[0]
    qi = pl.program_id(2)

    @pl.when(qi == 0)
    def _():
        for c in range(seq // LANES):
            cs = slice(c * LANES, (c + 1) * LANES)
            vt_ref[:, cs] = v_ref[cs, :].astype(F32).T.astype(BF16)

    qt = (q_ref[...].astype(F32) * (B_HEAD_DIM ** -0.5 * LOG2E)).T
    dim = lax.broadcasted_iota(jnp.int32, qt.shape, 0)
    qqt_ref[...] = jnp.concatenate([jnp.where(dim < B_HEAD_DIM, qt, 0.0),
                                    jnp.where(dim >= B_HEAD_DIM, qt, 0.0)], axis=1).astype(BF16)
    n_blocks = (qi * bq) // bk + 1

    def scores(j):
        kb = k_ref[pl.ds(pl.multiple_of(j * bk, bk), bk), :]
        s = jnp.dot(kb, qqt_ref[...], preferred_element_type=F32)
        bias = bias_ref[jnp.minimum((qi * bq - j * bk) // bq, n_bias - 1)]
        return s + jnp.concatenate([bias, bias], axis=1)

    def weighted_values(j):
        vt = vt_ref[:, pl.ds(pl.multiple_of(j * bk, bk), bk)]
        acc_ref[...] = al_ref[...] * acc_ref[...] + jnp.dot(vt, p_ref[...], preferred_element_type=F32)

    m_ref[...] = jnp.full(m_ref.shape, NEG_BIG, F32)
    l_ref[...] = jnp.zeros(l_ref.shape, F32)
    al_ref[...] = jnp.ones(al_ref.shape, F32)
    acc_ref[...] = jnp.zeros(acc_ref.shape, F32)
    p_ref[...] = jnp.zeros(p_ref.shape, BF16)
    s_ref[...] = scores(0)

    def body(j, carry):
        weighted_values(jnp.maximum(j - 1, 0))
        s = s_ref[...]
        m_prev = m_ref[...]
        m_next = jnp.maximum(m_prev, jnp.max(s, axis=0, keepdims=True))
        alpha = jnp.exp2(m_prev - m_next)
        p = jnp.exp2(s - m_next)
        l_ref[...] = alpha * l_ref[...] + jnp.sum(p, axis=0, keepdims=True)
        m_ref[...] = m_next
        al_ref[...] = alpha
        p_ref[...] = p.astype(BF16)
        s_ref[...] = scores(jnp.minimum(j + 1, n_blocks - 1))
        return carry

    lax.fori_loop(0, n_blocks, body, 0)
    weighted_values(n_blocks - 1)

    lam_p = lam_ref[...]
    lam = (jnp.exp(jnp.sum(lam_p[0:1] * lam_p[1:2], axis=1, keepdims=True))
           - jnp.exp(jnp.sum(lam_p[2:3] * lam_p[3:4], axis=1, keepdims=True)) + lam_init)
    o = acc_ref[...] * (1.0 / l_ref[...])
    attn = o[:, :bq] - lam * o[:, bq:]
    ms = jnp.mean(attn * attn, axis=0, keepdims=True)
    y = ((attn * lax.rsqrt(ms + EPS)) * sub_ref[...]) * (1.0 - lam_init)
    o_ref[...] = y.T.astype(BF16)


def _attn(proj, bias_tiles, lam_p, subln_t, *, batch, seq, bq, lam_init):
    t = proj.shape[0]
    hw = 2 * B_HEAD_DIM
    heads, n_bias, bk, _ = bias_tiles.shape
    nq = seq // bq
    return pl.pallas_call(
        functools.partial(_attn_kernel, lam_init=lam_init),
        out_shape=jax.ShapeDtypeStruct((t, heads * hw), BF16),
        grid=(batch, heads, nq),
        in_specs=[
            pl.BlockSpec((4, B_HEAD_DIM), lambda b, h, i: (0, 0)),
            pl.BlockSpec((hw, bq), lambda b, h, i: (0, 0)),
            pl.BlockSpec((None, n_bias, bk, bq), lambda b, h, i: (h, 0, 0, 0)),
            pl.BlockSpec((bq, hw), lambda b, h, i: (b * nq + i, OFF_Q // hw + h)),
            pl.BlockSpec((seq, hw), lambda b, h, i: (b, OFF_K // hw + h)),
            pl.BlockSpec((seq, hw), lambda b, h, i: (b, OFF_V // hw + h)),
        ],
        out_specs=pl.BlockSpec((bq, hw), lambda b, h, i: (b * nq + i, h)),
        scratch_shapes=[pltpu.VMEM((hw, seq), BF16),
                        pltpu.VMEM((hw, 2 * bq), BF16),
                        pltpu.VMEM((bk, 2 * bq), F32),
                        pltpu.VMEM((bk, 2 * bq), BF16),
                        pltpu.VMEM((1, 2 * bq), F32), pltpu.VMEM((1, 2 * bq), F32),
                        pltpu.VMEM((1, 2 * bq), F32), pltpu.VMEM((hw, 2 * bq), F32)],
        compiler_params=_cparams(3),
        name="diff_attn",
    )(lam_p, subln_t, bias_tiles, proj, proj, proj)


def _split_dot(x, e_bf16, passes):
    out = None
    r = x
    for _ in range(passes):
        hi = r.astype(BF16)
        part = jnp.dot(hi, e_bf16, preferred_element_type=F32)
        out = part if out is None else out + part
        r = r - hi.astype(F32)
    return out


def _mamba_kernel(xbc_ref, z_ref, dt_ref, cw_ref, cb_ref, dtb_ref, alog_ref, dsk_ref, ng_ref, e_ref,
                  o_ref, tail_ref, st_ref):
    c_inner = z_ref.shape[1]
    gw = c_inner // C_GROUPS
    c = pl.program_id(1)

    @pl.when(c == 0)
    def _():
        tail_ref[...] = jnp.zeros_like(tail_ref)
        st_ref[...] = jnp.zeros_like(st_ref)

    r_i = lax.broadcasted_iota(jnp.int32, (CHUNK, CHUNK), 0)
    c_i = lax.broadcasted_iota(jnp.int32, (CHUNK, CHUNK), 1)
    x_cur = xbc_ref[...]
    shift = jnp.concatenate([(r_i - c_i == d).astype(BF16) for d in range(C_CONV - 1, 0, -1)], axis=0)
    delayed = jnp.dot(shift, x_cur, preferred_element_type=F32)
    tail = tail_ref[8:16, :]
    row8 = lax.broadcasted_iota(jnp.int32, tail.shape, 0)
    conv = cb_ref[...] + cw_ref[C_CONV - 1:C_CONV, :] * x_cur.astype(F32)
    conv_top = jnp.zeros_like(tail)
    for k in range(C_CONV - 1):
        d = C_CONV - 1 - k
        conv = conv + cw_ref[k:k + 1, :] * delayed[k * CHUNK:(k + 1) * CHUNK]
        conv_top = conv_top + cw_ref[k:k + 1, :] * jnp.where(row8 < d, pltpu.roll(tail, d, 0), 0.0)
    conv = jnp.concatenate([conv[0:8] + conv_top, conv[8:]], axis=0)
    tail_ref[...] = xbc_ref[CHUNK - 16:CHUNK, :].astype(F32)
    xbc = conv * jax.nn.sigmoid(conv)
    xs = xbc[:, :c_inner]
    bm = xbc[:, c_inner:c_inner + C_GROUPS * C_STATE]
    cm = xbc[:, c_inner + C_GROUPS * C_STATE:]

    dt = jax.nn.softplus(dt_ref[...] + dtb_ref[...])
    a = dt * (-jnp.exp(alog_ref[...]))
    tri = r_i >= c_i
    lower = tri.astype(F32)
    upper = (r_i <= c_i).astype(F32)
    hp = lax.Precision.HIGHEST
    a_cs = jnp.dot(lower, a, precision=hp, preferred_element_type=F32)
    a_cs_t = jnp.dot(a.T, upper, precision=hp, preferred_element_type=F32)

    e = e_ref[...]
    dt_x = _split_dot(dt, e, 2)
    acs_x = _split_dot(a_cs, e, 3)
    last_x = acs_x[CHUNK - 1:CHUNK, :]
    xd = xs * dt_x
    xd_b = xd.astype(BF16)
    xdd_b = (xd * jnp.exp(last_x - acs_x)).astype(BF16)
    prev = st_ref[...]
    prev_b = prev.astype(BF16)
    eacs = jnp.exp(acs_x)
    lane_lo = lax.broadcasted_iota(jnp.int32, (CHUNK, LANES), 1) < C_HEAD_DIM
    heads_per_group = gw // C_HEAD_DIM

    y_parts = []
    st_parts = []
    for g in range(C_GROUPS):
        gs = slice(g * gw, (g + 1) * gw)
        b_g = bm[:, g * C_STATE:(g + 1) * C_STATE]
        c_g = cm[:, g * C_STATE:(g + 1) * C_STATE].astype(BF16)
        cb = lax.dot_general(c_g, b_g.astype(BF16), (((1,), (1,)), ((), ())), preferred_element_type=F32)
        st_parts.append(jnp.dot(b_g.T.astype(BF16), xdd_b[:, gs], preferred_element_type=F32))
        y_off = jnp.dot(c_g, prev_b[:, gs], preferred_element_type=F32) * eacs[:, gs]
        tiles = []
        for pr in range(heads_per_group // 2):
            t_idx = g * (heads_per_group // 2) + pr
            xt = xd_b[:, t_idx * LANES:(t_idx + 1) * LANES]
            res = []
            for half in range(2):
                hd = 2 * t_idx + half
                seg = a_cs[:, hd:hd + 1] - a_cs_t[hd:hd + 1, :]
                decay = jnp.where(tri, jnp.exp(jnp.minimum(seg, 0.0)), 0.0)
                res.append(jnp.dot((cb * decay).astype(BF16), xt, preferred_element_type=F32))
            tiles.append(jnp.where(lane_lo, res[0], res[1]))
        y_parts.append(jnp.concatenate(tiles, axis=1) + y_off)

    st_ref[...] = prev * jnp.exp(last_x) + jnp.concatenate(st_parts, axis=1)
    y = jnp.concatenate(y_parts, axis=1) + dsk_ref[...] * xs
    zf = z_ref[...].astype(F32)
    gated = y * (zf * jax.nn.sigmoid(zf))
    o_ref[...] = _rmsnorm_f32(gated, ng_ref[...]).astype(BF16)


def _mamba(proj, dt_raw, conv_w, conv_b, dt_bias, a_log, d_skip_x, norm_g, expand, *, batch, seq):
    t = proj.shape[0]
    c_inner = norm_g.shape[1]
    conv_ch = conv_w.shape[1]
    nc = seq // CHUNK
    row = lambda b, c: (b * nc + c, 0)
    const = lambda b, c: (0, 0)
    return pl.pallas_call(
        _mamba_kernel,
        out_shape=jax.ShapeDtypeStruct((t, c_inner), BF16),
        grid=(batch, nc),
        in_specs=[
            pl.BlockSpec((CHUNK, conv_ch), lambda b, c: (b * nc + c, OFF_XBC // conv_ch)),
            pl.BlockSpec((CHUNK, c_inner), lambda b, c: (b * nc + c, OFF_Z // c_inner)),
            pl.BlockSpec((CHUNK, LANES), row),
            pl.BlockSpec((C_CONV, conv_ch), const),
            pl.BlockSpec((1, conv_ch), const),
            pl.BlockSpec((1, LANES), const),
            pl.BlockSpec((1, LANES), const),
            pl.BlockSpec((1, c_inner), const),
            pl.BlockSpec((1, c_inner), const),
            pl.BlockSpec((LANES, c_inner), const),
        ],
        out_specs=pl.BlockSpec((CHUNK, c_inner), row),
        scratch_shapes=[pltpu.VMEM((16, conv_ch), F32), pltpu.VMEM((C_STATE, c_inner), F32)],
        compiler_params=_cparams(2),
        name="mamba_ssd",
    )(proj, proj, dt_raw, conv_w, conv_b, dt_bias, a_log, d_skip_x, norm_g, expand)


def _merge_kernel(ya_ref, yb_ref, yc_ref, g0_ref, g1_ref, g2_ref, wa_ref, wb_ref, wc_ref, o_ref):
    pa = jnp.dot(ya_ref[...], wa_ref[...], preferred_element_type=F32)
    pb = jnp.dot(yb_ref[...], wb_ref[...], preferred_element_type=F32)
    pc = jnp.dot(yc_ref[...], wc_ref[...], preferred_element_type=F32)
    o_ref[...] = (g0_ref[...].astype(F32) * pa + g1_ref[...].astype(F32) * pb
                  + g2_ref[...].astype(F32) * pc).astype(BF16)


def _merge(ya, yb, yc, proj, w_pa, w_pb, w_pc, layer, *, bm, bn):
    t = ya.shape[0]
    d = w_pa.shape[2]
    gate_spec = lambda k: pl.BlockSpec((bm, bn), lambda i, j: (i, (OFF_GATE + k * d) // bn + j))
    return pl.pallas_call(
        _merge_kernel,
        out_shape=jax.ShapeDtypeStruct((t, d), BF16),
        grid=(t // bm, d // bn),
        in_specs=[
            pl.BlockSpec((bm, ya.shape[1]), lambda i, j: (i, 0)),
            pl.BlockSpec((bm, yb.shape[1]), lambda i, j: (i, 0)),
            pl.BlockSpec((bm, yc.shape[1]), lambda i, j: (i, 0)),
            gate_spec(0), gate_spec(1), gate_spec(2),
            pl.BlockSpec((None, w_pa.shape[1], bn), lambda i, j: (layer, 0, j)),
            pl.BlockSpec((None, w_pb.shape[1], bn), lambda i, j: (layer, 0, j)),
            pl.BlockSpec((None, w_pc.shape[1], bn), lambda i, j: (layer, 0, j)),
        ],
        out_specs=pl.BlockSpec((bm, bn), lambda i, j: (i, j)),
        compiler_params=_cparams(2),
        name="merge",
    )(ya, yb, yc, proj, proj, proj, w_pa, w_pb, w_pc)


def _outproj_kernel(m_ref, w_ref, h_ref, o_ref):
    o_ref[...] = h_ref[...] + jnp.dot(m_ref[...], w_ref[...], preferred_element_type=F32)


def _outproj(merged, w_out, h, layer, *, bm, bn):
    t, d = h.shape
    return pl.pallas_call(
        _outproj_kernel,
        out_shape=jax.ShapeDtypeStruct((t, d), F32),
        grid=(t // bm, d // bn),
        in_specs=[
            pl.BlockSpec((bm, merged.shape[1]), lambda i, j: (i, 0)),
            pl.BlockSpec((None, merged.shape[1], bn), lambda i, j: (layer, 0, j)),
            pl.BlockSpec((bm, bn), lambda i, j: (i, j)),
        ],
        out_specs=pl.BlockSpec((bm, bn), lambda i, j: (i, j)),
        compiler_params=_cparams(2),
        name="outproj",
    )(merged, w_out, h)


def _final_norm_kernel(x_ref, g_ref, o_ref):
    o_ref[...] = _rmsnorm_f32(x_ref[...], g_ref[...])


def _final_norm(h, g, *, bm):
    t, d = h.shape
    return pl.pallas_call(
        _final_norm_kernel,
        out_shape=jax.ShapeDtypeStruct((t, d), F32),
        grid=(t // bm,),
        in_specs=[pl.BlockSpec((bm, d), lambda i: (i, 0)), pl.BlockSpec((1, d), lambda i: (0, 0))],
        out_specs=pl.BlockSpec((bm, d), lambda i: (i, 0)),
        compiler_params=_cparams(1),
        name="final_norm",
    )(h, g)


def kernel(x, rel_bias, final_norm, ffn1_norm, ffn1_wi, ffn1_wo, mix_norm, w_in, sgu_ln_g, sgu_ln_b, sgu_w, sgu_b, diff_lambda, diff_subln, conv_w, conv_b, dt_bias, a_log, d_skip, ssm_norm, w_pa, w_pb, w_pc, w_out, ffn2_norm, ffn2_wi, ffn2_wo):
    batch, seq, d = x.shape
    depth = ffn1_wi.shape[0]
    t = batch * seq
    heads_c = dt_bias.shape[1]
    a_width = sgu_ln_g.shape[1]
    c_inner = ssm_norm.shape[1]
    conv_ch = conv_w.shape[2]
    d_ff = ffn1_wo.shape[1]

    bm = min(1024, t)
    bf = 512
    bq = 512
    bk = 512
    assert t % bm == 0 and d_ff % bf == 0 and seq % bk == 0 and bk % bq == 0 and bq % LANES == 0
    assert w_in.shape[2] == OFF_GATE + heads_c + 3 * d and conv_ch == OFF_AU

    o_au, o_av, o_q, o_z, o_xbc = 0, a_width, 2 * a_width, 5 * a_width, 5 * a_width + c_inner
    o_dt = o_xbc + conv_ch
    o_g = o_dt + heads_c
    w_main = jnp.concatenate(
        [w_in[:, :, o_xbc:o_dt], w_in[:, :, o_au:o_av], w_in[:, :, o_z:o_xbc], w_in[:, :, o_av:o_q],
         w_in[:, :, o_q:o_z], w_in[:, :, o_g:]], axis=2).astype(BF16)
    w_dt = jnp.pad(w_in[:, :, o_dt:o_g], ((0, 0), (0, 0), (0, LANES - heads_c))).astype(BF16)
    wi1, wo1 = ffn1_wi.astype(BF16), ffn1_wo.astype(BF16)
    wi2, wo2 = ffn2_wi.astype(BF16), ffn2_wo.astype(BF16)
    wpa, wpb, wpc, wout = w_pa.astype(BF16), w_pb.astype(BF16), w_pc.astype(BF16), w_out.astype(BF16)

    bsb = jnp.repeat(jnp.swapaxes(sgu_b, 1, 2), a_width // A_GROUPS, axis=2)
    pad_h = ((0, 0), (0, LANES - heads_c))
    dtb = jnp.pad(dt_bias, pad_h)
    alog = jnp.pad(a_log, pad_h)
    dsk_x = jnp.repeat(d_skip, C_HEAD_DIM, axis=1)
    expand = np.zeros((LANES, c_inner), np.float32)
    expand[np.arange(c_inner) // C_HEAD_DIM, np.arange(c_inner)] = 1.0
    expand = jnp.asarray(expand, BF16)
    bias_tiles = _bias_tiles(rel_bias, _bucket_tiles(bq, bk))
    subln_t = jnp.broadcast_to(diff_subln[:, :, None], diff_subln.shape + (bq,))

    h = x.reshape(t, d)
    for l in range(depth):
        lam_init = 0.8 - 0.6 * math.exp(-0.3 * l)
        h = _ffn(h, ffn1_norm[l][None], wi1, wo1, l, bm=bm, bf=bf)
        proj, dt_raw = _inproj(h, mix_norm[l][None], w_main, w_dt, l, bm=bm, bn=1024)
        ya = _sgu(proj, sgu_ln_g[l][None], sgu_ln_b[l][None], sgu_w[l], bsb[l], rows=2 * CHUNK)
        yb = _attn(proj, bias_tiles, diff_lambda[l], subln_t[l],
                   batch=batch, seq=seq, bq=bq, lam_init=lam_init)
        yc = _mamba(proj, dt_raw, conv_w[l], conv_b[l][None], dtb[l][None], alog[l][None],
                    dsk_x[l][None], ssm_norm[l][None], expand, batch=batch, seq=seq)
        merged = _merge(ya, yb, yc, proj, wpa, wpb, wpc, l, bm=bm, bn=512)
        h = _outproj(merged, wout, h, l, bm=bm, bn=1024)
        h = _ffn(h, ffn2_norm[l][None], wi2, wo2, l, bm=bm, bf=bf)
    return _final_norm(h, final_norm[None], bm=bm).reshape(batch, seq, d)
```

```python
import functools
import math

import numpy as np
import jax
import jax.numpy as jnp
from jax import lax
from jax.experimental import pallas as pl
from jax.experimental.pallas import tpu as pltpu

F32 = jnp.float32
BF16 = jnp.bfloat16

EPS = 1e-6
CHUNK = 128
LANES = 128
BF16_ROWS = 16
A_GROUPS = 8
B_HEAD_DIM = 64
C_HEAD_DIM = 64
C_GROUPS = 4
C_STATE = 128
C_CONV = 4
REL_BUCKETS = 32
REL_MAX_DIST = 128
REL_EXACT = REL_BUCKETS // 2
NEG_BIG = -1e30
LOG2E = math.log2(math.e)
VMEM_LIMIT_V7X = 56 * 1024 * 1024

OFF_XBC, OFF_AU, OFF_Z, OFF_AV, OFF_Q, OFF_K, OFF_V, OFF_GATE = 0, 3072, 4096, 6144, 7168, 8192, 9216, 10240


def _cparams(n_axes):
    return pltpu.CompilerParams(dimension_semantics=("arbitrary",) * n_axes,
                                vmem_limit_bytes=VMEM_LIMIT_V7X)


def _rmsnorm_f32(x, g):
    ms = jnp.mean(x * x, axis=-1, keepdims=True)
    return (x * lax.rsqrt(ms + EPS)) * g


def _ffn_kernel(x_ref, g_ref, wg_ref, wu_ref, wo_ref, o_ref, xn_ref):
    j = pl.program_id(1)

    @pl.when(j == 0)
    def _():
        x = x_ref[...]
        xn_ref[...] = _rmsnorm_f32(x, g_ref[...]).astype(BF16)
        o_ref[...] = x

    xn = xn_ref[...]
    gate = jnp.dot(xn, wg_ref[...], preferred_element_type=F32)
    up = jnp.dot(xn, wu_ref[...], preferred_element_type=F32)
    act = ((0.5 * gate) * jax.nn.sigmoid(gate) * up).astype(BF16)
    o_ref[...] += jnp.dot(act, wo_ref[...], preferred_element_type=F32)


def _ffn(h, g, wi, wo, layer, *, bm, bf):
    t, d = h.shape
    d_ff = wo.shape[1]
    nj = d_ff // bf
    return pl.pallas_call(
        _ffn_kernel,
        out_shape=jax.ShapeDtypeStruct((t, d), F32),
        grid=(t // bm, nj),
        in_specs=[
            pl.BlockSpec((bm, d), lambda i, j: (i, 0)),
            pl.BlockSpec((1, d), lambda i, j: (0, 0)),
            pl.BlockSpec((None, d, bf), lambda i, j: (layer, 0, j)),
            pl.BlockSpec((None, d, bf), lambda i, j: (layer, 0, j + nj)),
            pl.BlockSpec((None, bf, d), lambda i, j: (layer, j, 0)),
        ],
        out_specs=pl.BlockSpec((bm, d), lambda i, j: (i, 0)),
        scratch_shapes=[pltpu.VMEM((bm, d), BF16)],
        compiler_params=_cparams(2),
        name="ffn",
    )(h, g, wi, wi, wo)


def _inproj_kernel(x_ref, g_ref, w_ref, wdt_ref, o_ref, dt_ref, xn_ref, *, nj_plain):
    j = pl.program_id(1)

    @pl.when(j == 0)
    def _():
        xn = _rmsnorm_f32(x_ref[...], g_ref[...]).astype(BF16)
        xn_ref[...] = xn
        dt_ref[...] = jnp.dot(xn, wdt_ref[...], preferred_element_type=F32)

    acc = jnp.dot(xn_ref[...], w_ref[...], preferred_element_type=F32)
    o_ref[...] = jnp.where(j >= nj_plain, jax.nn.sigmoid(acc), acc).astype(BF16)


def _inproj(h, g, w_main, w_dt, layer, *, bm, bn):
    t, d = h.shape
    n = w_main.shape[2]
    return pl.pallas_call(
        functools.partial(_inproj_kernel, nj_plain=OFF_GATE // bn),
        out_shape=(jax.ShapeDtypeStruct((t, n), BF16), jax.ShapeDtypeStruct((t, LANES), F32)),
        grid=(t // bm, n // bn),
        in_specs=[
            pl.BlockSpec((bm, d), lambda i, j: (i, 0)),
            pl.BlockSpec((1, d), lambda i, j: (0, 0)),
            pl.BlockSpec((None, d, bn), lambda i, j: (layer, 0, j)),
            pl.BlockSpec((None, d, LANES), lambda i, j: (layer, 0, 0)),
        ],
        out_specs=(pl.BlockSpec((bm, bn), lambda i, j: (i, j)),
                   pl.BlockSpec((bm, LANES), lambda i, j: (i, 0))),
        scratch_shapes=[pltpu.VMEM((bm, d), BF16)],
        compiler_params=_cparams(2),
        name="inproj",
    )(h, g, w_main, w_dt)


def _sgu_kernel(u_ref, v_ref, lng_ref, lnb_ref, w_ref, bsb_ref, o_ref):
    rows = u_ref.shape[0]
    u = jax.nn.gelu(u_ref[...].astype(F32))
    v = jax.nn.gelu(v_ref[...].astype(F32))
    mu = jnp.mean(v, axis=-1, keepdims=True)
    var = jnp.mean(jnp.square(v - mu), axis=-1, keepdims=True)
    vn = (((v - mu) * lax.rsqrt(var + EPS)) * lng_ref[...] + lnb_ref[...]).astype(BF16)
    r_i = lax.broadcasted_iota(jnp.int32, (CHUNK, CHUNK), 0)
    c_i = lax.broadcasted_iota(jnp.int32, (CHUNK, CHUNK), 1)
    tri = r_i >= c_i
    for g in range(A_GROUPS):
        wm = jnp.where(tri, w_ref[g], 0.0).astype(BF16)
        cs = slice(g * LANES, (g + 1) * LANES)
        for c in range(rows // CHUNK):
            rs = slice(c * CHUNK, (c + 1) * CHUNK)
            mixed = jnp.dot(wm, vn[rs, cs], preferred_element_type=F32) + bsb_ref[:, cs]
            o_ref[rs, cs] = (u[rs, cs] * mixed).astype(BF16)


def _sgu(proj, ln_g, ln_b, w_s, bsb, *, rows):
    t = proj.shape[0]
    aw = ln_g.shape[1]
    return pl.pallas_call(
        _sgu_kernel,
        out_shape=jax.ShapeDtypeStruct((t, aw), BF16),
        grid=(t // rows,),
        in_specs=[
            pl.BlockSpec((rows, aw), lambda i: (i, OFF_AU // aw)),
            pl.BlockSpec((rows, aw), lambda i: (i, OFF_AV // aw)),
            pl.BlockSpec((1, aw), lambda i: (0, 0)),
            pl.BlockSpec((1, aw), lambda i: (0, 0)),
            pl.BlockSpec((A_GROUPS, CHUNK, CHUNK), lambda i: (0, 0, 0)),
            pl.BlockSpec((CHUNK, aw), lambda i: (0, 0)),
        ],
        out_specs=pl.BlockSpec((rows, aw), lambda i: (i, 0)),
        compiler_params=_cparams(1),
        name="sgu",
    )(proj, proj, ln_g, ln_b, w_s, bsb)


def _t5_bucket_np(dist):
    n = np.maximum(dist, 0)
    nf = np.maximum(n, 1).astype(np.float64)
    large = REL_EXACT + (np.log(nf / REL_EXACT) / math.log(REL_MAX_DIST / REL_EXACT)
                         * (REL_BUCKETS - REL_EXACT)).astype(np.int32)
    large = np.minimum(large, REL_BUCKETS - 1)
    return np.where(n < REL_EXACT, n, large).astype(np.int32)


def _bucket_tiles(bq, bk):
    last_bucket_from = int(np.argmax(_t5_bucket_np(np.arange(4 * REL_MAX_DIST)) == REL_BUCKETS - 1))
    n_tiles = -(-(bk - 1 + last_bucket_from) // bq)
    c = np.arange(bk)[:, None]
    r = np.arange(bq)[None, :]
    tiles = []
    for t in range(n_tiles):
        dist = t * bq + r - c
        tiles.append(np.where(dist >= 0, _t5_bucket_np(dist), REL_BUCKETS))
    assert _t5_bucket_np(np.array([n_tiles * bq - (bk - 1)]))[0] == REL_BUCKETS - 1
    return np.stack(tiles).astype(np.int32)


BIAS_ROWS = 64


def _bias_tiles_kernel(rb_ref, bkt_ref, o_ref, *, present):
    hd = pl.program_id(0)
    n_bias, bk, bq = o_ref.shape
    for t in range(n_bias - 1):
        for rc in range(bk // BIAS_ROWS):
            rs = slice(rc * BIAS_ROWS, (rc + 1) * BIAS_ROWS)
            bt = bkt_ref[t, rs, :]
            tile = jnp.where(bt == REL_BUCKETS, NEG_BIG, 0.0).astype(F32)
            for b in present[t][rc]:
                tile = jnp.where(bt == b, rb_ref[b, hd] * LOG2E, tile)
            o_ref[t, rs, :] = tile
    o_ref[n_bias - 1] = jnp.full((bk, bq), rb_ref[REL_BUCKETS - 1, hd] * LOG2E, F32)


def _bias_tiles(rel_bias, buckets_np):
    n_tiles, bk, bq = buckets_np.shape
    heads = rel_bias.shape[1]
    present = tuple(tuple(tuple(int(b) for b in np.unique(buckets_np[t, rc * BIAS_ROWS:(rc + 1) * BIAS_ROWS])
                                if b < REL_BUCKETS)
                          for rc in range(bk // BIAS_ROWS)) for t in range(n_tiles))
    return pl.pallas_call(
        functools.partial(_bias_tiles_kernel, present=present),
        out_shape=jax.ShapeDtypeStruct((heads, n_tiles + 1, bk, bq), F32),
        grid=(heads,),
        in_specs=[pl.BlockSpec(memory_space=pltpu.SMEM),
                  pl.BlockSpec((n_tiles, bk, bq), lambda h: (0, 0, 0))],
        out_specs=pl.BlockSpec((None, n_tiles + 1, bk, bq), lambda h: (h, 0, 0, 0)),
        compiler_params=_cparams(1),
        name="bias_tiles",
    )(rel_bias, jnp.asarray(buckets_np))


def _attn_kernel(lam_ref, sub_ref, bias_ref, q_ref, k_ref, v_ref, o_ref,
                 vt_ref, qqt_ref, s_ref, p_ref, m_ref, al_ref, acc_ref, *, lam_init):
    bq, hw = q_ref.shape
    bk = s_ref.shape[0]
    seq = k_ref.shape[0]
    n_bias = bias_ref.shape[0]
    qi = pl.program_id(2)

    @pl.when(qi == 0)
    def _():
        for c in range(seq // LANES):
            cs = slice(c * LANES, (c + 1) * LANES)
            vt_ref[0:hw, cs] = v_ref[cs, :].astype(F32).T.astype(BF16)
        pad_row = lax.broadcasted_iota(jnp.int32, (vt_ref.shape[0] - hw, seq), 0)
        vt_ref[hw:, :] = jnp.where(pad_row == 0, 1.0, 0.0).astype(BF16)

    qt = (q_ref[...].astype(F32) * (B_HEAD_DIM ** -0.5 * LOG2E)).T
    dim = lax.broadcasted_iota(jnp.int32, qt.shape, 0)
    qqt_ref[...] = jnp.concatenate([jnp.where(dim < B_HEAD_DIM, qt, 0.0),
                                    jnp.where(dim >= B_HEAD_DIM, qt, 0.0)], axis=1).astype(BF16)
    n_blocks = (qi * bq) // bk + 1

    def scores(j):
        kb = k_ref[pl.ds(pl.multiple_of(j * bk, bk), bk), :]
        s = jnp.dot(kb, qqt_ref[...], preferred_element_type=F32)
        bias = bias_ref[jnp.minimum((qi * bq - j * bk) // bq, n_bias - 1)]
        return s + jnp.concatenate([bias, bias], axis=1)

    def weighted_values(j):
        vt = vt_ref[:, pl.ds(pl.multiple_of(j * bk, bk), bk)]
        acc_ref[...] = al_ref[...] * acc_ref[...] + jnp.dot(vt, p_ref[...], preferred_element_type=F32)

    m_ref[...] = jnp.full(m_ref.shape, NEG_BIG, F32)
    al_ref[...] = jnp.ones(al_ref.shape, F32)
    acc_ref[...] = jnp.zeros(acc_ref.shape, F32)
    p_ref[...] = jnp.zeros(p_ref.shape, BF16)
    s_ref[...] = scores(0)

    def body(j, carry):
        weighted_values(jnp.maximum(j - 1, 0))
        s = s_ref[...]
        m_prev = m_ref[...]
        m_next = jnp.maximum(m_prev, jnp.max(s, axis=0, keepdims=True))
        alpha = jnp.exp2(m_prev - m_next)
        p = jnp.exp2(s - m_next)
        m_ref[...] = m_next
        al_ref[...] = alpha
        p_ref[...] = p.astype(BF16)
        s_ref[...] = scores(jnp.minimum(j + 1, n_blocks - 1))
        return carry

    lax.fori_loop(0, n_blocks, body, 0)
    weighted_values(n_blocks - 1)

    lam_p = lam_ref[...]
    lam = (jnp.exp(jnp.sum(lam_p[0:1] * lam_p[1:2], axis=1, keepdims=True))
           - jnp.exp(jnp.sum(lam_p[2:3] * lam_p[3:4], axis=1, keepdims=True)) + lam_init)
    o = acc_ref[0:hw, :] * (1.0 / acc_ref[hw:hw + 1, :])
    attn = o[:, :bq] - lam * o[:, bq:]
    ms = jnp.mean(attn * attn, axis=0, keepdims=True)
    y = ((attn * lax.rsqrt(ms + EPS)) * sub_ref[...]) * (1.0 - lam_init)
    o_ref[...] = y.T.astype(BF16)


def _attn(proj, bias_tiles, lam_p, subln_t, *, batch, seq, bq, lam_init):
    t = proj.shape[0]
    hw = 2 * B_HEAD_DIM
    heads, n_bias, bk, _ = bias_tiles.shape
    nq = seq // bq
    return pl.pallas_call(
        functools.partial(_attn_kernel, lam_init=lam_init),
        out_shape=jax.ShapeDtypeStruct((t, heads * hw), BF16),
        grid=(batch, heads, nq),
        in_specs=[
            pl.BlockSpec((4, B_HEAD_DIM), lambda b, h, i: (0, 0)),
            pl.BlockSpec((hw, bq), lambda b, h, i: (0, 0)),
            pl.BlockSpec((None, n_bias, bk, bq), lambda b, h, i: (h, 0, 0, 0)),
            pl.BlockSpec((bq, hw), lambda b, h, i: (b * nq + i, OFF_Q // hw + h)),
            pl.BlockSpec((seq, hw), lambda b, h, i: (b, OFF_K // hw + h)),
            pl.BlockSpec((seq, hw), lambda b, h, i: (b, OFF_V // hw + h)),
        ],
        out_specs=pl.BlockSpec((bq, hw), lambda b, h, i: (b * nq + i, h)),
        scratch_shapes=[pltpu.VMEM((hw + BF16_ROWS, seq), BF16),
                        pltpu.VMEM((hw, 2 * bq), BF16),
                        pltpu.VMEM((bk, 2 * bq), F32),
                        pltpu.VMEM((bk, 2 * bq), BF16),
                        pltpu.VMEM((1, 2 * bq), F32), pltpu.VMEM((1, 2 * bq), F32),
                        pltpu.VMEM((hw + BF16_ROWS, 2 * bq), F32)],
        compiler_params=_cparams(3),
        name="diff_attn",
    )(lam_p, subln_t, bias_tiles, proj, proj, proj)


def _split_dot(x, e_bf16, passes):
    out = None
    r = x
    for _ in range(passes):
        hi = r.astype(BF16)
        part = jnp.dot(hi, e_bf16, preferred_element_type=F32)
        out = part if out is None else out + part
        r = r - hi.astype(F32)
    return out


def _mamba_kernel(xbc_ref, z_ref, dt_ref, cw_ref, cb_ref, dtb_ref, alog_ref, dsk_ref, ng_ref, e_ref,
                  o_ref, tail_ref, st_ref):
    rows, c_inner = z_ref.shape
    gw = c_inner // C_GROUPS
    heads_per_group = gw // C_HEAD_DIM
    c = pl.program_id(1)

    @pl.when(c == 0)
    def _():
        tail_ref[...] = jnp.zeros_like(tail_ref)
        st_ref[...] = jnp.zeros_like(st_ref)

    r_i = lax.broadcasted_iota(jnp.int32, (CHUNK, CHUNK), 0)
    c_i = lax.broadcasted_iota(jnp.int32, (CHUNK, CHUNK), 1)
    tri = r_i >= c_i
    lower = tri.astype(F32)
    upper = (r_i <= c_i).astype(F32)
    shift = jnp.concatenate([(r_i - c_i == d).astype(BF16) for d in range(C_CONV - 1, 0, -1)], axis=0)
    row8 = lax.broadcasted_iota(jnp.int32, (8, xbc_ref.shape[1]), 0)
    lane_lo = lax.broadcasted_iota(jnp.int32, (CHUNK, LANES), 1) < C_HEAD_DIM
    neg_a = -jnp.exp(alog_ref[...]) * LOG2E
    e = e_ref[...]
    hp = lax.Precision.HIGHEST
    prev = st_ref[...]

    for u in range(rows // CHUNK):
        rs = slice(u * CHUNK, (u + 1) * CHUNK)
        x_cur = xbc_ref[rs, :]
        delayed = jnp.dot(shift, x_cur, preferred_element_type=F32)
        if u == 0:
            tail = tail_ref[8:16, :]
        else:
            tail = xbc_ref[u * CHUNK - 16:u * CHUNK, :].astype(F32)[8:16, :]
        conv = cb_ref[...] + cw_ref[C_CONV - 1:C_CONV, :] * x_cur.astype(F32)
        conv_top = jnp.zeros_like(tail)
        for k in range(C_CONV - 1):
            d = C_CONV - 1 - k
            conv = conv + cw_ref[k:k + 1, :] * delayed[k * CHUNK:(k + 1) * CHUNK]
            conv_top = conv_top + cw_ref[k:k + 1, :] * jnp.where(row8 < d, pltpu.roll(tail, d, 0), 0.0)
        conv = jnp.concatenate([conv[0:8] + conv_top, conv[8:]], axis=0)
        xbc = conv * jax.nn.sigmoid(conv)
        xs = xbc[:, :c_inner]
        bm = xbc[:, c_inner:c_inner + C_GROUPS * C_STATE]
        cm = xbc[:, c_inner + C_GROUPS * C_STATE:]

        dt = jax.nn.softplus(dt_ref[rs, :] + dtb_ref[...])
        a = dt * neg_a
        a_cs = jnp.dot(lower, a, precision=hp, preferred_element_type=F32)
        a_cs_t = jnp.dot(a.T, upper, precision=hp, preferred_element_type=F32)

        dt_x = _split_dot(dt, e, 2)
        acs_x = _split_dot(a_cs, e, 3)
        last_x = acs_x[CHUNK - 1:CHUNK, :]
        xd = xs * dt_x
        xd_b = xd.astype(BF16)
        xdd_b = (xd * jnp.exp2(last_x - acs_x)).astype(BF16)
        prev_b = prev.astype(BF16)
        eacs = jnp.exp2(acs_x)

        y_parts = []
        st_parts = []
        for g in range(C_GROUPS):
            gs = slice(g * gw, (g + 1) * gw)
            b_g = bm[:, g * C_STATE:(g + 1) * C_STATE]
            c_g = cm[:, g * C_STATE:(g + 1) * C_STATE].astype(BF16)
            cb = lax.dot_general(c_g, b_g.astype(BF16), (((1,), (1,)), ((), ())), preferred_element_type=F32)
            st_parts.append(jnp.dot(b_g.T.astype(BF16), xdd_b[:, gs], preferred_element_type=F32))
            y_off = jnp.dot(c_g, prev_b[:, gs], preferred_element_type=F32) * eacs[:, gs]
            tiles = []
            for pr in range(heads_per_group // 2):
                t_idx = g * (heads_per_group // 2) + pr
                xt = xd_b[:, t_idx * LANES:(t_idx + 1) * LANES]
                res = []
                for half in range(2):
                    hd = 2 * t_idx + half
                    seg = a_cs[:, hd:hd + 1] - a_cs_t[hd:hd + 1, :]
                    decay = jnp.where(tri, jnp.exp2(jnp.minimum(seg, 0.0)), 0.0)
                    res.append(jnp.dot((cb * decay).astype(BF16), xt, preferred_element_type=F32))
                tiles.append(jnp.where(lane_lo, res[0], res[1]))
            y_parts.append(jnp.concatenate(tiles, axis=1) + y_off)

        prev = prev * jnp.exp2(last_x) + jnp.concatenate(st_parts, axis=1)
        y = jnp.concatenate(y_parts, axis=1) + dsk_ref[...] * xs
        zf = z_ref[rs, :].astype(F32)
        gated = y * (zf * jax.nn.sigmoid(zf))
        o_ref[rs, :] = _rmsnorm_f32(gated, ng_ref[...]).astype(BF16)

    st_ref[...] = prev
    tail_ref[...] = xbc_ref[rows - 16:rows, :].astype(F32)


def _mamba(proj, dt_raw, conv_w, conv_b, dt_bias, a_log, d_skip_x, norm_g, expand, *, batch, seq, rows):
    t = proj.shape[0]
    c_inner = norm_g.shape[1]
    conv_ch = conv_w.shape[1]
    nc = seq // rows
    row = lambda b, c: (b * nc + c, 0)
    const = lambda b, c: (0, 0)
    return pl.pallas_call(
        _mamba_kernel,
        out_shape=jax.ShapeDtypeStruct((t, c_inner), BF16),
        grid=(batch, nc),
        in_specs=[
            pl.BlockSpec((rows, conv_ch), lambda b, c: (b * nc + c, OFF_XBC // conv_ch)),
            pl.BlockSpec((rows, c_inner), lambda b, c: (b * nc + c, OFF_Z // c_inner)),
            pl.BlockSpec((rows, LANES), row),
            pl.BlockSpec((C_CONV, conv_ch), const),
            pl.BlockSpec((1, conv_ch), const),
            pl.BlockSpec((1, LANES), const),
            pl.BlockSpec((1, LANES), const),
            pl.BlockSpec((1, c_inner), const),
            pl.BlockSpec((1, c_inner), const),
            pl.BlockSpec((LANES, c_inner), const),
        ],
        out_specs=pl.BlockSpec((rows, c_inner), row),
        scratch_shapes=[pltpu.VMEM((16, conv_ch), F32), pltpu.VMEM((C_STATE, c_inner), F32)],
        compiler_params=_cparams(2),
        name="mamba_ssd",
    )(proj, proj, dt_raw, conv_w, conv_b, dt_bias, a_log, d_skip_x, norm_g, expand)


def _merge_kernel(ya_ref, yb_ref, yc_ref, g0_ref, g1_ref, g2_ref, wa_ref, wb_ref, wc_ref, o_ref):
    pa = jnp.dot(ya_ref[...], wa_ref[...], preferred_element_type=F32)
    pb = jnp.dot(yb_ref[...], wb_ref[...], preferred_element_type=F32)
    pc = jnp.dot(yc_ref[...], wc_ref[...], preferred_element_type=F32)
    o_ref[...] = (g0_ref[...].astype(F32) * pa + g1_ref[...].astype(F32) * pb
                  + g2_ref[...].astype(F32) * pc).astype(BF16)


def _merge(ya, yb, yc, proj, w_pa, w_pb, w_pc, layer, *, bm, bn):
    t = ya.shape[0]
    d = w_pa.shape[2]
    gate_spec = lambda k: pl.BlockSpec((bm, bn), lambda i, j: (i, (OFF_GATE + k * d) // bn + j))
    return pl.pallas_call(
        _merge_kernel,
        out_shape=jax.ShapeDtypeStruct((t, d), BF16),
        grid=(t // bm, d // bn),
        in_specs=[
            pl.BlockSpec((bm, ya.shape[1]), lambda i, j: (i, 0)),
            pl.BlockSpec((bm, yb.shape[1]), lambda i, j: (i, 0)),
            pl.BlockSpec((bm, yc.shape[1]), lambda i, j: (i, 0)),
            gate_spec(0), gate_spec(1), gate_spec(2),
            pl.BlockSpec((None, w_pa.shape[1], bn), lambda i, j: (layer, 0, j)),
            pl.BlockSpec((None, w_pb.shape[1], bn), lambda i, j: (layer, 0, j)),
            pl.BlockSpec((None, w_pc.shape[1], bn), lambda i, j: (layer, 0, j)),
        ],
        out_specs=pl.BlockSpec((bm, bn), lambda i, j: (i, j)),
        compiler_params=_cparams(2),
        name="merge",
    )(ya, yb, yc, proj, proj, proj, w_pa, w_pb, w_pc)


def _outproj_kernel(m_ref, w_ref, h_ref, o_ref):
    o_ref[...] = h_ref[...] + jnp.dot(m_ref[...], w_ref[...], preferred_element_type=F32)


def _outproj(merged, w_out, h, layer, *, bm, bn):
    t, d = h.shape
    return pl.pallas_call(
        _outproj_kernel,
        out_shape=jax.ShapeDtypeStruct((t, d), F32),
        grid=(t // bm, d // bn),
        in_specs=[
            pl.BlockSpec((bm, merged.shape[1]), lambda i, j: (i, 0)),
            pl.BlockSpec((None, merged.shape[1], bn), lambda i, j: (layer, 0, j)),
            pl.BlockSpec((bm, bn), lambda i, j: (i, j)),
        ],
        out_specs=pl.BlockSpec((bm, bn), lambda i, j: (i, j)),
        compiler_params=_cparams(2),
        name="outproj",
    )(merged, w_out, h)


def _final_norm_kernel(x_ref, g_ref, o_ref):
    o_ref[...] = _rmsnorm_f32(x_ref[...], g_ref[...])


def _final_norm(h, g, *, bm):
    t, d = h.shape
    return pl.pallas_call(
        _final_norm_kernel,
        out_shape=jax.ShapeDtypeStruct((t, d), F32),
        grid=(t // bm,),
        in_specs=[pl.BlockSpec((bm, d), lambda i: (i, 0)), pl.BlockSpec((1, d), lambda i: (0, 0))],
        out_specs=pl.BlockSpec((bm, d), lambda i: (i, 0)),
        compiler_params=_cparams(1),
        name="final_norm",
    )(h, g)


def kernel(x, rel_bias, final_norm, ffn1_norm, ffn1_wi, ffn1_wo, mix_norm, w_in, sgu_ln_g, sgu_ln_b, sgu_w, sgu_b, diff_lambda, diff_subln, conv_w, conv_b, dt_bias, a_log, d_skip, ssm_norm, w_pa, w_pb, w_pc, w_out, ffn2_norm, ffn2_wi, ffn2_wo):
    batch, seq, d = x.shape
    depth = ffn1_wi.shape[0]
    t = batch * seq
    heads_c = dt_bias.shape[1]
    a_width = sgu_ln_g.shape[1]
    c_inner = ssm_norm.shape[1]
    conv_ch = conv_w.shape[2]
    d_ff = ffn1_wo.shape[1]

    bm = min(1024, t)
    bf = 512
    bq = 512
    bk = 512
    assert t % bm == 0 and d_ff % bf == 0 and seq % bk == 0 and bk % bq == 0 and bq % LANES == 0
    assert w_in.shape[2] == OFF_GATE + heads_c + 3 * d and conv_ch == OFF_AU

    o_au, o_av, o_q, o_z, o_xbc = 0, a_width, 2 * a_width, 5 * a_width, 5 * a_width + c_inner
    o_dt = o_xbc + conv_ch
    o_g = o_dt + heads_c
    w_main = jnp.concatenate(
        [w_in[:, :, o_xbc:o_dt], w_in[:, :, o_au:o_av], w_in[:, :, o_z:o_xbc], w_in[:, :, o_av:o_q],
         w_in[:, :, o_q:o_z], w_in[:, :, o_g:]], axis=2).astype(BF16)
    w_dt = jnp.pad(w_in[:, :, o_dt:o_g], ((0, 0), (0, 0), (0, LANES - heads_c))).astype(BF16)
    wi1, wo1 = ffn1_wi.astype(BF16), ffn1_wo.astype(BF16)
    wi2, wo2 = ffn2_wi.astype(BF16), ffn2_wo.astype(BF16)
    wpa, wpb, wpc, wout = w_pa.astype(BF16), w_pb.astype(BF16), w_pc.astype(BF16), w_out.astype(BF16)

    bsb = jnp.repeat(jnp.swapaxes(sgu_b, 1, 2), a_width // A_GROUPS, axis=2)
    pad_h = ((0, 0), (0, LANES - heads_c))
    dtb = jnp.pad(dt_bias, pad_h)
    alog = jnp.pad(a_log, pad_h)
    dsk_x = jnp.repeat(d_skip, C_HEAD_DIM, axis=1)
    expand = np.zeros((LANES, c_inner), np.float32)
    expand[np.arange(c_inner) // C_HEAD_DIM, np.arange(c_inner)] = 1.0
    expand = jnp.asarray(expand, BF16)
    bias_tiles = _bias_tiles(rel_bias, _bucket_tiles(bq, bk))
    subln_t = jnp.broadcast_to(diff_subln[:, :, None], diff_subln.shape + (bq,))

    h = x.reshape(t, d)
    for l in range(depth):
        lam_init = 0.8 - 0.6 * math.exp(-0.3 * l)
        h = _ffn(h, ffn1_norm[l][None], wi1, wo1, l, bm=bm, bf=bf)
        proj, dt_raw = _inproj(h, mix_norm[l][None], w_main, w_dt, l, bm=bm, bn=1024)
        ya = _sgu(proj, sgu_ln_g[l][None], sgu_ln_b[l][None], sgu_w[l], bsb[l], rows=2 * CHUNK)
        yb = _attn(proj, bias_tiles, diff_lambda[l], subln_t[l],
                   batch=batch, seq=seq, bq=bq, lam_init=lam_init)
        yc = _mamba(proj, dt_raw, conv_w[l], conv_b[l][None], dtb[l][None], alog[l][None],
                    dsk_x[l][None], ssm_norm[l][None], expand, batch=batch, seq=seq, rows=2 * CHUNK)
        merged = _merge(ya, yb, yc, proj, wpa, wpb, wpc, l, bm=bm, bn=512)
        h = _outproj(merged, wout, h, l, bm=bm, bn=1024)
        h = _ffn(h, ffn2_norm[l][None], wi2, wo2, l, bm=bm, bf=bf)
    return _final_norm(h, final_norm[None], bm=bm).reshape(batch, seq, d)
```

```python
import functools
import math

import numpy as np
import jax
import jax.numpy as jnp
from jax import lax
from jax.experimental import pallas as pl
from jax.experimental.pallas import tpu as pltpu

F32 = jnp.float32
BF16 = jnp.bfloat16

EPS = 1e-6
CHUNK = 128
LANES = 128
BF16_ROWS = 16
A_GROUPS = 8
B_HEAD_DIM = 64
C_HEAD_DIM = 64
C_GROUPS = 4
C_STATE = 128
C_CONV = 4
REL_BUCKETS = 32
REL_MAX_DIST = 128
REL_EXACT = REL_BUCKETS // 2
NEG_BIG = -1e30
LOG2E = math.log2(math.e)
VMEM_LIMIT_V7X = 56 * 1024 * 1024

OFF_XBC, OFF_AU, OFF_Z, OFF_AV, OFF_Q, OFF_K, OFF_V, OFF_GATE = 0, 3072, 4096, 6144, 7168, 8192, 9216, 10240


def _cparams(n_axes):
    return pltpu.CompilerParams(dimension_semantics=("arbitrary",) * n_axes,
                                vmem_limit_bytes=VMEM_LIMIT_V7X)


def _rmsnorm_f32(x, g):
    ms = jnp.mean(x * x, axis=-1, keepdims=True)
    return (x * lax.rsqrt(ms + EPS)) * g


def _ffn_kernel(x_ref, g_ref, wg_ref, wu_ref, wo_ref, o_ref, xn_ref):
    j = pl.program_id(1)

    @pl.when(j == 0)
    def _():
        x = x_ref[...]
        xn_ref[...] = _rmsnorm_f32(x, g_ref[...]).astype(BF16)
        o_ref[...] = x

    xn = xn_ref[...]
    gate = jnp.dot(xn, wg_ref[...], preferred_element_type=F32)
    up = jnp.dot(xn, wu_ref[...], preferred_element_type=F32)
    act = ((0.5 * gate) * jax.nn.sigmoid(gate) * up).astype(BF16)
    o_ref[...] += jnp.dot(act, wo_ref[...], preferred_element_type=F32)


def _ffn(h, g, wi, wo, layer, *, bm, bf):
    t, d = h.shape
    d_ff = wo.shape[1]
    nj = d_ff // bf
    return pl.pallas_call(
        _ffn_kernel,
        out_shape=jax.ShapeDtypeStruct((t, d), F32),
        grid=(t // bm, nj),
        in_specs=[
            pl.BlockSpec((bm, d), lambda i, j: (i, 0)),
            pl.BlockSpec((1, d), lambda i, j: (0, 0)),
            pl.BlockSpec((None, d, bf), lambda i, j: (layer, 0, j)),
            pl.BlockSpec((None, d, bf), lambda i, j: (layer, 0, j + nj)),
            pl.BlockSpec((None, bf, d), lambda i, j: (layer, j, 0)),
        ],
        out_specs=pl.BlockSpec((bm, d), lambda i, j: (i, 0)),
        scratch_shapes=[pltpu.VMEM((bm, d), BF16)],
        compiler_params=_cparams(2),
        name="ffn",
    )(h, g, wi, wi, wo)


def _inproj_kernel(x_ref, g_ref, w_ref, wdt_ref, o_ref, dt_ref, xn_ref, *, nj_plain):
    j = pl.program_id(1)

    @pl.when(j == 0)
    def _():
        xn = _rmsnorm_f32(x_ref[...], g_ref[...]).astype(BF16)
        xn_ref[...] = xn
        dt_ref[...] = jnp.dot(xn, wdt_ref[...], preferred_element_type=F32)

    acc = jnp.dot(xn_ref[...], w_ref[...], preferred_element_type=F32)
    o_ref[...] = jnp.where(j >= nj_plain, jax.nn.sigmoid(acc), acc).astype(BF16)


def _inproj(h, g, w_main, w_dt, layer, *, bm, bn):
    t, d = h.shape
    n = w_main.shape[2]
    return pl.pallas_call(
        functools.partial(_inproj_kernel, nj_plain=OFF_GATE // bn),
        out_shape=(jax.ShapeDtypeStruct((t, n), BF16), jax.ShapeDtypeStruct((t, LANES), F32)),
        grid=(t // bm, n // bn),
        in_specs=[
            pl.BlockSpec((bm, d), lambda i, j: (i, 0)),
            pl.BlockSpec((1, d), lambda i, j: (0, 0)),
            pl.BlockSpec((None, d, bn), lambda i, j: (layer, 0, j)),
            pl.BlockSpec((None, d, LANES), lambda i, j: (layer, 0, 0)),
        ],
        out_specs=(pl.BlockSpec((bm, bn), lambda i, j: (i, j)),
                   pl.BlockSpec((bm, LANES), lambda i, j: (i, 0))),
        scratch_shapes=[pltpu.VMEM((bm, d), BF16)],
        compiler_params=_cparams(2),
        name="inproj",
    )(h, g, w_main, w_dt)


def _sgu_kernel(u_ref, v_ref, lng_ref, lnb_ref, w_ref, bsb_ref, o_ref):
    rows = u_ref.shape[0]
    u = jax.nn.gelu(u_ref[...].astype(F32))
    v = jax.nn.gelu(v_ref[...].astype(F32))
    mu = jnp.mean(v, axis=-1, keepdims=True)
    var = jnp.mean(jnp.square(v - mu), axis=-1, keepdims=True)
    vn = (((v - mu) * lax.rsqrt(var + EPS)) * lng_ref[...] + lnb_ref[...]).astype(BF16)
    r_i = lax.broadcasted_iota(jnp.int32, (CHUNK, CHUNK), 0)
    c_i = lax.broadcasted_iota(jnp.int32, (CHUNK, CHUNK), 1)
    tri = r_i >= c_i
    for g in range(A_GROUPS):
        wm = jnp.where(tri, w_ref[g], 0.0).astype(BF16)
        cs = slice(g * LANES, (g + 1) * LANES)
        for c in range(rows // CHUNK):
            rs = slice(c * CHUNK, (c + 1) * CHUNK)
            mixed = jnp.dot(wm, vn[rs, cs], preferred_element_type=F32) + bsb_ref[:, cs]
            o_ref[rs, cs] = (u[rs, cs] * mixed).astype(BF16)


def _sgu(proj, ln_g, ln_b, w_s, bsb, *, rows):
    t = proj.shape[0]
    aw = ln_g.shape[1]
    return pl.pallas_call(
        _sgu_kernel,
        out_shape=jax.ShapeDtypeStruct((t, aw), BF16),
        grid=(t // rows,),
        in_specs=[
            pl.BlockSpec((rows, aw), lambda i: (i, OFF_AU // aw)),
            pl.BlockSpec((rows, aw), lambda i: (i, OFF_AV // aw)),
            pl.BlockSpec((1, aw), lambda i: (0, 0)),
            pl.BlockSpec((1, aw), lambda i: (0, 0)),
            pl.BlockSpec((A_GROUPS, CHUNK, CHUNK), lambda i: (0, 0, 0)),
            pl.BlockSpec((CHUNK, aw), lambda i: (0, 0)),
        ],
        out_specs=pl.BlockSpec((rows, aw), lambda i: (i, 0)),
        compiler_params=_cparams(1),
        name="sgu",
    )(proj, proj, ln_g, ln_b, w_s, bsb)


def _t5_bucket_np(dist):
    n = np.maximum(dist, 0)
    nf = np.maximum(n, 1).astype(np.float64)
    large = REL_EXACT + (np.log(nf / REL_EXACT) / math.log(REL_MAX_DIST / REL_EXACT)
                         * (REL_BUCKETS - REL_EXACT)).astype(np.int32)
    large = np.minimum(large, REL_BUCKETS - 1)
    return np.where(n < REL_EXACT, n, large).astype(np.int32)


def _bucket_tiles(bq, bk):
    last_bucket_from = int(np.argmax(_t5_bucket_np(np.arange(4 * REL_MAX_DIST)) == REL_BUCKETS - 1))
    n_tiles = -(-(bk - 1 + last_bucket_from) // bq)
    c = np.arange(bk)[:, None]
    r = np.arange(bq)[None, :]
    tiles = []
    for t in range(n_tiles):
        dist = t * bq + r - c
        tiles.append(np.where(dist >= 0, _t5_bucket_np(dist), REL_BUCKETS))
    assert _t5_bucket_np(np.array([n_tiles * bq - (bk - 1)]))[0] == REL_BUCKETS - 1
    return np.stack(tiles).astype(np.int32)


BIAS_ROWS = 64


def _bias_tiles_kernel(rb_ref, bkt_ref, o_ref, *, present):
    hd = pl.program_id(0)
    n_bias, bk, bq = o_ref.shape
    for t in range(n_bias - 1):
        for rc in range(bk // BIAS_ROWS):
            rs = slice(rc * BIAS_ROWS, (rc + 1) * BIAS_ROWS)
            bt = bkt_ref[t, rs, :]
            tile = jnp.where(bt == REL_BUCKETS, NEG_BIG, 0.0).astype(F32)
            for b in present[t][rc]:
                tile = jnp.where(bt == b, rb_ref[b, hd] * LOG2E, tile)
            o_ref[t, rs, :] = tile
    o_ref[n_bias - 1] = jnp.full((bk, bq), rb_ref[REL_BUCKETS - 1, hd] * LOG2E, F32)


def _bias_tiles(rel_bias, buckets_np):
    n_tiles, bk, bq = buckets_np.shape
    heads = rel_bias.shape[1]
    present = tuple(tuple(tuple(int(b) for b in np.unique(buckets_np[t, rc * BIAS_ROWS:(rc + 1) * BIAS_ROWS])
                                if b < REL_BUCKETS)
                          for rc in range(bk // BIAS_ROWS)) for t in range(n_tiles))
    return pl.pallas_call(
        functools.partial(_bias_tiles_kernel, present=present),
        out_shape=jax.ShapeDtypeStruct((heads, n_tiles + 1, bk, bq), F32),
        grid=(heads,),
        in_specs=[pl.BlockSpec(memory_space=pltpu.SMEM),
                  pl.BlockSpec((n_tiles, bk, bq), lambda h: (0, 0, 0))],
        out_specs=pl.BlockSpec((None, n_tiles + 1, bk, bq), lambda h: (h, 0, 0, 0)),
        compiler_params=_cparams(1),
        name="bias_tiles",
    )(rel_bias, jnp.asarray(buckets_np))


def _attn_kernel(lam_ref, sub_ref, bias_ref, q_ref, k_ref, v_ref, o_ref,
                 vt_ref, qqt_ref, s_ref, p_ref, cm_ref, m_ref, al_ref, acc_ref, *, lam_init):
    hw = 2 * B_HEAD_DIM
    bq = q_ref.shape[0]
    hpb, bk = s_ref.shape[0], s_ref.shape[1]
    seq = k_ref.shape[0]
    n_bias = bias_ref.shape[1]
    qi = pl.program_id(2)
    cols = [slice(hh * hw, (hh + 1) * hw) for hh in range(hpb)]

    @pl.when(qi == 0)
    def _():
        pad_row = lax.broadcasted_iota(jnp.int32, (vt_ref.shape[2] - hw, bk), 0)
        ones_row = jnp.where(pad_row == 0, 1.0, 0.0).astype(BF16)
        for hh in range(hpb):
            for c in range(seq // LANES):
                blk, off = divmod(c * LANES, bk)
                vt_ref[hh, blk, 0:hw, off:off + LANES] = (
                    v_ref[c * LANES:(c + 1) * LANES, cols[hh]].astype(F32).T.astype(BF16))
            for blk in range(seq // bk):
                vt_ref[hh, blk, hw:, :] = ones_row

    dim = lax.broadcasted_iota(jnp.int32, (hw, bq), 0)
    for hh in range(hpb):
        qt = (q_ref[:, cols[hh]].astype(F32) * (B_HEAD_DIM ** -0.5 * LOG2E)).T
        qqt_ref[hh] = jnp.concatenate([jnp.where(dim < B_HEAD_DIM, qt, 0.0),
                                       jnp.where(dim >= B_HEAD_DIM, qt, 0.0)], axis=1).astype(BF16)
    n_blocks = (qi * bq) // bk + 1

    def scores(hh, j):
        kb = k_ref[pl.ds(pl.multiple_of(j * bk, bk), bk), cols[hh]]
        s = jnp.dot(kb, qqt_ref[hh], preferred_element_type=F32)
        bias = bias_ref[hh, jnp.minimum((qi * bq - j * bk) // bq, n_bias - 1)]
        s = s + jnp.concatenate([bias, bias], axis=1)
        s_ref[hh] = s
        cm_ref[hh] = jnp.max(s, axis=0, keepdims=True)

    def probabilities(hh):
        m_prev = m_ref[hh]
        m_next = jnp.maximum(m_prev, cm_ref[hh])
        m_ref[hh] = m_next
        al_ref[hh] = jnp.exp2(m_prev - m_next)
        p_ref[hh] = jnp.exp2(s_ref[hh] - m_next).astype(BF16)

    def weighted_values(hh, j):
        vt = vt_ref[hh, j]
        acc_ref[hh] = al_ref[hh] * acc_ref[hh] + jnp.dot(vt, p_ref[hh], preferred_element_type=F32)

    m_ref[...] = jnp.full(m_ref.shape, NEG_BIG, F32)
    al_ref[...] = jnp.ones(al_ref.shape, F32)
    acc_ref[...] = jnp.zeros(acc_ref.shape, F32)
    p_ref[...] = jnp.zeros(p_ref.shape, BF16)
    for hh in range(hpb):
        scores(hh, 0)

    def body(j, carry):
        for hh in range(hpb):
            weighted_values(hh, jnp.maximum(j - 1, 0))
        for hh in range(hpb):
            probabilities(hh)
        for hh in range(hpb):
            scores(hh, j + 1)
        return carry

    lax.fori_loop(0, n_blocks - 1, body, 0)
    for hh in range(hpb):
        weighted_values(hh, jnp.maximum(n_blocks - 2, 0))
        probabilities(hh)

    lam_p = lam_ref[...]
    lam = (jnp.exp(jnp.sum(lam_p[0:1] * lam_p[1:2], axis=1, keepdims=True))
           - jnp.exp(jnp.sum(lam_p[2:3] * lam_p[3:4], axis=1, keepdims=True)) + lam_init)
    for hh in range(hpb):
        weighted_values(hh, n_blocks - 1)
        o = acc_ref[hh, 0:hw, :] * (1.0 / acc_ref[hh, hw:hw + 1, :])
        attn = o[:, :bq] - lam * o[:, bq:]
        ms = jnp.mean(attn * attn, axis=0, keepdims=True)
        y = ((attn * lax.rsqrt(ms + EPS)) * sub_ref[...]) * (1.0 - lam_init)
        o_ref[:, cols[hh]] = y.T.astype(BF16)


def _attn(proj, bias_tiles, lam_p, subln_t, *, batch, seq, bq, hpb, lam_init):
    t = proj.shape[0]
    hw = 2 * B_HEAD_DIM
    heads, n_bias, bk, _ = bias_tiles.shape
    nq = seq // bq
    bw = hpb * hw
    return pl.pallas_call(
        functools.partial(_attn_kernel, lam_init=lam_init),
        out_shape=jax.ShapeDtypeStruct((t, heads * hw), BF16),
        grid=(batch, heads // hpb, nq),
        in_specs=[
            pl.BlockSpec((4, B_HEAD_DIM), lambda b, h, i: (0, 0)),
            pl.BlockSpec((hw, bq), lambda b, h, i: (0, 0)),
            pl.BlockSpec((hpb, n_bias, bk, bq), lambda b, h, i: (h, 0, 0, 0)),
            pl.BlockSpec((bq, bw), lambda b, h, i: (b * nq + i, OFF_Q // bw + h)),
            pl.BlockSpec((seq, bw), lambda b, h, i: (b, OFF_K // bw + h)),
            pl.BlockSpec((seq, bw), lambda b, h, i: (b, OFF_V // bw + h)),
        ],
        out_specs=pl.BlockSpec((bq, bw), lambda b, h, i: (b * nq + i, h)),
        scratch_shapes=[pltpu.VMEM((hpb, seq // bk, hw + BF16_ROWS, bk), BF16),
                        pltpu.VMEM((hpb, hw, 2 * bq), BF16),
                        pltpu.VMEM((hpb, bk, 2 * bq), F32),
                        pltpu.VMEM((hpb, bk, 2 * bq), BF16),
                        pltpu.VMEM((hpb, 1, 2 * bq), F32), pltpu.VMEM((hpb, 1, 2 * bq), F32),
                        pltpu.VMEM((hpb, 1, 2 * bq), F32),
                        pltpu.VMEM((hpb, hw + BF16_ROWS, 2 * bq), F32)],
        compiler_params=_cparams(3),
        name="diff_attn",
    )(lam_p, subln_t, bias_tiles, proj, proj, proj)


def _split_dot(x, e_bf16, passes):
    out = None
    r = x
    for _ in range(passes):
        hi = r.astype(BF16)
        part = jnp.dot(hi, e_bf16, preferred_element_type=F32)
        out = part if out is None else out + part
        r = r - hi.astype(F32)
    return out


def _mamba_kernel(xbc_ref, z_ref, dt_ref, cw_ref, cb_ref, dtb_ref, alog_ref, dsk_ref, ng_ref, e_ref,
                  o_ref, tail_ref, st_ref):
    rows, c_inner = z_ref.shape
    gw = c_inner // C_GROUPS
    heads_per_group = gw // C_HEAD_DIM
    c = pl.program_id(1)

    @pl.when(c == 0)
    def _():
        tail_ref[...] = jnp.zeros_like(tail_ref)
        st_ref[...] = jnp.zeros_like(st_ref)

    r_i = lax.broadcasted_iota(jnp.int32, (CHUNK, CHUNK), 0)
    c_i = lax.broadcasted_iota(jnp.int32, (CHUNK, CHUNK), 1)
    tri = r_i >= c_i
    lower = tri.astype(F32)
    upper = (r_i <= c_i).astype(F32)
    shift = jnp.concatenate([(r_i - c_i == d).astype(BF16) for d in range(C_CONV - 1, 0, -1)], axis=0)
    row8 = lax.broadcasted_iota(jnp.int32, (8, xbc_ref.shape[1]), 0)
    lane_lo = lax.broadcasted_iota(jnp.int32, (CHUNK, LANES), 1) < C_HEAD_DIM
    neg_a = -jnp.exp(alog_ref[...]) * LOG2E
    e = e_ref[...]
    hp = lax.Precision.HIGHEST
    prev = st_ref[...]

    for u in range(rows // CHUNK):
        rs = slice(u * CHUNK, (u + 1) * CHUNK)
        x_cur = xbc_ref[rs, :]
        delayed = jnp.dot(shift, x_cur, preferred_element_type=F32)
        if u == 0:
            tail = tail_ref[8:16, :]
        else:
            tail = xbc_ref[u * CHUNK - 16:u * CHUNK, :].astype(F32)[8:16, :]
        conv = cb_ref[...] + cw_ref[C_CONV - 1:C_CONV, :] * x_cur.astype(F32)
        conv_top = jnp.zeros_like(tail)
        for k in range(C_CONV - 1):
            d = C_CONV - 1 - k
            conv = conv + cw_ref[k:k + 1, :] * delayed[k * CHUNK:(k + 1) * CHUNK]
            conv_top = conv_top + cw_ref[k:k + 1, :] * jnp.where(row8 < d, pltpu.roll(tail, d, 0), 0.0)
        conv = jnp.concatenate([conv[0:8] + conv_top, conv[8:]], axis=0)
        xbc = conv * jax.nn.sigmoid(conv)
        xs = xbc[:, :c_inner]
        bm = xbc[:, c_inner:c_inner + C_GROUPS * C_STATE]
        cm = xbc[:, c_inner + C_GROUPS * C_STATE:]

        dt = jax.nn.softplus(dt_ref[rs, :] + dtb_ref[...])
        a = dt * neg_a
        a_cs = jnp.dot(lower, a, precision=hp, preferred_element_type=F32)
        a_cs_t = jnp.dot(a.T, upper, precision=hp, preferred_element_type=F32)

        dt_x = _split_dot(dt, e, 2)
        acs_x = _split_dot(a_cs, e, 3)
        last_x = acs_x[CHUNK - 1:CHUNK, :]
        xd = xs * dt_x
        xd_b = xd.astype(BF16)
        xdd_b = (xd * jnp.exp2(last_x - acs_x)).astype(BF16)
        prev_b = prev.astype(BF16)
        eacs = jnp.exp2(acs_x)

        y_parts = []
        st_parts = []
        for g in range(C_GROUPS):
            gs = slice(g * gw, (g + 1) * gw)
            b_g = bm[:, g * C_STATE:(g + 1) * C_STATE]
            c_g = cm[:, g * C_STATE:(g + 1) * C_STATE].astype(BF16)
            cb = lax.dot_general(c_g, b_g.astype(BF16), (((1,), (1,)), ((), ())), preferred_element_type=F32)
            st_parts.append(jnp.dot(b_g.T.astype(BF16), xdd_b[:, gs], preferred_element_type=F32))
            y_off = jnp.dot(c_g, prev_b[:, gs], preferred_element_type=F32) * eacs[:, gs]
            tiles = []
            for pr in range(heads_per_group // 2):
                t_idx = g * (heads_per_group // 2) + pr
                xt = xd_b[:, t_idx * LANES:(t_idx + 1) * LANES]
                res = []
                for half in range(2):
                    hd = 2 * t_idx + half
                    seg = a_cs[:, hd:hd + 1] - a_cs_t[hd:hd + 1, :]
                    decay = jnp.where(tri, jnp.exp2(jnp.minimum(seg, 0.0)), 0.0)
                    res.append(jnp.dot((cb * decay).astype(BF16), xt, preferred_element_type=F32))
                tiles.append(jnp.where(lane_lo, res[0], res[1]))
            y_parts.append(jnp.concatenate(tiles, axis=1) + y_off)

        prev = prev * jnp.exp2(last_x) + jnp.concatenate(st_parts, axis=1)
        y = jnp.concatenate(y_parts, axis=1) + dsk_ref[...] * xs
        zf = z_ref[rs, :].astype(F32)
        gated = y * (zf * jax.nn.sigmoid(zf))
        o_ref[rs, :] = _rmsnorm_f32(gated, ng_ref[...]).astype(BF16)

    st_ref[...] = prev
    tail_ref[...] = xbc_ref[rows - 16:rows, :].astype(F32)


def _mamba(proj, dt_raw, conv_w, conv_b, dt_bias, a_log, d_skip_x, norm_g, expand, *, batch, seq, rows):
    t = proj.shape[0]
    c_inner = norm_g.shape[1]
    conv_ch = conv_w.shape[1]
    nc = seq // rows
    row = lambda b, c: (b * nc + c, 0)
    const = lambda b, c: (0, 0)
    return pl.pallas_call(
        _mamba_kernel,
        out_shape=jax.ShapeDtypeStruct((t, c_inner), BF16),
        grid=(batch, nc),
        in_specs=[
            pl.BlockSpec((rows, conv_ch), lambda b, c: (b * nc + c, OFF_XBC // conv_ch)),
            pl.BlockSpec((rows, c_inner), lambda b, c: (b * nc + c, OFF_Z // c_inner)),
            pl.BlockSpec((rows, LANES), row),
            pl.BlockSpec((C_CONV, conv_ch), const),
            pl.BlockSpec((1, conv_ch), const),
            pl.BlockSpec((1, LANES), const),
            pl.BlockSpec((1, LANES), const),
            pl.BlockSpec((1, c_inner), const),
            pl.BlockSpec((1, c_inner), const),
            pl.BlockSpec((LANES, c_inner), const),
        ],
        out_specs=pl.BlockSpec((rows, c_inner), row),
        scratch_shapes=[pltpu.VMEM((16, conv_ch), F32), pltpu.VMEM((C_STATE, c_inner), F32)],
        compiler_params=_cparams(2),
        name="mamba_ssd",
    )(proj, proj, dt_raw, conv_w, conv_b, dt_bias, a_log, d_skip_x, norm_g, expand)


def _merge_kernel(ya_ref, yb_ref, yc_ref, g0_ref, g1_ref, g2_ref, wa_ref, wb_ref, wc_ref, o_ref):
    pa = jnp.dot(ya_ref[...], wa_ref[...], preferred_element_type=F32)
    pb = jnp.dot(yb_ref[...], wb_ref[...], preferred_element_type=F32)
    pc = jnp.dot(yc_ref[...], wc_ref[...], preferred_element_type=F32)
    o_ref[...] = (g0_ref[...].astype(F32) * pa + g1_ref[...].astype(F32) * pb
                  + g2_ref[...].astype(F32) * pc).astype(BF16)


def _merge(ya, yb, yc, proj, w_pa, w_pb, w_pc, layer, *, bm, bn):
    t = ya.shape[0]
    d = w_pa.shape[2]
    gate_spec = lambda k: pl.BlockSpec((bm, bn), lambda i, j: (i, (OFF_GATE + k * d) // bn + j))
    return pl.pallas_call(
        _merge_kernel,
        out_shape=jax.ShapeDtypeStruct((t, d), BF16),
        grid=(t // bm, d // bn),
        in_specs=[
            pl.BlockSpec((bm, ya.shape[1]), lambda i, j: (i, 0)),
            pl.BlockSpec((bm, yb.shape[1]), lambda i, j: (i, 0)),
            pl.BlockSpec((bm, yc.shape[1]), lambda i, j: (i, 0)),
            gate_spec(0), gate_spec(1), gate_spec(2),
            pl.BlockSpec((None, w_pa.shape[1], bn), lambda i, j: (layer, 0, j)),
            pl.BlockSpec((None, w_pb.shape[1], bn), lambda i, j: (layer, 0, j)),
            pl.BlockSpec((None, w_pc.shape[1], bn), lambda i, j: (layer, 0, j)),
        ],
        out_specs=pl.BlockSpec((bm, bn), lambda i, j: (i, j)),
        compiler_params=_cparams(2),
        name="merge",
    )(ya, yb, yc, proj, proj, proj, w_pa, w_pb, w_pc)


def _outproj_kernel(m_ref, w_ref, h_ref, o_ref):
    o_ref[...] = h_ref[...] + jnp.dot(m_ref[...], w_ref[...], preferred_element_type=F32)


def _outproj(merged, w_out, h, layer, *, bm, bn):
    t, d = h.shape
    return pl.pallas_call(
        _outproj_kernel,
        out_shape=jax.ShapeDtypeStruct((t, d), F32),
        grid=(t // bm, d // bn),
        in_specs=[
            pl.BlockSpec((bm, merged.shape[1]), lambda i, j: (i, 0)),
            pl.BlockSpec((None, merged.shape[1], bn), lambda i, j: (layer, 0, j)),
            pl.BlockSpec((bm, bn), lambda i, j: (i, j)),
        ],
        out_specs=pl.BlockSpec((bm, bn), lambda i, j: (i, j)),
        compiler_params=_cparams(2),
        name="outproj",
    )(merged, w_out, h)


def _final_norm_kernel(x_ref, g_ref, o_ref):
    o_ref[...] = _rmsnorm_f32(x_ref[...], g_ref[...])


def _final_norm(h, g, *, bm):
    t, d = h.shape
    return pl.pallas_call(
        _final_norm_kernel,
        out_shape=jax.ShapeDtypeStruct((t, d), F32),
        grid=(t // bm,),
        in_specs=[pl.BlockSpec((bm, d), lambda i: (i, 0)), pl.BlockSpec((1, d), lambda i: (0, 0))],
        out_specs=pl.BlockSpec((bm, d), lambda i: (i, 0)),
        compiler_params=_cparams(1),
        name="final_norm",
    )(h, g)


def kernel(x, rel_bias, final_norm, ffn1_norm, ffn1_wi, ffn1_wo, mix_norm, w_in, sgu_ln_g, sgu_ln_b, sgu_w, sgu_b, diff_lambda, diff_subln, conv_w, conv_b, dt_bias, a_log, d_skip, ssm_norm, w_pa, w_pb, w_pc, w_out, ffn2_norm, ffn2_wi, ffn2_wo):
    batch, seq, d = x.shape
    depth = ffn1_wi.shape[0]
    t = batch * seq
    heads_c = dt_bias.shape[1]
    a_width = sgu_ln_g.shape[1]
    c_inner = ssm_norm.shape[1]
    conv_ch = conv_w.shape[2]
    d_ff = ffn1_wo.shape[1]

    bm = min(1024, t)
    bf = 512
    bq = 512
    bk = 512
    assert t % bm == 0 and d_ff % bf == 0 and seq % bk == 0 and bk % bq == 0 and bq % LANES == 0
    assert w_in.shape[2] == OFF_GATE + heads_c + 3 * d and conv_ch == OFF_AU

    o_au, o_av, o_q, o_z, o_xbc = 0, a_width, 2 * a_width, 5 * a_width, 5 * a_width + c_inner
    o_dt = o_xbc + conv_ch
    o_g = o_dt + heads_c
    w_main = jnp.concatenate(
        [w_in[:, :, o_xbc:o_dt], w_in[:, :, o_au:o_av], w_in[:, :, o_z:o_xbc], w_in[:, :, o_av:o_q],
         w_in[:, :, o_q:o_z], w_in[:, :, o_g:]], axis=2).astype(BF16)
    w_dt = jnp.pad(w_in[:, :, o_dt:o_g], ((0, 0), (0, 0), (0, LANES - heads_c))).astype(BF16)
    wi1, wo1 = ffn1_wi.astype(BF16), ffn1_wo.astype(BF16)
    wi2, wo2 = ffn2_wi.astype(BF16), ffn2_wo.astype(BF16)
    wpa, wpb, wpc, wout = w_pa.astype(BF16), w_pb.astype(BF16), w_pc.astype(BF16), w_out.astype(BF16)

    bsb = jnp.repeat(jnp.swapaxes(sgu_b, 1, 2), a_width // A_GROUPS, axis=2)
    pad_h = ((0, 0), (0, LANES - heads_c))
    dtb = jnp.pad(dt_bias, pad_h)
    alog = jnp.pad(a_log, pad_h)
    dsk_x = jnp.repeat(d_skip, C_HEAD_DIM, axis=1)
    expand = np.zeros((LANES, c_inner), np.float32)
    expand[np.arange(c_inner) // C_HEAD_DIM, np.arange(c_inner)] = 1.0
    expand = jnp.asarray(expand, BF16)
    bias_tiles = _bias_tiles(rel_bias, _bucket_tiles(bq, bk))
    subln_t = jnp.broadcast_to(diff_subln[:, :, None], diff_subln.shape + (bq,))

    h = x.reshape(t, d)
    for l in range(depth):
        lam_init = 0.8 - 0.6 * math.exp(-0.3 * l)
        h = _ffn(h, ffn1_norm[l][None], wi1, wo1, l, bm=bm, bf=bf)
        proj, dt_raw = _inproj(h, mix_norm[l][None], w_main, w_dt, l, bm=bm, bn=1024)
        ya = _sgu(proj, sgu_ln_g[l][None], sgu_ln_b[l][None], sgu_w[l], bsb[l], rows=2 * CHUNK)
        yb = _attn(proj, bias_tiles, diff_lambda[l], subln_t[l],
                   batch=batch, seq=seq, bq=bq, hpb=2, lam_init=lam_init)
        yc = _mamba(proj, dt_raw, conv_w[l], conv_b[l][None], dtb[l][None], alog[l][None],
                    dsk_x[l][None], ssm_norm[l][None], expand, batch=batch, seq=seq, rows=2 * CHUNK)
        merged = _merge(ya, yb, yc, proj, wpa, wpb, wpc, l, bm=bm, bn=512)
        h = _outproj(merged, wout, h, l, bm=bm, bn=1024)
        h = _ffn(h, ffn2_norm[l][None], wi2, wo2, l, bm=bm, bf=bf)
    return _final_norm(h, final_norm[None], bm=bm).reshape(batch, seq, d)
```

```python
import functools
import math

import numpy as np
import jax
import jax.numpy as jnp
from jax import lax
from jax.experimental import pallas as pl
from jax.experimental.pallas import tpu as pltpu

F32 = jnp.float32
BF16 = jnp.bfloat16

EPS = 1e-6
CHUNK = 128
LANES = 128
BF16_ROWS = 16
A_GROUPS = 8
B_HEAD_DIM = 64
C_HEAD_DIM = 64
C_GROUPS = 4
C_STATE = 128
C_CONV = 4
REL_BUCKETS = 32
REL_MAX_DIST = 128
REL_EXACT = REL_BUCKETS // 2
NEG_BIG = -1e30
LOG2E = math.log2(math.e)
VMEM_LIMIT_V7X = 56 * 1024 * 1024

OFF_XBC, OFF_AU, OFF_Z, OFF_AV, OFF_Q, OFF_K, OFF_V, OFF_GATE = 0, 3072, 4096, 6144, 7168, 8192, 9216, 10240


def _cparams(n_axes):
    return pltpu.CompilerParams(dimension_semantics=("arbitrary",) * n_axes,
                                vmem_limit_bytes=VMEM_LIMIT_V7X)


def _rmsnorm_f32(x, g):
    ms = jnp.mean(x * x, axis=-1, keepdims=True)
    return (x * lax.rsqrt(ms + EPS)) * g


def _ffn_kernel(x_ref, g_ref, wg_ref, wu_ref, wo_ref, o_ref, xn_ref):
    j = pl.program_id(1)

    @pl.when(j == 0)
    def _():
        x = x_ref[...]
        xn_ref[...] = _rmsnorm_f32(x, g_ref[...]).astype(BF16)
        o_ref[...] = x

    xn = xn_ref[...]
    gate = jnp.dot(xn, wg_ref[...].astype(BF16), preferred_element_type=F32)
    up = jnp.dot(xn, wu_ref[...].astype(BF16), preferred_element_type=F32)
    act = ((0.5 * gate) * jax.nn.sigmoid(gate) * up).astype(BF16)
    o_ref[...] += jnp.dot(act, wo_ref[...].astype(BF16), preferred_element_type=F32)


def _ffn(h, g, wi, wo, layer, *, bm, bf):
    t, d = h.shape
    d_ff = wo.shape[1]
    nj = d_ff // bf
    return pl.pallas_call(
        _ffn_kernel,
        out_shape=jax.ShapeDtypeStruct((t, d), F32),
        grid=(t // bm, nj),
        in_specs=[
            pl.BlockSpec((bm, d), lambda i, j: (i, 0)),
            pl.BlockSpec((1, d), lambda i, j: (0, 0)),
            pl.BlockSpec((None, d, bf), lambda i, j: (layer, 0, j)),
            pl.BlockSpec((None, d, bf), lambda i, j: (layer, 0, j + nj)),
            pl.BlockSpec((None, bf, d), lambda i, j: (layer, j, 0)),
        ],
        out_specs=pl.BlockSpec((bm, d), lambda i, j: (i, 0)),
        scratch_shapes=[pltpu.VMEM((bm, d), BF16)],
        compiler_params=_cparams(2),
        name="ffn",
    )(h, g, wi, wi, wo)


def _inproj_kernel(x_ref, g_ref, w_ref, wdt_ref, o_ref, dt_ref, xn_ref, *, nj_plain):
    j = pl.program_id(1)

    @pl.when(j == 0)
    def _():
        xn = _rmsnorm_f32(x_ref[...], g_ref[...]).astype(BF16)
        xn_ref[...] = xn
        dt_ref[...] = jnp.dot(xn, wdt_ref[...], preferred_element_type=F32)

    acc = jnp.dot(xn_ref[...], w_ref[...], preferred_element_type=F32)
    o_ref[...] = jnp.where(j >= nj_plain, jax.nn.sigmoid(acc), acc).astype(BF16)


def _inproj(h, g, w_main, w_dt, layer, *, bm, bn):
    t, d = h.shape
    n = w_main.shape[2]
    return pl.pallas_call(
        functools.partial(_inproj_kernel, nj_plain=OFF_GATE // bn),
        out_shape=(jax.ShapeDtypeStruct((t, n), BF16), jax.ShapeDtypeStruct((t, LANES), F32)),
        grid=(t // bm, n // bn),
        in_specs=[
            pl.BlockSpec((bm, d), lambda i, j: (i, 0)),
            pl.BlockSpec((1, d), lambda i, j: (0, 0)),
            pl.BlockSpec((None, d, bn), lambda i, j: (layer, 0, j)),
            pl.BlockSpec((None, d, LANES), lambda i, j: (layer, 0, 0)),
        ],
        out_specs=(pl.BlockSpec((bm, bn), lambda i, j: (i, j)),
                   pl.BlockSpec((bm, LANES), lambda i, j: (i, 0))),
        scratch_shapes=[pltpu.VMEM((bm, d), BF16)],
        compiler_params=_cparams(2),
        name="inproj",
    )(h, g, w_main, w_dt)


def _sgu_kernel(u_ref, v_ref, lng_ref, lnb_ref, w_ref, bsb_ref, o_ref):
    rows = u_ref.shape[0]
    u = jax.nn.gelu(u_ref[...].astype(F32))
    v = jax.nn.gelu(v_ref[...].astype(F32))
    mu = jnp.mean(v, axis=-1, keepdims=True)
    var = jnp.mean(jnp.square(v - mu), axis=-1, keepdims=True)
    vn = (((v - mu) * lax.rsqrt(var + EPS)) * lng_ref[...] + lnb_ref[...]).astype(BF16)
    r_i = lax.broadcasted_iota(jnp.int32, (CHUNK, CHUNK), 0)
    c_i = lax.broadcasted_iota(jnp.int32, (CHUNK, CHUNK), 1)
    tri = r_i >= c_i
    for g in range(A_GROUPS):
        wm = jnp.where(tri, w_ref[g], 0.0).astype(BF16)
        cs = slice(g * LANES, (g + 1) * LANES)
        for c in range(rows // CHUNK):
            rs = slice(c * CHUNK, (c + 1) * CHUNK)
            mixed = jnp.dot(wm, vn[rs, cs], preferred_element_type=F32) + bsb_ref[:, cs]
            o_ref[rs, cs] = (u[rs, cs] * mixed).astype(BF16)


def _sgu(proj, ln_g, ln_b, w_s, bsb, *, rows):
    t = proj.shape[0]
    aw = ln_g.shape[1]
    return pl.pallas_call(
        _sgu_kernel,
        out_shape=jax.ShapeDtypeStruct((t, aw), BF16),
        grid=(t // rows,),
        in_specs=[
            pl.BlockSpec((rows, aw), lambda i: (i, OFF_AU // aw)),
            pl.BlockSpec((rows, aw), lambda i: (i, OFF_AV // aw)),
            pl.BlockSpec((1, aw), lambda i: (0, 0)),
            pl.BlockSpec((1, aw), lambda i: (0, 0)),
            pl.BlockSpec((A_GROUPS, CHUNK, CHUNK), lambda i: (0, 0, 0)),
            pl.BlockSpec((CHUNK, aw), lambda i: (0, 0)),
        ],
        out_specs=pl.BlockSpec((rows, aw), lambda i: (i, 0)),
        compiler_params=_cparams(1),
        name="sgu",
    )(proj, proj, ln_g, ln_b, w_s, bsb)


def _t5_bucket_np(dist):
    n = np.maximum(dist, 0)
    nf = np.maximum(n, 1).astype(np.float64)
    large = REL_EXACT + (np.log(nf / REL_EXACT) / math.log(REL_MAX_DIST / REL_EXACT)
                         * (REL_BUCKETS - REL_EXACT)).astype(np.int32)
    large = np.minimum(large, REL_BUCKETS - 1)
    return np.where(n < REL_EXACT, n, large).astype(np.int32)


def _bucket_tiles(bq, bk):
    last_bucket_from = int(np.argmax(_t5_bucket_np(np.arange(4 * REL_MAX_DIST)) == REL_BUCKETS - 1))
    n_tiles = -(-(bk - 1 + last_bucket_from) // bq)
    c = np.arange(bk)[:, None]
    r = np.arange(bq)[None, :]
    tiles = []
    for t in range(n_tiles):
        dist = t * bq + r - c
        tiles.append(np.where(dist >= 0, _t5_bucket_np(dist), REL_BUCKETS))
    assert _t5_bucket_np(np.array([n_tiles * bq - (bk - 1)]))[0] == REL_BUCKETS - 1
    return np.stack(tiles).astype(np.int32)


BIAS_ROWS = 64


def _bias_tiles_kernel(rb_ref, bkt_ref, o_ref, *, present):
    hd = pl.program_id(0)
    n_bias, bk, bq = o_ref.shape
    for t in range(n_bias - 1):
        for rc in range(bk // BIAS_ROWS):
            rs = slice(rc * BIAS_ROWS, (rc + 1) * BIAS_ROWS)
            bt = bkt_ref[t, rs, :]
            tile = jnp.where(bt == REL_BUCKETS, NEG_BIG, 0.0).astype(F32)
            for b in present[t][rc]:
                tile = jnp.where(bt == b, rb_ref[b, hd] * LOG2E, tile)
            o_ref[t, rs, :] = tile
    o_ref[n_bias - 1] = jnp.full((bk, bq), rb_ref[REL_BUCKETS - 1, hd] * LOG2E, F32)


def _bias_tiles(rel_bias, buckets_np):
    n_tiles, bk, bq = buckets_np.shape
    heads = rel_bias.shape[1]
    present = tuple(tuple(tuple(int(b) for b in np.unique(buckets_np[t, rc * BIAS_ROWS:(rc + 1) * BIAS_ROWS])
                                if b < REL_BUCKETS)
                          for rc in range(bk // BIAS_ROWS)) for t in range(n_tiles))
    return pl.pallas_call(
        functools.partial(_bias_tiles_kernel, present=present),
        out_shape=jax.ShapeDtypeStruct((heads, n_tiles + 1, bk, bq), F32),
        grid=(heads,),
        in_specs=[pl.BlockSpec(memory_space=pltpu.SMEM),
                  pl.BlockSpec((n_tiles, bk, bq), lambda h: (0, 0, 0))],
        out_specs=pl.BlockSpec((None, n_tiles + 1, bk, bq), lambda h: (h, 0, 0, 0)),
        compiler_params=_cparams(1),
        name="bias_tiles",
    )(rel_bias, jnp.asarray(buckets_np))


def _attn_kernel(lam_ref, sub_ref, bias_ref, q_ref, k_ref, v_ref, o_ref,
                 vt_ref, qqt_ref, s_ref, p_ref, cm_ref, m_ref, al_ref, acc_ref, *, lam_init):
    hw = 2 * B_HEAD_DIM
    bq = q_ref.shape[0]
    hpb, bk = s_ref.shape[0], s_ref.shape[1]
    seq = k_ref.shape[0]
    n_bias = bias_ref.shape[1]
    qi = pl.program_id(2)
    cols = [slice(hh * hw, (hh + 1) * hw) for hh in range(hpb)]

    @pl.when(qi == 0)
    def _():
        pad_row = lax.broadcasted_iota(jnp.int32, (vt_ref.shape[2] - hw, bk), 0)
        ones_row = jnp.where(pad_row == 0, 1.0, 0.0).astype(BF16)
        for hh in range(hpb):
            for c in range(seq // LANES):
                blk, off = divmod(c * LANES, bk)
                vt_ref[hh, blk, 0:hw, off:off + LANES] = (
                    v_ref[c * LANES:(c + 1) * LANES, cols[hh]].astype(F32).T.astype(BF16))
            for blk in range(seq // bk):
                vt_ref[hh, blk, hw:, :] = ones_row

    dim = lax.broadcasted_iota(jnp.int32, (hw, bq), 0)
    for hh in range(hpb):
        qt = (q_ref[:, cols[hh]].astype(F32) * (B_HEAD_DIM ** -0.5 * LOG2E)).T
        qqt_ref[hh] = jnp.concatenate([jnp.where(dim < B_HEAD_DIM, qt, 0.0),
                                       jnp.where(dim >= B_HEAD_DIM, qt, 0.0)], axis=1).astype(BF16)
    n_blocks = (qi * bq) // bk + 1

    def scores(hh, j):
        kb = k_ref[pl.ds(pl.multiple_of(j * bk, bk), bk), cols[hh]]
        s = jnp.dot(kb, qqt_ref[hh], preferred_element_type=F32)
        bias = bias_ref[hh, jnp.minimum((qi * bq - j * bk) // bq, n_bias - 1)]
        s = s + jnp.concatenate([bias, bias], axis=1)
        s_ref[hh] = s
        cm_ref[hh] = jnp.max(s, axis=0, keepdims=True)

    def probabilities(hh):
        m_prev = m_ref[hh]
        m_next = jnp.maximum(m_prev, cm_ref[hh])
        m_ref[hh] = m_next
        al_ref[hh] = jnp.exp2(m_prev - m_next)
        p_ref[hh] = jnp.exp2(s_ref[hh] - m_next).astype(BF16)

    def weighted_values(hh, j):
        vt = vt_ref[hh, j]
        acc_ref[hh] = al_ref[hh] * acc_ref[hh] + jnp.dot(vt, p_ref[hh], preferred_element_type=F32)

    m_ref[...] = jnp.full(m_ref.shape, NEG_BIG, F32)
    al_ref[...] = jnp.ones(al_ref.shape, F32)
    acc_ref[...] = jnp.zeros(acc_ref.shape, F32)
    p_ref[...] = jnp.zeros(p_ref.shape, BF16)
    for hh in range(hpb):
        scores(hh, 0)

    def body(j, carry):
        for hh in range(hpb):
            weighted_values(hh, jnp.maximum(j - 1, 0))
        for hh in range(hpb):
            probabilities(hh)
        for hh in range(hpb):
            scores(hh, j + 1)
        return carry

    lax.fori_loop(0, n_blocks - 1, body, 0)
    for hh in range(hpb):
        weighted_values(hh, jnp.maximum(n_blocks - 2, 0))
        probabilities(hh)

    lam_p = lam_ref[...]
    lam = (jnp.exp(jnp.sum(lam_p[0:1] * lam_p[1:2], axis=1, keepdims=True))
           - jnp.exp(jnp.sum(lam_p[2:3] * lam_p[3:4], axis=1, keepdims=True)) + lam_init)
    for hh in range(hpb):
        weighted_values(hh, n_blocks - 1)
        o = acc_ref[hh, 0:hw, :] * (1.0 / acc_ref[hh, hw:hw + 1, :])
        attn = o[:, :bq] - lam * o[:, bq:]
        ms = jnp.mean(attn * attn, axis=0, keepdims=True)
        y = ((attn * lax.rsqrt(ms + EPS)) * sub_ref[...]) * (1.0 - lam_init)
        o_ref[:, cols[hh]] = y.T.astype(BF16)


def _attn(proj, bias_tiles, lam_p, subln_t, *, batch, seq, bq, hpb, lam_init):
    t = proj.shape[0]
    hw = 2 * B_HEAD_DIM
    heads, n_bias, bk, _ = bias_tiles.shape
    nq = seq // bq
    bw = hpb * hw
    return pl.pallas_call(
        functools.partial(_attn_kernel, lam_init=lam_init),
        out_shape=jax.ShapeDtypeStruct((t, heads * hw), BF16),
        grid=(batch, heads // hpb, nq),
        in_specs=[
            pl.BlockSpec((4, B_HEAD_DIM), lambda b, h, i: (0, 0)),
            pl.BlockSpec((hw, bq), lambda b, h, i: (0, 0)),
            pl.BlockSpec((hpb, n_bias, bk, bq), lambda b, h, i: (h, 0, 0, 0)),
            pl.BlockSpec((bq, bw), lambda b, h, i: (b * nq + i, OFF_Q // bw + h)),
            pl.BlockSpec((seq, bw), lambda b, h, i: (b, OFF_K // bw + h)),
            pl.BlockSpec((seq, bw), lambda b, h, i: (b, OFF_V // bw + h)),
        ],
        out_specs=pl.BlockSpec((bq, bw), lambda b, h, i: (b * nq + i, h)),
        scratch_shapes=[pltpu.VMEM((hpb, seq // bk, hw + BF16_ROWS, bk), BF16),
                        pltpu.VMEM((hpb, hw, 2 * bq), BF16),
                        pltpu.VMEM((hpb, bk, 2 * bq), F32),
                        pltpu.VMEM((hpb, bk, 2 * bq), BF16),
                        pltpu.VMEM((hpb, 1, 2 * bq), F32), pltpu.VMEM((hpb, 1, 2 * bq), F32),
                        pltpu.VMEM((hpb, 1, 2 * bq), F32),
                        pltpu.VMEM((hpb, hw + BF16_ROWS, 2 * bq), F32)],
        compiler_params=_cparams(3),
        name="diff_attn",
    )(lam_p, subln_t, bias_tiles, proj, proj, proj)


def _split_dot(x, e_bf16, passes):
    out = None
    r = x
    for _ in range(passes):
        hi = r.astype(BF16)
        part = jnp.dot(hi, e_bf16, preferred_element_type=F32)
        out = part if out is None else out + part
        r = r - hi.astype(F32)
    return out


def _mamba_kernel(xbc_ref, z_ref, dt_ref, cw_ref, cb_ref, dtb_ref, alog_ref, dsk_ref, ng_ref, e_ref,
                  o_ref, tail_ref, st_ref):
    rows, c_inner = z_ref.shape
    gw = c_inner // C_GROUPS
    heads_per_group = gw // C_HEAD_DIM
    c = pl.program_id(1)

    @pl.when(c == 0)
    def _():
        tail_ref[...] = jnp.zeros_like(tail_ref)
        st_ref[...] = jnp.zeros_like(st_ref)

    r_i = lax.broadcasted_iota(jnp.int32, (CHUNK, CHUNK), 0)
    c_i = lax.broadcasted_iota(jnp.int32, (CHUNK, CHUNK), 1)
    tri = r_i >= c_i
    lower = tri.astype(F32)
    upper = (r_i <= c_i).astype(F32)
    shift = jnp.concatenate([(r_i - c_i == d).astype(BF16) for d in range(C_CONV - 1, 0, -1)], axis=0)
    row8 = lax.broadcasted_iota(jnp.int32, (8, xbc_ref.shape[1]), 0)
    lane_lo = lax.broadcasted_iota(jnp.int32, (CHUNK, LANES), 1) < C_HEAD_DIM
    neg_a = -jnp.exp(alog_ref[...]) * LOG2E
    e = e_ref[...]
    hp = lax.Precision.HIGHEST
    prev = st_ref[...]

    for u in range(rows // CHUNK):
        rs = slice(u * CHUNK, (u + 1) * CHUNK)
        x_cur = xbc_ref[rs, :]
        delayed = jnp.dot(shift, x_cur, preferred_element_type=F32)
        if u == 0:
            tail = tail_ref[8:16, :]
        else:
            tail = xbc_ref[u * CHUNK - 16:u * CHUNK, :].astype(F32)[8:16, :]
        conv = cb_ref[...] + cw_ref[C_CONV - 1:C_CONV, :] * x_cur.astype(F32)
        conv_top = jnp.zeros_like(tail)
        for k in range(C_CONV - 1):
            d = C_CONV - 1 - k
            conv = conv + cw_ref[k:k + 1, :] * delayed[k * CHUNK:(k + 1) * CHUNK]
            conv_top = conv_top + cw_ref[k:k + 1, :] * jnp.where(row8 < d, pltpu.roll(tail, d, 0), 0.0)
        conv = jnp.concatenate([conv[0:8] + conv_top, conv[8:]], axis=0)
        xbc = conv * jax.nn.sigmoid(conv)
        xs = xbc[:, :c_inner]
        bm = xbc[:, c_inner:c_inner + C_GROUPS * C_STATE]
        cm = xbc[:, c_inner + C_GROUPS * C_STATE:]

        dt = jax.nn.softplus(dt_ref[rs, :] + dtb_ref[...])
        a = dt * neg_a
        a_cs = jnp.dot(lower, a, precision=hp, preferred_element_type=F32)
        a_cs_t = jnp.dot(a.T, upper, precision=hp, preferred_element_type=F32)

        dt_x = _split_dot(dt, e, 2)
        acs_x = _split_dot(a_cs, e, 3)
        last_x = acs_x[CHUNK - 1:CHUNK, :]
        xd = xs * dt_x
        xd_b = xd.astype(BF16)
        xdd_b = (xd * jnp.exp2(last_x - acs_x)).astype(BF16)
        prev_b = prev.astype(BF16)
        eacs = jnp.exp2(acs_x)

        y_parts = []
        st_parts = []
        for g in range(C_GROUPS):
            gs = slice(g * gw, (g + 1) * gw)
            b_g = bm[:, g * C_STATE:(g + 1) * C_STATE]
            c_g = cm[:, g * C_STATE:(g + 1) * C_STATE].astype(BF16)
            cb = lax.dot_general(c_g, b_g.astype(BF16), (((1,), (1,)), ((), ())), preferred_element_type=F32)
            st_parts.append(jnp.dot(b_g.T.astype(BF16), xdd_b[:, gs], preferred_element_type=F32))
            y_off = jnp.dot(c_g, prev_b[:, gs], preferred_element_type=F32) * eacs[:, gs]
            tiles = []
            for pr in range(heads_per_group // 2):
                t_idx = g * (heads_per_group // 2) + pr
                xt = xd_b[:, t_idx * LANES:(t_idx + 1) * LANES]
                res = []
                for half in range(2):
                    hd = 2 * t_idx + half
                    seg = a_cs[:, hd:hd + 1] - a_cs_t[hd:hd + 1, :]
                    decay = jnp.where(tri, jnp.exp2(jnp.minimum(seg, 0.0)), 0.0)
                    res.append(jnp.dot((cb * decay).astype(BF16), xt, preferred_element_type=F32))
                tiles.append(jnp.where(lane_lo, res[0], res[1]))
            y_parts.append(jnp.concatenate(tiles, axis=1) + y_off)

        prev = prev * jnp.exp2(last_x) + jnp.concatenate(st_parts, axis=1)
        y = jnp.concatenate(y_parts, axis=1) + dsk_ref[...] * xs
        zf = z_ref[rs, :].astype(F32)
        gated = y * (zf * jax.nn.sigmoid(zf))
        o_ref[rs, :] = _rmsnorm_f32(gated, ng_ref[...]).astype(BF16)

    st_ref[...] = prev
    tail_ref[...] = xbc_ref[rows - 16:rows, :].astype(F32)


def _mamba(proj, dt_raw, conv_w, conv_b, dt_bias, a_log, d_skip_x, norm_g, expand, *, batch, seq, rows):
    t = proj.shape[0]
    c_inner = norm_g.shape[1]
    conv_ch = conv_w.shape[1]
    nc = seq // rows
    row = lambda b, c: (b * nc + c, 0)
    const = lambda b, c: (0, 0)
    return pl.pallas_call(
        _mamba_kernel,
        out_shape=jax.ShapeDtypeStruct((t, c_inner), BF16),
        grid=(batch, nc),
        in_specs=[
            pl.BlockSpec((rows, conv_ch), lambda b, c: (b * nc + c, OFF_XBC // conv_ch)),
            pl.BlockSpec((rows, c_inner), lambda b, c: (b * nc + c, OFF_Z // c_inner)),
            pl.BlockSpec((rows, LANES), row),
            pl.BlockSpec((C_CONV, conv_ch), const),
            pl.BlockSpec((1, conv_ch), const),
            pl.BlockSpec((1, LANES), const),
            pl.BlockSpec((1, LANES), const),
            pl.BlockSpec((1, c_inner), const),
            pl.BlockSpec((1, c_inner), const),
            pl.BlockSpec((LANES, c_inner), const),
        ],
        out_specs=pl.BlockSpec((rows, c_inner), row),
        scratch_shapes=[pltpu.VMEM((16, conv_ch), F32), pltpu.VMEM((C_STATE, c_inner), F32)],
        compiler_params=_cparams(2),
        name="mamba_ssd",
    )(proj, proj, dt_raw, conv_w, conv_b, dt_bias, a_log, d_skip_x, norm_g, expand)


def _merge_kernel(ya_ref, yb_ref, yc_ref, g0_ref, g1_ref, g2_ref, wa_ref, wb_ref, wc_ref, o_ref):
    pa = jnp.dot(ya_ref[...], wa_ref[...].astype(BF16), preferred_element_type=F32)
    pb = jnp.dot(yb_ref[...], wb_ref[...].astype(BF16), preferred_element_type=F32)
    pc = jnp.dot(yc_ref[...], wc_ref[...].astype(BF16), preferred_element_type=F32)
    o_ref[...] = (g0_ref[...].astype(F32) * pa + g1_ref[...].astype(F32) * pb
                  + g2_ref[...].astype(F32) * pc).astype(BF16)


def _merge(ya, yb, yc, proj, w_pa, w_pb, w_pc, layer, *, bm, bn):
    t = ya.shape[0]
    d = w_pa.shape[2]
    gate_spec = lambda k: pl.BlockSpec((bm, bn), lambda i, j: (i, (OFF_GATE + k * d) // bn + j))
    return pl.pallas_call(
        _merge_kernel,
        out_shape=jax.ShapeDtypeStruct((t, d), BF16),
        grid=(t // bm, d // bn),
        in_specs=[
            pl.BlockSpec((bm, ya.shape[1]), lambda i, j: (i, 0)),
            pl.BlockSpec((bm, yb.shape[1]), lambda i, j: (i, 0)),
            pl.BlockSpec((bm, yc.shape[1]), lambda i, j: (i, 0)),
            gate_spec(0), gate_spec(1), gate_spec(2),
            pl.BlockSpec((None, w_pa.shape[1], bn), lambda i, j: (layer, 0, j)),
            pl.BlockSpec((None, w_pb.shape[1], bn), lambda i, j: (layer, 0, j)),
            pl.BlockSpec((None, w_pc.shape[1], bn), lambda i, j: (layer, 0, j)),
        ],
        out_specs=pl.BlockSpec((bm, bn), lambda i, j: (i, j)),
        compiler_params=_cparams(2),
        name="merge",
    )(ya, yb, yc, proj, proj, proj, w_pa, w_pb, w_pc)


def _outproj_kernel(m_ref, w_ref, h_ref, o_ref):
    o_ref[...] = h_ref[...] + jnp.dot(m_ref[...], w_ref[...].astype(BF16), preferred_element_type=F32)


def _outproj(merged, w_out, h, layer, *, bm, bn):
    t, d = h.shape
    return pl.pallas_call(
        _outproj_kernel,
        out_shape=jax.ShapeDtypeStruct((t, d), F32),
        grid=(t // bm, d // bn),
        in_specs=[
            pl.BlockSpec((bm, merged.shape[1]), lambda i, j: (i, 0)),
            pl.BlockSpec((None, merged.shape[1], bn), lambda i, j: (layer, 0, j)),
            pl.BlockSpec((bm, bn), lambda i, j: (i, j)),
        ],
        out_specs=pl.BlockSpec((bm, bn), lambda i, j: (i, j)),
        compiler_params=_cparams(2),
        name="outproj",
    )(merged, w_out, h)


def _final_norm_kernel(x_ref, g_ref, o_ref):
    o_ref[...] = _rmsnorm_f32(x_ref[...], g_ref[...])


def _final_norm(h, g, *, bm):
    t, d = h.shape
    return pl.pallas_call(
        _final_norm_kernel,
        out_shape=jax.ShapeDtypeStruct((t, d), F32),
        grid=(t // bm,),
        in_specs=[pl.BlockSpec((bm, d), lambda i: (i, 0)), pl.BlockSpec((1, d), lambda i: (0, 0))],
        out_specs=pl.BlockSpec((bm, d), lambda i: (i, 0)),
        compiler_params=_cparams(1),
        name="final_norm",
    )(h, g)


def kernel(x, rel_bias, final_norm, ffn1_norm, ffn1_wi, ffn1_wo, mix_norm, w_in, sgu_ln_g, sgu_ln_b, sgu_w, sgu_b, diff_lambda, diff_subln, conv_w, conv_b, dt_bias, a_log, d_skip, ssm_norm, w_pa, w_pb, w_pc, w_out, ffn2_norm, ffn2_wi, ffn2_wo):
    batch, seq, d = x.shape
    depth = ffn1_wi.shape[0]
    t = batch * seq
    heads_c = dt_bias.shape[1]
    a_width = sgu_ln_g.shape[1]
    c_inner = ssm_norm.shape[1]
    conv_ch = conv_w.shape[2]
    d_ff = ffn1_wo.shape[1]

    bm = min(1024, t)
    bf = 256
    bq = 512
    bk = 512
    assert t % bm == 0 and d_ff % bf == 0 and seq % bk == 0 and bk % bq == 0 and bq % LANES == 0
    assert w_in.shape[2] == OFF_GATE + heads_c + 3 * d and conv_ch == OFF_AU

    o_au, o_av, o_q, o_z, o_xbc = 0, a_width, 2 * a_width, 5 * a_width, 5 * a_width + c_inner
    o_dt = o_xbc + conv_ch
    o_g = o_dt + heads_c
    w_main = jnp.concatenate(
        [w_in[:, :, o_xbc:o_dt], w_in[:, :, o_au:o_av], w_in[:, :, o_z:o_xbc], w_in[:, :, o_av:o_q],
         w_in[:, :, o_q:o_z], w_in[:, :, o_g:]], axis=2).astype(BF16)
    w_dt = jnp.pad(w_in[:, :, o_dt:o_g], ((0, 0), (0, 0), (0, LANES - heads_c))).astype(BF16)
    wi1, wo1, wi2, wo2 = ffn1_wi, ffn1_wo, ffn2_wi, ffn2_wo
    wpa, wpb, wpc, wout = w_pa, w_pb, w_pc, w_out

    bsb = jnp.repeat(jnp.swapaxes(sgu_b, 1, 2), a_width // A_GROUPS, axis=2)
    pad_h = ((0, 0), (0, LANES - heads_c))
    dtb = jnp.pad(dt_bias, pad_h)
    alog = jnp.pad(a_log, pad_h)
    dsk_x = jnp.repeat(d_skip, C_HEAD_DIM, axis=1)
    expand = np.zeros((LANES, c_inner), np.float32)
    expand[np.arange(c_inner) // C_HEAD_DIM, np.arange(c_inner)] = 1.0
    expand = jnp.asarray(expand, BF16)
    bias_tiles = _bias_tiles(rel_bias, _bucket_tiles(bq, bk))
    subln_t = jnp.broadcast_to(diff_subln[:, :, None], diff_subln.shape + (bq,))

    h = x.reshape(t, d)
    for l in range(depth):
        lam_init = 0.8 - 0.6 * math.exp(-0.3 * l)
        h = _ffn(h, ffn1_norm[l][None], wi1, wo1, l, bm=bm, bf=bf)
        proj, dt_raw = _inproj(h, mix_norm[l][None], w_main, w_dt, l, bm=bm, bn=2048)
        ya = _sgu(proj, sgu_ln_g[l][None], sgu_ln_b[l][None], sgu_w[l], bsb[l], rows=2 * CHUNK)
        yb = _attn(proj, bias_tiles, diff_lambda[l], subln_t[l],
                   batch=batch, seq=seq, bq=bq, hpb=2, lam_init=lam_init)
        yc = _mamba(proj, dt_raw, conv_w[l], conv_b[l][None], dtb[l][None], alog[l][None],
                    dsk_x[l][None], ssm_norm[l][None], expand, batch=batch, seq=seq, rows=2 * CHUNK)
        merged = _merge(ya, yb, yc, proj, wpa, wpb, wpc, l, bm=bm, bn=512)
        h = _outproj(merged, wout, h, l, bm=bm, bn=1024)
        h = _ffn(h, ffn2_norm[l][None], wi2, wo2, l, bm=bm, bf=bf)
    return _final_norm(h, final_norm[None], bm=bm).reshape(batch, seq, d)
```

```python
import functools
import math

import numpy as np
import jax
import jax.numpy as jnp
from jax import lax
from jax.experimental import pallas as pl
from jax.experimental.pallas import tpu as pltpu

F32 = jnp.float32
BF16 = jnp.bfloat16

EPS = 1e-6
CHUNK = 128
LANES = 128
BF16_ROWS = 16
A_GROUPS = 8
B_HEAD_DIM = 64
C_HEAD_DIM = 64
C_GROUPS = 4
C_STATE = 128
C_CONV = 4
REL_BUCKETS = 32
REL_MAX_DIST = 128
REL_EXACT = REL_BUCKETS // 2
NEG_BIG = -1e30
LOG2E = math.log2(math.e)
VMEM_LIMIT_V7X = 56 * 1024 * 1024
ATTN_VMEM_LIMIT_V7X = 60 * 1024 * 1024

OFF_XBC, OFF_AU, OFF_Z, OFF_AV, OFF_Q, OFF_K, OFF_V, OFF_GATE = 0, 3072, 4096, 6144, 7168, 8192, 9216, 10240


def _cparams(n_axes):
    return pltpu.CompilerParams(dimension_semantics=("arbitrary",) * n_axes,
                                vmem_limit_bytes=VMEM_LIMIT_V7X)


def _rmsnorm_f32(x, g):
    ms = jnp.mean(x * x, axis=-1, keepdims=True)
    return (x * lax.rsqrt(ms + EPS)) * g


def _ffn_kernel(x_ref, g_ref, wg_ref, wu_ref, wo_ref, o_ref, xn_ref):
    j = pl.program_id(1)

    @pl.when(j == 0)
    def _():
        x = x_ref[...]
        xn_ref[...] = _rmsnorm_f32(x, g_ref[...]).astype(BF16)
        o_ref[...] = x

    xn = xn_ref[...]
    gate = jnp.dot(xn, wg_ref[...], preferred_element_type=F32)
    up = jnp.dot(xn, wu_ref[...], preferred_element_type=F32)
    act = ((0.5 * gate) * jax.nn.sigmoid(gate) * up).astype(BF16)
    o_ref[...] += jnp.dot(act, wo_ref[...], preferred_element_type=F32)


def _ffn(h, g, wi, wo, layer, *, bm, bf):
    t, d = h.shape
    d_ff = wo.shape[1]
    nj = d_ff // bf
    return pl.pallas_call(
        _ffn_kernel,
        out_shape=jax.ShapeDtypeStruct((t, d), F32),
        grid=(t // bm, nj),
        in_specs=[
            pl.BlockSpec((bm, d), lambda i, j: (i, 0)),
            pl.BlockSpec((1, d), lambda i, j: (0, 0)),
            pl.BlockSpec((None, d, bf), lambda i, j: (layer, 0, j)),
            pl.BlockSpec((None, d, bf), lambda i, j: (layer, 0, j + nj)),
            pl.BlockSpec((None, bf, d), lambda i, j: (layer, j, 0)),
        ],
        out_specs=pl.BlockSpec((bm, d), lambda i, j: (i, 0)),
        scratch_shapes=[pltpu.VMEM((bm, d), BF16)],
        compiler_params=_cparams(2),
        name="ffn",
    )(h, g, wi, wi, wo)


def _inproj_kernel(x_ref, g_ref, w_ref, wdt_ref, o_ref, dt_ref, xn_ref, *, nj_plain):
    j = pl.program_id(1)

    @pl.when(j == 0)
    def _():
        xn = _rmsnorm_f32(x_ref[...], g_ref[...]).astype(BF16)
        xn_ref[...] = xn
        dt_ref[...] = jnp.dot(xn, wdt_ref[...], preferred_element_type=F32)

    acc = jnp.dot(xn_ref[...], w_ref[...], preferred_element_type=F32)
    o_ref[...] = jnp.where(j >= nj_plain, jax.nn.sigmoid(acc), acc).astype(BF16)


def _inproj(h, g, w_main, w_dt, layer, *, bm, bn):
    t, d = h.shape
    n = w_main.shape[2]
    return pl.pallas_call(
        functools.partial(_inproj_kernel, nj_plain=OFF_GATE // bn),
        out_shape=(jax.ShapeDtypeStruct((t, n), BF16), jax.ShapeDtypeStruct((t, LANES), F32)),
        grid=(t // bm, n // bn),
        in_specs=[
            pl.BlockSpec((bm, d), lambda i, j: (i, 0)),
            pl.BlockSpec((1, d), lambda i, j: (0, 0)),
            pl.BlockSpec((None, d, bn), lambda i, j: (layer, 0, j)),
            pl.BlockSpec((None, d, LANES), lambda i, j: (layer, 0, 0)),
        ],
        out_specs=(pl.BlockSpec((bm, bn), lambda i, j: (i, j)),
                   pl.BlockSpec((bm, LANES), lambda i, j: (i, 0))),
        scratch_shapes=[pltpu.VMEM((bm, d), BF16)],
        compiler_params=_cparams(2),
        name="inproj",
    )(h, g, w_main, w_dt)


def _sgu_kernel(u_ref, v_ref, lng_ref, lnb_ref, w_ref, bsb_ref, o_ref):
    rows = u_ref.shape[0]
    u = jax.nn.gelu(u_ref[...].astype(F32))
    v = jax.nn.gelu(v_ref[...].astype(F32))
    mu = jnp.mean(v, axis=-1, keepdims=True)
    var = jnp.mean(jnp.square(v - mu), axis=-1, keepdims=True)
    vn = (((v - mu) * lax.rsqrt(var + EPS)) * lng_ref[...] + lnb_ref[...]).astype(BF16)
    r_i = lax.broadcasted_iota(jnp.int32, (CHUNK, CHUNK), 0)
    c_i = lax.broadcasted_iota(jnp.int32, (CHUNK, CHUNK), 1)
    tri = r_i >= c_i
    for g in range(A_GROUPS):
        wm = jnp.where(tri, w_ref[g], 0.0).astype(BF16)
        cs = slice(g * LANES, (g + 1) * LANES)
        for c in range(rows // CHUNK):
            rs = slice(c * CHUNK, (c + 1) * CHUNK)
            mixed = jnp.dot(wm, vn[rs, cs], preferred_element_type=F32) + bsb_ref[:, cs]
            o_ref[rs, cs] = (u[rs, cs] * mixed).astype(BF16)


def _sgu(proj, ln_g, ln_b, w_s, bsb, *, rows):
    t = proj.shape[0]
    aw = ln_g.shape[1]
    return pl.pallas_call(
        _sgu_kernel,
        out_shape=jax.ShapeDtypeStruct((t, aw), BF16),
        grid=(t // rows,),
        in_specs=[
            pl.BlockSpec((rows, aw), lambda i: (i, OFF_AU // aw)),
            pl.BlockSpec((rows, aw), lambda i: (i, OFF_AV // aw)),
            pl.BlockSpec((1, aw), lambda i: (0, 0)),
            pl.BlockSpec((1, aw), lambda i: (0, 0)),
            pl.BlockSpec((A_GROUPS, CHUNK, CHUNK), lambda i: (0, 0, 0)),
            pl.BlockSpec((CHUNK, aw), lambda i: (0, 0)),
        ],
        out_specs=pl.BlockSpec((rows, aw), lambda i: (i, 0)),
        compiler_params=_cparams(1),
        name="sgu",
    )(proj, proj, ln_g, ln_b, w_s, bsb)


def _t5_bucket_np(dist):
    n = np.maximum(dist, 0)
    nf = np.maximum(n, 1).astype(np.float64)
    large = REL_EXACT + (np.log(nf / REL_EXACT) / math.log(REL_MAX_DIST / REL_EXACT)
                         * (REL_BUCKETS - REL_EXACT)).astype(np.int32)
    large = np.minimum(large, REL_BUCKETS - 1)
    return np.where(n < REL_EXACT, n, large).astype(np.int32)


def _bucket_tiles(bq, bk):
    last_bucket_from = int(np.argmax(_t5_bucket_np(np.arange(4 * REL_MAX_DIST)) == REL_BUCKETS - 1))
    n_tiles = -(-(bk - 1 + last_bucket_from) // bq)
    c = np.arange(bk)[:, None]
    r = np.arange(bq)[None, :]
    tiles = []
    for t in range(n_tiles):
        dist = t * bq + r - c
        tiles.append(np.where(dist >= 0, _t5_bucket_np(dist), REL_BUCKETS))
    assert _t5_bucket_np(np.array([n_tiles * bq - (bk - 1)]))[0] == REL_BUCKETS - 1
    return np.stack(tiles).astype(np.int32)


BIAS_ROWS = 64


def _bias_tiles_kernel(rb_ref, bkt_ref, o_ref, *, present):
    hd = pl.program_id(0)
    n_bias, bk, bq = o_ref.shape
    for t in range(n_bias - 1):
        for rc in range(bk // BIAS_ROWS):
            rs = slice(rc * BIAS_ROWS, (rc + 1) * BIAS_ROWS)
            bt = bkt_ref[t, rs, :]
            tile = jnp.where(bt == REL_BUCKETS, NEG_BIG, 0.0).astype(F32)
            for b in present[t][rc]:
                tile = jnp.where(bt == b, rb_ref[b, hd] * LOG2E, tile)
            o_ref[t, rs, :] = tile
    o_ref[n_bias - 1] = jnp.full((bk, bq), rb_ref[REL_BUCKETS - 1, hd] * LOG2E, F32)


def _bias_tiles(rel_bias, buckets_np):
    n_tiles, bk, bq = buckets_np.shape
    heads = rel_bias.shape[1]
    present = tuple(tuple(tuple(int(b) for b in np.unique(buckets_np[t, rc * BIAS_ROWS:(rc + 1) * BIAS_ROWS])
                                if b < REL_BUCKETS)
                          for rc in range(bk // BIAS_ROWS)) for t in range(n_tiles))
    return pl.pallas_call(
        functools.partial(_bias_tiles_kernel, present=present),
        out_shape=jax.ShapeDtypeStruct((heads, n_tiles + 1, bk, bq), F32),
        grid=(heads,),
        in_specs=[pl.BlockSpec(memory_space=pltpu.SMEM),
                  pl.BlockSpec((n_tiles, bk, bq), lambda h: (0, 0, 0))],
        out_specs=pl.BlockSpec((None, n_tiles + 1, bk, bq), lambda h: (h, 0, 0, 0)),
        compiler_params=_cparams(1),
        name="bias_tiles",
    )(rel_bias, jnp.asarray(buckets_np))


def _attn_kernel(lam_ref, sub_ref, bias_ref, q_ref, k_ref, v_ref, o_ref,
                 vt_ref, qqt_ref, s_ref, p_ref, cm_ref, m_ref, al_ref, acc_ref, *, lam_init):
    hw = 2 * B_HEAD_DIM
    hpb, bk = s_ref.shape[0], s_ref.shape[1]
    bq = s_ref.shape[2] // 2
    seq = k_ref.shape[0]
    nq = seq // bq
    n_bias = bias_ref.shape[1]
    cols = [slice(hh * hw, (hh + 1) * hw) for hh in range(hpb)]
    last_block = lambda i: (i * bq) // bk
    n_pairs = sum(last_block(i) + 1 for i in range(nq))

    pad_row = lax.broadcasted_iota(jnp.int32, (vt_ref.shape[2] - hw, bk), 0)
    ones_row = jnp.where(pad_row == 0, 1.0, 0.0).astype(BF16)
    dim = lax.broadcasted_iota(jnp.int32, (hw, bq), 0)
    for hh in range(hpb):
        for c in range(seq // LANES):
            blk, off = divmod(c * LANES, bk)
            vt_ref[hh, blk, 0:hw, off:off + LANES] = (
                v_ref[c * LANES:(c + 1) * LANES, cols[hh]].astype(F32).T.astype(BF16))
        for blk in range(seq // bk):
            vt_ref[hh, blk, hw:, :] = ones_row
        for i in range(nq):
            qt = (q_ref[i * bq:(i + 1) * bq, cols[hh]].astype(F32) * (B_HEAD_DIM ** -0.5 * LOG2E)).T
            qqt_ref[hh, i] = jnp.concatenate([jnp.where(dim < B_HEAD_DIM, qt, 0.0),
                                              jnp.where(dim >= B_HEAD_DIM, qt, 0.0)], axis=1).astype(BF16)
    m_ref[...] = jnp.full(m_ref.shape, NEG_BIG, F32)
    al_ref[...] = jnp.ones(al_ref.shape, F32)
    acc_ref[...] = jnp.zeros(acc_ref.shape, F32)
    p_ref[...] = jnp.zeros(p_ref.shape, BF16)

    def scores(hh, i, j):
        kb = k_ref[pl.ds(pl.multiple_of(j * bk, bk), bk), cols[hh]]
        s = jnp.dot(kb, qqt_ref[hh, i], preferred_element_type=F32)
        bias = bias_ref[hh, jnp.minimum((i * bq - j * bk) // bq, n_bias - 1)]
        s = s + jnp.concatenate([bias, bias], axis=1)
        s_ref[hh] = s
        cm_ref[hh] = jnp.max(s, axis=0, keepdims=True)

    def probabilities(hh, i):
        m_prev = m_ref[hh, i]
        m_next = jnp.maximum(m_prev, cm_ref[hh])
        m_ref[hh, i] = m_next
        al_ref[hh] = jnp.exp2(m_prev - m_next)
        p_ref[hh] = jnp.exp2(s_ref[hh] - m_next).astype(BF16)

    def weighted_values(hh, i, j):
        acc_ref[hh, i] = al_ref[hh] * acc_ref[hh, i] + jnp.dot(vt_ref[hh, j], p_ref[hh],
                                                                preferred_element_type=F32)

    for hh in range(hpb):
        scores(hh, 0, 0)

    def body(t, carry):
        ip, jp, ic, jc = carry
        wrap = jc >= last_block(ic)
        i_next = jnp.where(wrap, ic + 1, ic)
        j_next = jnp.where(wrap, 0, jc + 1)
        for hh in range(hpb):
            weighted_values(hh, ip, jp)
        for hh in range(hpb):
            probabilities(hh, ic)
        for hh in range(hpb):
            scores(hh, i_next, j_next)
        return ic, jc, i_next, j_next

    zero = jnp.int32(0)
    ip, jp, ic, jc = lax.fori_loop(0, n_pairs - 1, body, (zero, zero, zero, zero))
    for hh in range(hpb):
        weighted_values(hh, ip, jp)
        probabilities(hh, ic)
    for hh in range(hpb):
        weighted_values(hh, ic, jc)

    lam_p = lam_ref[...]
    lam = (jnp.exp(jnp.sum(lam_p[0:1] * lam_p[1:2], axis=1, keepdims=True))
           - jnp.exp(jnp.sum(lam_p[2:3] * lam_p[3:4], axis=1, keepdims=True)) + lam_init)
    for hh in range(hpb):
        for i in range(nq):
            o = acc_ref[hh, i, 0:hw, :] * (1.0 / acc_ref[hh, i, hw:hw + 1, :])
            attn = o[:, :bq] - lam * o[:, bq:]
            ms = jnp.mean(attn * attn, axis=0, keepdims=True)
            y = ((attn * lax.rsqrt(ms + EPS)) * sub_ref[...]) * (1.0 - lam_init)
            o_ref[i * bq:(i + 1) * bq, cols[hh]] = y.T.astype(BF16)


def _attn(proj, bias_tiles, lam_p, subln_t, *, batch, seq, bq, hpb, lam_init):
    t = proj.shape[0]
    hw = 2 * B_HEAD_DIM
    heads, n_bias, bk, _ = bias_tiles.shape
    nq = seq // bq
    bw = hpb * hw
    return pl.pallas_call(
        functools.partial(_attn_kernel, lam_init=lam_init),
        out_shape=jax.ShapeDtypeStruct((t, heads * hw), BF16),
        grid=(batch, heads // hpb),
        in_specs=[
            pl.BlockSpec((4, B_HEAD_DIM), lambda b, h: (0, 0)),
            pl.BlockSpec((hw, bq), lambda b, h: (0, 0)),
            pl.BlockSpec((hpb, n_bias, bk, bq), lambda b, h: (h, 0, 0, 0)),
            pl.BlockSpec((seq, bw), lambda b, h: (b, OFF_Q // bw + h)),
            pl.BlockSpec((seq, bw), lambda b, h: (b, OFF_K // bw + h)),
            pl.BlockSpec((seq, bw), lambda b, h: (b, OFF_V // bw + h)),
        ],
        out_specs=pl.BlockSpec((seq, bw), lambda b, h: (b, h)),
        scratch_shapes=[pltpu.VMEM((hpb, seq // bk, hw + BF16_ROWS, bk), BF16),
                        pltpu.VMEM((hpb, nq, hw, 2 * bq), BF16),
                        pltpu.VMEM((hpb, bk, 2 * bq), F32),
                        pltpu.VMEM((hpb, bk, 2 * bq), BF16),
                        pltpu.VMEM((hpb, 1, 2 * bq), F32),
                        pltpu.VMEM((hpb, nq, 1, 2 * bq), F32),
                        pltpu.VMEM((hpb, 1, 2 * bq), F32),
                        pltpu.VMEM((hpb, nq, hw + BF16_ROWS, 2 * bq), F32)],
        compiler_params=pltpu.CompilerParams(dimension_semantics=("arbitrary", "arbitrary"),
                                             vmem_limit_bytes=ATTN_VMEM_LIMIT_V7X),
        name="diff_attn",
    )(lam_p, subln_t, bias_tiles, proj, proj, proj)


def _split_dot(x, e_bf16, passes):
    out = None
    r = x
    for _ in range(passes):
        hi = r.astype(BF16)
        part = jnp.dot(hi, e_bf16, preferred_element_type=F32)
        out = part if out is None else out + part
        r = r - hi.astype(F32)
    return out


def _mamba_kernel(xbc_ref, z_ref, dt_ref, cw_ref, cb_ref, dtb_ref, alog_ref, dsk_ref, ng_ref, e_ref,
                  o_ref, tail_ref, st_ref):
    rows, c_inner = z_ref.shape
    gw = c_inner // C_GROUPS
    heads_per_group = gw // C_HEAD_DIM
    c = pl.program_id(1)

    @pl.when(c == 0)
    def _():
        tail_ref[...] = jnp.zeros_like(tail_ref)
        st_ref[...] = jnp.zeros_like(st_ref)

    r_i = lax.broadcasted_iota(jnp.int32, (CHUNK, CHUNK), 0)
    c_i = lax.broadcasted_iota(jnp.int32, (CHUNK, CHUNK), 1)
    tri = r_i >= c_i
    lower = tri.astype(F32)
    upper = (r_i <= c_i).astype(F32)
    shift = jnp.concatenate([(r_i - c_i == d).astype(BF16) for d in range(C_CONV - 1, 0, -1)], axis=0)
    row8 = lax.broadcasted_iota(jnp.int32, (8, xbc_ref.shape[1]), 0)
    lane_lo = lax.broadcasted_iota(jnp.int32, (CHUNK, LANES), 1) < C_HEAD_DIM
    neg_a = -jnp.exp(alog_ref[...]) * LOG2E
    e = e_ref[...]
    hp = lax.Precision.HIGHEST
    prev = st_ref[...]

    for u in range(rows // CHUNK):
        rs = slice(u * CHUNK, (u + 1) * CHUNK)
        x_cur = xbc_ref[rs, :]
        delayed = jnp.dot(shift, x_cur, preferred_element_type=F32)
        if u == 0:
            tail = tail_ref[8:16, :]
        else:
            tail = xbc_ref[u * CHUNK - 16:u * CHUNK, :].astype(F32)[8:16, :]
        conv = cb_ref[...] + cw_ref[C_CONV - 1:C_CONV, :] * x_cur.astype(F32)
        conv_top = jnp.zeros_like(tail)
        for k in range(C_CONV - 1):
            d = C_CONV - 1 - k
            conv = conv + cw_ref[k:k + 1, :] * delayed[k * CHUNK:(k + 1) * CHUNK]
            conv_top = conv_top + cw_ref[k:k + 1, :] * jnp.where(row8 < d, pltpu.roll(tail, d, 0), 0.0)
        conv = jnp.concatenate([conv[0:8] + conv_top, conv[8:]], axis=0)
        xbc = conv * jax.nn.sigmoid(conv)
        xs = xbc[:, :c_inner]
        bm = xbc[:, c_inner:c_inner + C_GROUPS * C_STATE]
        cm = xbc[:, c_inner + C_GROUPS * C_STATE:]

        dt = jax.nn.softplus(dt_ref[rs, :] + dtb_ref[...])
        a = dt * neg_a
        a_cs = jnp.dot(lower, a, precision=hp, preferred_element_type=F32)
        a_cs_t = jnp.dot(a.T, upper, precision=hp, preferred_element_type=F32)

        dt_x = _split_dot(dt, e, 2)
        acs_x = _split_dot(a_cs, e, 3)
        last_x = acs_x[CHUNK - 1:CHUNK, :]
        xd = xs * dt_x
        xd_b = xd.astype(BF16)
        xdd_b = (xd * jnp.exp2(last_x - acs_x)).astype(BF16)
        prev_b = prev.astype(BF16)
        eacs = jnp.exp2(acs_x)

        y_parts = []
        st_parts = []
        for g in range(C_GROUPS):
            gs = slice(g * gw, (g + 1) * gw)
            b_g = bm[:, g * C_STATE:(g + 1) * C_STATE]
            c_g = cm[:, g * C_STATE:(g + 1) * C_STATE].astype(BF16)
            cb = lax.dot_general(c_g, b_g.astype(BF16), (((1,), (1,)), ((), ())), preferred_element_type=F32)
            st_parts.append(jnp.dot(b_g.T.astype(BF16), xdd_b[:, gs], preferred_element_type=F32))
            y_off = jnp.dot(c_g, prev_b[:, gs], preferred_element_type=F32) * eacs[:, gs]
            tiles = []
            for pr in range(heads_per_group // 2):
                t_idx = g * (heads_per_group // 2) + pr
                xt = xd_b[:, t_idx * LANES:(t_idx + 1) * LANES]
                res = []
                for half in range(2):
                    hd = 2 * t_idx + half
                    seg = a_cs[:, hd:hd + 1] - a_cs_t[hd:hd + 1, :]
                    decay = jnp.where(tri, jnp.exp2(jnp.minimum(seg, 0.0)), 0.0)
                    res.append(jnp.dot((cb * decay).astype(BF16), xt, preferred_element_type=F32))
                tiles.append(jnp.where(lane_lo, res[0], res[1]))
            y_parts.append(jnp.concatenate(tiles, axis=1) + y_off)

        prev = prev * jnp.exp2(last_x) + jnp.concatenate(st_parts, axis=1)
        y = jnp.concatenate(y_parts, axis=1) + dsk_ref[...] * xs
        zf = z_ref[rs, :].astype(F32)
        gated = y * (zf * jax.nn.sigmoid(zf))
        o_ref[rs, :] = _rmsnorm_f32(gated, ng_ref[...]).astype(BF16)

    st_ref[...] = prev
    tail_ref[...] = xbc_ref[rows - 16:rows, :].astype(F32)


def _mamba(proj, dt_raw, conv_w, conv_b, dt_bias, a_log, d_skip_x, norm_g, expand, *, batch, seq, rows):
    t = proj.shape[0]
    c_inner = norm_g.shape[1]
    conv_ch = conv_w.shape[1]
    nc = seq // rows
    row = lambda b, c: (b * nc + c, 0)
    const = lambda b, c: (0, 0)
    return pl.pallas_call(
        _mamba_kernel,
        out_shape=jax.ShapeDtypeStruct((t, c_inner), BF16),
        grid=(batch, nc),
        in_specs=[
            pl.BlockSpec((rows, conv_ch), lambda b, c: (b * nc + c, OFF_XBC // conv_ch)),
            pl.BlockSpec((rows, c_inner), lambda b, c: (b * nc + c, OFF_Z // c_inner)),
            pl.BlockSpec((rows, LANES), row),
            pl.BlockSpec((C_CONV, conv_ch), const),
            pl.BlockSpec((1, conv_ch), const),
            pl.BlockSpec((1, LANES), const),
            pl.BlockSpec((1, LANES), const),
            pl.BlockSpec((1, c_inner), const),
            pl.BlockSpec((1, c_inner), const),
            pl.BlockSpec((LANES, c_inner), const),
        ],
        out_specs=pl.BlockSpec((rows, c_inner), row),
        scratch_shapes=[pltpu.VMEM((16, conv_ch), F32), pltpu.VMEM((C_STATE, c_inner), F32)],
        compiler_params=_cparams(2),
        name="mamba_ssd",
    )(proj, proj, dt_raw, conv_w, conv_b, dt_bias, a_log, d_skip_x, norm_g, expand)


def _merge_kernel(ya_ref, yb_ref, yc_ref, g0_ref, g1_ref, g2_ref, wa_ref, wb_ref, wc_ref, o_ref):
    pa = jnp.dot(ya_ref[...], wa_ref[...], preferred_element_type=F32)
    pb = jnp.dot(yb_ref[...], wb_ref[...], preferred_element_type=F32)
    pc = jnp.dot(yc_ref[...], wc_ref[...], preferred_element_type=F32)
    o_ref[...] = (g0_ref[...].astype(F32) * pa + g1_ref[...].astype(F32) * pb
                  + g2_ref[...].astype(F32) * pc).astype(BF16)


def _merge(ya, yb, yc, proj, w_pa, w_pb, w_pc, layer, *, bm, bn):
    t = ya.shape[0]
    d = w_pa.shape[2]
    gate_spec = lambda k: pl.BlockSpec((bm, bn), lambda i, j: (i, (OFF_GATE + k * d) // bn + j))
    return pl.pallas_call(
        _merge_kernel,
        out_shape=jax.ShapeDtypeStruct((t, d), BF16),
        grid=(t // bm, d // bn),
        in_specs=[
            pl.BlockSpec((bm, ya.shape[1]), lambda i, j: (i, 0)),
            pl.BlockSpec((bm, yb.shape[1]), lambda i, j: (i, 0)),
            pl.BlockSpec((bm, yc.shape[1]), lambda i, j: (i, 0)),
            gate_spec(0), gate_spec(1), gate_spec(2),
            pl.BlockSpec((None, w_pa.shape[1], bn), lambda i, j: (layer, 0, j)),
            pl.BlockSpec((None, w_pb.shape[1], bn), lambda i, j: (layer, 0, j)),
            pl.BlockSpec((None, w_pc.shape[1], bn), lambda i, j: (layer, 0, j)),
        ],
        out_specs=pl.BlockSpec((bm, bn), lambda i, j: (i, j)),
        compiler_params=_cparams(2),
        name="merge",
    )(ya, yb, yc, proj, proj, proj, w_pa, w_pb, w_pc)


def _outproj_kernel(m_ref, w_ref, h_ref, o_ref):
    o_ref[...] = h_ref[...] + jnp.dot(m_ref[...], w_ref[...], preferred_element_type=F32)


def _outproj(merged, w_out, h, layer, *, bm, bn):
    t, d = h.shape
    return pl.pallas_call(
        _outproj_kernel,
        out_shape=jax.ShapeDtypeStruct((t, d), F32),
        grid=(t // bm, d // bn),
        in_specs=[
            pl.BlockSpec((bm, merged.shape[1]), lambda i, j: (i, 0)),
            pl.BlockSpec((None, merged.shape[1], bn), lambda i, j: (layer, 0, j)),
            pl.BlockSpec((bm, bn), lambda i, j: (i, j)),
        ],
        out_specs=pl.BlockSpec((bm, bn), lambda i, j: (i, j)),
        compiler_params=_cparams(2),
        name="outproj",
    )(merged, w_out, h)


def _final_norm_kernel(x_ref, g_ref, o_ref):
    o_ref[...] = _rmsnorm_f32(x_ref[...], g_ref[...])


def _final_norm(h, g, *, bm):
    t, d = h.shape
    return pl.pallas_call(
        _final_norm_kernel,
        out_shape=jax.ShapeDtypeStruct((t, d), F32),
        grid=(t // bm,),
        in_specs=[pl.BlockSpec((bm, d), lambda i: (i, 0)), pl.BlockSpec((1, d), lambda i: (0, 0))],
        out_specs=pl.BlockSpec((bm, d), lambda i: (i, 0)),
        compiler_params=_cparams(1),
        name="final_norm",
    )(h, g)


def kernel(x, rel_bias, final_norm, ffn1_norm, ffn1_wi, ffn1_wo, mix_norm, w_in, sgu_ln_g, sgu_ln_b, sgu_w, sgu_b, diff_lambda, diff_subln, conv_w, conv_b, dt_bias, a_log, d_skip, ssm_norm, w_pa, w_pb, w_pc, w_out, ffn2_norm, ffn2_wi, ffn2_wo):
    batch, seq, d = x.shape
    depth = ffn1_wi.shape[0]
    t = batch * seq
    heads_c = dt_bias.shape[1]
    a_width = sgu_ln_g.shape[1]
    c_inner = ssm_norm.shape[1]
    conv_ch = conv_w.shape[2]
    d_ff = ffn1_wo.shape[1]

    bm = min(1024, t)
    bf = 512
    bq = 512
    bk = 512
    assert t % bm == 0 and d_ff % bf == 0 and seq % bk == 0 and bk % bq == 0 and bq % LANES == 0
    assert w_in.shape[2] == OFF_GATE + heads_c + 3 * d and conv_ch == OFF_AU

    o_au, o_av, o_q, o_z, o_xbc = 0, a_width, 2 * a_width, 5 * a_width, 5 * a_width + c_inner
    o_dt = o_xbc + conv_ch
    o_g = o_dt + heads_c
    w_main = jnp.concatenate(
        [w_in[:, :, o_xbc:o_dt], w_in[:, :, o_au:o_av], w_in[:, :, o_z:o_xbc], w_in[:, :, o_av:o_q],
         w_in[:, :, o_q:o_z], w_in[:, :, o_g:]], axis=2).astype(BF16)
    w_dt = jnp.pad(w_in[:, :, o_dt:o_g], ((0, 0), (0, 0), (0, LANES - heads_c))).astype(BF16)
    wi1, wo1 = ffn1_wi.astype(BF16), ffn1_wo.astype(BF16)
    wi2, wo2 = ffn2_wi.astype(BF16), ffn2_wo.astype(BF16)
    wpa, wpb, wpc, wout = w_pa.astype(BF16), w_pb.astype(BF16), w_pc.astype(BF16), w_out.astype(BF16)

    bsb = jnp.repeat(jnp.swapaxes(sgu_b, 1, 2), a_width // A_GROUPS, axis=2)
    pad_h = ((0, 0), (0, LANES - heads_c))
    dtb = jnp.pad(dt_bias, pad_h)
    alog = jnp.pad(a_log, pad_h)
    dsk_x = jnp.repeat(d_skip, C_HEAD_DIM, axis=1)
    expand = np.zeros((LANES, c_inner), np.float32)
    expand[np.arange(c_inner) // C_HEAD_DIM, np.arange(c_inner)] = 1.0
    expand = jnp.asarray(expand, BF16)
    bias_tiles = _bias_tiles(rel_bias, _bucket_tiles(bq, bk))
    subln_t = jnp.broadcast_to(diff_subln[:, :, None], diff_subln.shape + (bq,))

    h = x.reshape(t, d)
    for l in range(depth):
        lam_init = 0.8 - 0.6 * math.exp(-0.3 * l)
        h = _ffn(h, ffn1_norm[l][None], wi1, wo1, l, bm=bm, bf=bf)
        proj, dt_raw = _inproj(h, mix_norm[l][None], w_main, w_dt, l, bm=bm, bn=2048)
        ya = _sgu(proj, sgu_ln_g[l][None], sgu_ln_b[l][None], sgu_w[l], bsb[l], rows=2 * CHUNK)
        yb = _attn(proj, bias_tiles, diff_lambda[l], subln_t[l],
                   batch=batch, seq=seq, bq=bq, hpb=2, lam_init=lam_init)
        yc = _mamba(proj, dt_raw, conv_w[l], conv_b[l][None], dtb[l][None], alog[l][None],
                    dsk_x[l][None], ssm_norm[l][None], expand, batch=batch, seq=seq, rows=2 * CHUNK)
        merged = _merge(ya, yb, yc, proj, wpa, wpb, wpc, l, bm=bm, bn=512)
        h = _outproj(merged, wout, h, l, bm=bm, bn=1024)
        h = _ffn(h, ffn2_norm[l][None], wi2, wo2, l, bm=bm, bf=bf)
    return _final_norm(h, final_norm[None], bm=bm).reshape(batch, seq, d)
```

```python
import functools
import math

import numpy as np
import jax
import jax.numpy as jnp
from jax import lax
from jax.experimental import pallas as pl
from jax.experimental.pallas import tpu as pltpu

F32 = jnp.float32
BF16 = jnp.bfloat16

EPS = 1e-6
CHUNK = 128
LANES = 128
BF16_ROWS = 16
A_GROUPS = 8
B_HEAD_DIM = 64
C_HEAD_DIM = 64
C_GROUPS = 4
C_STATE = 128
C_CONV = 4
REL_BUCKETS = 32
REL_MAX_DIST = 128
REL_EXACT = REL_BUCKETS // 2
NEG_BIG = -1e30
LOG2E = math.log2(math.e)
VMEM_LIMIT_V7X = 56 * 1024 * 1024
ATTN_VMEM_LIMIT_V7X = 60 * 1024 * 1024

OFF_XBC, OFF_AU, OFF_Z, OFF_AV, OFF_Q, OFF_K, OFF_V, OFF_GATE = 0, 3072, 4096, 6144, 7168, 8192, 9216, 10240


def _cparams(n_axes):
    return pltpu.CompilerParams(dimension_semantics=("arbitrary",) * n_axes,
                                vmem_limit_bytes=VMEM_LIMIT_V7X)


def _rmsnorm_f32(x, g):
    ms = jnp.mean(x * x, axis=-1, keepdims=True)
    return (x * lax.rsqrt(ms + EPS)) * g


def _ffn_kernel(x_ref, g_ref, wg_ref, wu_ref, wo_ref, o_ref, xn_ref):
    j = pl.program_id(1)

    @pl.when(j == 0)
    def _():
        x = x_ref[...]
        xn_ref[...] = _rmsnorm_f32(x, g_ref[...]).astype(BF16)
        o_ref[...] = x

    xn = xn_ref[...]
    gate = jnp.dot(xn, wg_ref[...], preferred_element_type=F32)
    up = jnp.dot(xn, wu_ref[...], preferred_element_type=F32)
    act = ((0.5 * gate) * jax.nn.sigmoid(gate) * up).astype(BF16)
    o_ref[...] += jnp.dot(act, wo_ref[...], preferred_element_type=F32)


def _ffn(h, g, wi, wo, layer, *, bm):
    t, d = h.shape
    bf = wi.shape[3]
    nj = wi.shape[1] // 2
    return pl.pallas_call(
        _ffn_kernel,
        out_shape=jax.ShapeDtypeStruct((t, d), F32),
        grid=(t // bm, nj),
        in_specs=[
            pl.BlockSpec((bm, d), lambda i, j: (i, 0)),
            pl.BlockSpec((1, d), lambda i, j: (0, 0)),
            pl.BlockSpec((None, None, d, bf), lambda i, j: (layer, j, 0, 0)),
            pl.BlockSpec((None, None, d, bf), lambda i, j: (layer, j + nj, 0, 0)),
            pl.BlockSpec((None, bf, d), lambda i, j: (layer, j, 0)),
        ],
        out_specs=pl.BlockSpec((bm, d), lambda i, j: (i, 0)),
        scratch_shapes=[pltpu.VMEM((bm, d), BF16)],
        compiler_params=_cparams(2),
        name="ffn",
    )(h, g, wi, wi, wo)


def _inproj_kernel(x_ref, g_ref, w_ref, wdt_ref, o_ref, dt_ref, xn_ref, *, nj_plain):
    j = pl.program_id(1)

    @pl.when(j == 0)
    def _():
        xn = _rmsnorm_f32(x_ref[...], g_ref[...]).astype(BF16)
        xn_ref[...] = xn
        dt_ref[...] = jnp.dot(xn, wdt_ref[...], preferred_element_type=F32)

    @pl.when(j < nj_plain)
    def _():
        o_ref[...] = jnp.dot(xn_ref[...], w_ref[...], preferred_element_type=F32).astype(BF16)

    @pl.when(j >= nj_plain)
    def _():
        o_ref[...] = jax.nn.sigmoid(jnp.dot(xn_ref[...], w_ref[...], preferred_element_type=F32)).astype(BF16)


def _inproj(h, g, w_main, w_dt, layer, *, bm, bn):
    t, d = h.shape
    n = w_main.shape[2]
    return pl.pallas_call(
        functools.partial(_inproj_kernel, nj_plain=OFF_GATE // bn),
        out_shape=(jax.ShapeDtypeStruct((t, n), BF16), jax.ShapeDtypeStruct((t, LANES), F32)),
        grid=(t // bm, n // bn),
        in_specs=[
            pl.BlockSpec((bm, d), lambda i, j: (i, 0)),
            pl.BlockSpec((1, d), lambda i, j: (0, 0)),
            pl.BlockSpec((None, d, bn), lambda i, j: (layer, 0, j)),
            pl.BlockSpec((None, d, LANES), lambda i, j: (layer, 0, 0)),
        ],
        out_specs=(pl.BlockSpec((bm, bn), lambda i, j: (i, j)),
                   pl.BlockSpec((bm, LANES), lambda i, j: (i, 0))),
        scratch_shapes=[pltpu.VMEM((bm, d), BF16)],
        compiler_params=_cparams(2),
        name="inproj",
    )(h, g, w_main, w_dt)


def _sgu_kernel(u_ref, v_ref, lng_ref, lnb_ref, w_ref, bsb_ref, o_ref):
    rows = u_ref.shape[0]
    u = jax.nn.gelu(u_ref[...].astype(F32))
    v = jax.nn.gelu(v_ref[...].astype(F32))
    mu = jnp.mean(v, axis=-1, keepdims=True)
    var = jnp.mean(jnp.square(v - mu), axis=-1, keepdims=True)
    vn = (((v - mu) * lax.rsqrt(var + EPS)) * lng_ref[...] + lnb_ref[...]).astype(BF16)
    r_i = lax.broadcasted_iota(jnp.int32, (CHUNK, CHUNK), 0)
    c_i = lax.broadcasted_iota(jnp.int32, (CHUNK, CHUNK), 1)
    tri = r_i >= c_i
    for g in range(A_GROUPS):
        wm = jnp.where(tri, w_ref[g], 0.0).astype(BF16)
        cs = slice(g * LANES, (g + 1) * LANES)
        for c in range(rows // CHUNK):
            rs = slice(c * CHUNK, (c + 1) * CHUNK)
            mixed = jnp.dot(wm, vn[rs, cs], preferred_element_type=F32) + bsb_ref[:, cs]
            o_ref[rs, cs] = (u[rs, cs] * mixed).astype(BF16)


def _sgu(proj, ln_g, ln_b, w_s, bsb, *, rows):
    t = proj.shape[0]
    aw = ln_g.shape[1]
    return pl.pallas_call(
        _sgu_kernel,
        out_shape=jax.ShapeDtypeStruct((t, aw), BF16),
        grid=(t // rows,),
        in_specs=[
            pl.BlockSpec((rows, aw), lambda i: (i, OFF_AU // aw)),
            pl.BlockSpec((rows, aw), lambda i: (i, OFF_AV // aw)),
            pl.BlockSpec((1, aw), lambda i: (0, 0)),
            pl.BlockSpec((1, aw), lambda i: (0, 0)),
            pl.BlockSpec((A_GROUPS, CHUNK, CHUNK), lambda i: (0, 0, 0)),
            pl.BlockSpec((CHUNK, aw), lambda i: (0, 0)),
        ],
        out_specs=pl.BlockSpec((rows, aw), lambda i: (i, 0)),
        compiler_params=_cparams(1),
        name="sgu",
    )(proj, proj, ln_g, ln_b, w_s, bsb)


def _t5_bucket_np(dist):
    n = np.maximum(dist, 0)
    nf = np.maximum(n, 1).astype(np.float64)
    large = REL_EXACT + (np.log(nf / REL_EXACT) / math.log(REL_MAX_DIST / REL_EXACT)
                         * (REL_BUCKETS - REL_EXACT)).astype(np.int32)
    large = np.minimum(large, REL_BUCKETS - 1)
    return np.where(n < REL_EXACT, n, large).astype(np.int32)


def _bucket_tiles(bq, bk):
    last_bucket_from = int(np.argmax(_t5_bucket_np(np.arange(4 * REL_MAX_DIST)) == REL_BUCKETS - 1))
    n_tiles = -(-(bk - 1 + last_bucket_from) // bq)
    c = np.arange(bk)[:, None]
    r = np.arange(bq)[None, :]
    tiles = []
    for t in range(n_tiles):
        dist = t * bq + r - c
        tiles.append(np.where(dist >= 0, _t5_bucket_np(dist), REL_BUCKETS))
    assert _t5_bucket_np(np.array([n_tiles * bq - (bk - 1)]))[0] == REL_BUCKETS - 1
    return np.stack(tiles).astype(np.int32)


BIAS_ROWS = 64


def _bias_tiles_kernel(rb_ref, bkt_ref, o_ref, *, present):
    hd = pl.program_id(0)
    n_bias, bk, bq = o_ref.shape
    for t in range(n_bias - 1):
        for rc in range(bk // BIAS_ROWS):
            rs = slice(rc * BIAS_ROWS, (rc + 1) * BIAS_ROWS)
            bt = bkt_ref[t, rs, :]
            tile = jnp.where(bt == REL_BUCKETS, NEG_BIG, 0.0).astype(F32)
            for b in present[t][rc]:
                tile = jnp.where(bt == b, rb_ref[b, hd] * LOG2E, tile)
            o_ref[t, rs, :] = tile
    o_ref[n_bias - 1] = jnp.full((bk, bq), rb_ref[REL_BUCKETS - 1, hd] * LOG2E, F32)


def _bias_tiles(rel_bias, buckets_np):
    n_tiles, bk, bq = buckets_np.shape
    heads = rel_bias.shape[1]
    present = tuple(tuple(tuple(int(b) for b in np.unique(buckets_np[t, rc * BIAS_ROWS:(rc + 1) * BIAS_ROWS])
                                if b < REL_BUCKETS)
                          for rc in range(bk // BIAS_ROWS)) for t in range(n_tiles))
    return pl.pallas_call(
        functools.partial(_bias_tiles_kernel, present=present),
        out_shape=jax.ShapeDtypeStruct((heads, n_tiles + 1, bk, bq), F32),
        grid=(heads,),
        in_specs=[pl.BlockSpec(memory_space=pltpu.SMEM),
                  pl.BlockSpec((n_tiles, bk, bq), lambda h: (0, 0, 0))],
        out_specs=pl.BlockSpec((None, n_tiles + 1, bk, bq), lambda h: (h, 0, 0, 0)),
        compiler_params=_cparams(1),
        name="bias_tiles",
    )(rel_bias, jnp.asarray(buckets_np))


def _attn_kernel(lam_ref, sub_ref, bias_ref, q_ref, k_ref, v_ref, o_ref,
                 vt_ref, qqt_ref, s_ref, p_ref, cm_ref, m_ref, al_ref, acc_ref, *, lam_init):
    hw = 2 * B_HEAD_DIM
    hpb, bk = s_ref.shape[0], s_ref.shape[1]
    bq = s_ref.shape[2] // 2
    seq = k_ref.shape[0]
    nq = seq // bq
    n_bias = bias_ref.shape[1]
    cols = [slice(hh * hw, (hh + 1) * hw) for hh in range(hpb)]
    last_block = lambda i: (i * bq) // bk
    n_pairs = sum(last_block(i) + 1 for i in range(nq))

    pad_row = lax.broadcasted_iota(jnp.int32, (vt_ref.shape[2] - hw, bk), 0)
    ones_row = jnp.where(pad_row == 0, 1.0, 0.0).astype(BF16)
    dim = lax.broadcasted_iota(jnp.int32, (hw, bq), 0)
    for hh in range(hpb):
        for c in range(seq // LANES):
            blk, off = divmod(c * LANES, bk)
            vt_ref[hh, blk, 0:hw, off:off + LANES] = (
                v_ref[c * LANES:(c + 1) * LANES, cols[hh]].astype(F32).T.astype(BF16))
        for blk in range(seq // bk):
            vt_ref[hh, blk, hw:, :] = ones_row
        for i in range(nq):
            qt = (q_ref[i * bq:(i + 1) * bq, cols[hh]].astype(F32) * (B_HEAD_DIM ** -0.5 * LOG2E)).T
            qqt_ref[hh, i] = jnp.concatenate([jnp.where(dim < B_HEAD_DIM, qt, 0.0),
                                              jnp.where(dim >= B_HEAD_DIM, qt, 0.0)], axis=1).astype(BF16)
    m_ref[...] = jnp.full(m_ref.shape, NEG_BIG, F32)
    al_ref[...] = jnp.ones(al_ref.shape, F32)
    acc_ref[...] = jnp.zeros(acc_ref.shape, F32)
    p_ref[...] = jnp.zeros(p_ref.shape, BF16)

    def scores(hh, i, j):
        kb = k_ref[pl.ds(pl.multiple_of(j * bk, bk), bk), cols[hh]]
        s = jnp.dot(kb, qqt_ref[hh, i], preferred_element_type=F32)
        bias = bias_ref[hh, jnp.minimum((i * bq - j * bk) // bq, n_bias - 1)]
        s = s + jnp.concatenate([bias, bias], axis=1)
        s_ref[hh] = s
        cm_ref[hh] = jnp.max(s, axis=0, keepdims=True)

    def probabilities(hh, i):
        m_prev = m_ref[hh, i]
        m_next = jnp.maximum(m_prev, cm_ref[hh])
        m_ref[hh, i] = m_next
        al_ref[hh] = jnp.exp2(m_prev - m_next)
        p_ref[hh] = jnp.exp2(s_ref[hh] - m_next).astype(BF16)

    def weighted_values(hh, i, j):
        acc_ref[hh, i] = al_ref[hh] * acc_ref[hh, i] + jnp.dot(vt_ref[hh, j], p_ref[hh],
                                                                preferred_element_type=F32)

    for hh in range(hpb):
        scores(hh, 0, 0)

    def body(t, carry):
        ip, jp, ic, jc = carry
        wrap = jc >= last_block(ic)
        i_next = jnp.where(wrap, ic + 1, ic)
        j_next = jnp.where(wrap, 0, jc + 1)
        for hh in range(hpb):
            weighted_values(hh, ip, jp)
        for hh in range(hpb):
            probabilities(hh, ic)
        for hh in range(hpb):
            scores(hh, i_next, j_next)
        return ic, jc, i_next, j_next

    zero = jnp.int32(0)
    ip, jp, ic, jc = lax.fori_loop(0, n_pairs - 1, body, (zero, zero, zero, zero))
    for hh in range(hpb):
        weighted_values(hh, ip, jp)
        probabilities(hh, ic)
    for hh in range(hpb):
        weighted_values(hh, ic, jc)

    lam_p = lam_ref[...]
    lam = (jnp.exp(jnp.sum(lam_p[0:1] * lam_p[1:2], axis=1, keepdims=True))
           - jnp.exp(jnp.sum(lam_p[2:3] * lam_p[3:4], axis=1, keepdims=True)) + lam_init)
    for hh in range(hpb):
        for i in range(nq):
            o = acc_ref[hh, i, 0:hw, :] * (1.0 / acc_ref[hh, i, hw:hw + 1, :])
            attn = o[:, :bq] - lam * o[:, bq:]
            ms = jnp.mean(attn * attn, axis=0, keepdims=True)
            y = ((attn * lax.rsqrt(ms + EPS)) * sub_ref[...]) * (1.0 - lam_init)
            o_ref[i * bq:(i + 1) * bq, cols[hh]] = y.T.astype(BF16)


def _attn(proj, bias_tiles, lam_p, subln_t, *, batch, seq, bq, hpb, lam_init):
    t = proj.shape[0]
    hw = 2 * B_HEAD_DIM
    heads, n_bias, bk, _ = bias_tiles.shape
    nq = seq // bq
    bw = hpb * hw
    return pl.pallas_call(
        functools.partial(_attn_kernel, lam_init=lam_init),
        out_shape=jax.ShapeDtypeStruct((t, heads * hw), BF16),
        grid=(batch, heads // hpb),
        in_specs=[
            pl.BlockSpec((4, B_HEAD_DIM), lambda b, h: (0, 0)),
            pl.BlockSpec((hw, bq), lambda b, h: (0, 0)),
            pl.BlockSpec((hpb, n_bias, bk, bq), lambda b, h: (h, 0, 0, 0)),
            pl.BlockSpec((seq, bw), lambda b, h: (b, OFF_Q // bw + h)),
            pl.BlockSpec((seq, bw), lambda b, h: (b, OFF_K // bw + h)),
            pl.BlockSpec((seq, bw), lambda b, h: (b, OFF_V // bw + h)),
        ],
        out_specs=pl.BlockSpec((seq, bw), lambda b, h: (b, h)),
        scratch_shapes=[pltpu.VMEM((hpb, seq // bk, hw + BF16_ROWS, bk), BF16),
                        pltpu.VMEM((hpb, nq, hw, 2 * bq), BF16),
                        pltpu.VMEM((hpb, bk, 2 * bq), F32),
                        pltpu.VMEM((hpb, bk, 2 * bq), BF16),
                        pltpu.VMEM((hpb, 1, 2 * bq), F32),
                        pltpu.VMEM((hpb, nq, 1, 2 * bq), F32),
                        pltpu.VMEM((hpb, 1, 2 * bq), F32),
                        pltpu.VMEM((hpb, nq, hw + BF16_ROWS, 2 * bq), F32)],
        compiler_params=pltpu.CompilerParams(dimension_semantics=("arbitrary", "arbitrary"),
                                             vmem_limit_bytes=ATTN_VMEM_LIMIT_V7X),
        name="diff_attn",
    )(lam_p, subln_t, bias_tiles, proj, proj, proj)


def _split_dot(x, e_bf16, passes):
    out = None
    r = x
    for _ in range(passes):
        hi = r.astype(BF16)
        part = jnp.dot(hi, e_bf16, preferred_element_type=F32)
        out = part if out is None else out + part
        r = r - hi.astype(F32)
    return out


def _mamba_kernel(xbc_ref, z_ref, dt_ref, cw_ref, cb_ref, dtb_ref, alog_ref, dsk_ref, ng_ref, e_ref,
                  o_ref, tail_ref, st_ref):
    rows, c_inner = z_ref.shape
    gw = c_inner // C_GROUPS
    heads_per_group = gw // C_HEAD_DIM
    c = pl.program_id(1)

    @pl.when(c == 0)
    def _():
        tail_ref[...] = jnp.zeros_like(tail_ref)
        st_ref[...] = jnp.zeros_like(st_ref)

    r_i = lax.broadcasted_iota(jnp.int32, (CHUNK, CHUNK), 0)
    c_i = lax.broadcasted_iota(jnp.int32, (CHUNK, CHUNK), 1)
    tri = r_i >= c_i
    lower = tri.astype(F32)
    upper = (r_i <= c_i).astype(F32)
    shift = jnp.concatenate([(r_i - c_i == d).astype(BF16) for d in range(C_CONV - 1, 0, -1)], axis=0)
    row8 = lax.broadcasted_iota(jnp.int32, (8, xbc_ref.shape[1]), 0)
    lane_lo = lax.broadcasted_iota(jnp.int32, (CHUNK, LANES), 1) < C_HEAD_DIM
    neg_a = -jnp.exp(alog_ref[...]) * LOG2E
    e = e_ref[...]
    hp = lax.Precision.HIGHEST
    prev = st_ref[...]

    for u in range(rows // CHUNK):
        rs = slice(u * CHUNK, (u + 1) * CHUNK)
        x_cur = xbc_ref[rs, :]
        delayed = jnp.dot(shift, x_cur, preferred_element_type=F32)
        if u == 0:
            tail = tail_ref[8:16, :]
        else:
            tail = xbc_ref[u * CHUNK - 16:u * CHUNK, :].astype(F32)[8:16, :]
        conv = cb_ref[...] + cw_ref[C_CONV - 1:C_CONV, :] * x_cur.astype(F32)
        conv_top = jnp.zeros_like(tail)
        for k in range(C_CONV - 1):
            d = C_CONV - 1 - k
            conv = conv + cw_ref[k:k + 1, :] * delayed[k * CHUNK:(k + 1) * CHUNK]
            conv_top = conv_top + cw_ref[k:k + 1, :] * jnp.where(row8 < d, pltpu.roll(tail, d, 0), 0.0)
        conv = jnp.concatenate([conv[0:8] + conv_top, conv[8:]], axis=0)
        xbc = conv * jax.nn.sigmoid(conv)
        xs = xbc[:, :c_inner]
        bm = xbc[:, c_inner:c_inner + C_GROUPS * C_STATE]
        cm = xbc[:, c_inner + C_GROUPS * C_STATE:]

        dt = jax.nn.softplus(dt_ref[rs, :] + dtb_ref[...])
        a = dt * neg_a
        a_cs = jnp.dot(lower, a, precision=hp, preferred_element_type=F32)
        a_cs_t = jnp.dot(a.T, upper, precision=hp, preferred_element_type=F32)

        dt_x = _split_dot(dt, e, 2)
        acs_x = _split_dot(a_cs, e, 3)
        last_x = acs_x[CHUNK - 1:CHUNK, :]
        xd = xs * dt_x
        xd_b = xd.astype(BF16)
        xdd_b = (xd * jnp.exp2(last_x - acs_x)).astype(BF16)
        prev_b = prev.astype(BF16)
        eacs = jnp.exp2(acs_x)

        y_parts = []
        st_parts = []
        for g in range(C_GROUPS):
            gs = slice(g * gw, (g + 1) * gw)
            b_g = bm[:, g * C_STATE:(g + 1) * C_STATE]
            c_g = cm[:, g * C_STATE:(g + 1) * C_STATE].astype(BF16)
            cb = lax.dot_general(c_g, b_g.astype(BF16), (((1,), (1,)), ((), ())), preferred_element_type=F32)
            st_parts.append(jnp.dot(b_g.T.astype(BF16), xdd_b[:, gs], preferred_element_type=F32))
            y_off = jnp.dot(c_g, prev_b[:, gs], preferred_element_type=F32) * eacs[:, gs]
            tiles = []
            for pr in range(heads_per_group // 2):
                t_idx = g * (heads_per_group // 2) + pr
                xt = xd_b[:, t_idx * LANES:(t_idx + 1) * LANES]
                res = []
                for half in range(2):
                    hd = 2 * t_idx + half
                    seg = a_cs[:, hd:hd + 1] - a_cs_t[hd:hd + 1, :]
                    decay = jnp.where(tri, jnp.exp2(jnp.minimum(seg, 0.0)), 0.0)
                    res.append(jnp.dot((cb * decay).astype(BF16), xt, preferred_element_type=F32))
                tiles.append(jnp.where(lane_lo, res[0], res[1]))
            y_parts.append(jnp.concatenate(tiles, axis=1) + y_off)

        prev = prev * jnp.exp2(last_x) + jnp.concatenate(st_parts, axis=1)
        y = jnp.concatenate(y_parts, axis=1) + dsk_ref[...] * xs
        zf = z_ref[rs, :].astype(F32)
        gated = y * (zf * jax.nn.sigmoid(zf))
        o_ref[rs, :] = _rmsnorm_f32(gated, ng_ref[...]).astype(BF16)

    st_ref[...] = prev
    tail_ref[...] = xbc_ref[rows - 16:rows, :].astype(F32)


def _mamba(proj, dt_raw, conv_w, conv_b, dt_bias, a_log, d_skip_x, norm_g, expand, *, batch, seq, rows):
    t = proj.shape[0]
    c_inner = norm_g.shape[1]
    conv_ch = conv_w.shape[1]
    nc = seq // rows
    row = lambda b, c: (b * nc + c, 0)
    const = lambda b, c: (0, 0)
    return pl.pallas_call(
        _mamba_kernel,
        out_shape=jax.ShapeDtypeStruct((t, c_inner), BF16),
        grid=(batch, nc),
        in_specs=[
            pl.BlockSpec((rows, conv_ch), lambda b, c: (b * nc + c, OFF_XBC // conv_ch)),
            pl.BlockSpec((rows, c_inner), lambda b, c: (b * nc + c, OFF_Z // c_inner)),
            pl.BlockSpec((rows, LANES), row),
            pl.BlockSpec((C_CONV, conv_ch), const),
            pl.BlockSpec((1, conv_ch), const),
            pl.BlockSpec((1, LANES), const),
            pl.BlockSpec((1, LANES), const),
            pl.BlockSpec((1, c_inner), const),
            pl.BlockSpec((1, c_inner), const),
            pl.BlockSpec((LANES, c_inner), const),
        ],
        out_specs=pl.BlockSpec((rows, c_inner), row),
        scratch_shapes=[pltpu.VMEM((16, conv_ch), F32), pltpu.VMEM((C_STATE, c_inner), F32)],
        compiler_params=_cparams(2),
        name="mamba_ssd",
    )(proj, proj, dt_raw, conv_w, conv_b, dt_bias, a_log, d_skip_x, norm_g, expand)


def _merge_kernel(ya_ref, yb_ref, yc_ref, g0_ref, g1_ref, g2_ref, wa_ref, wb_ref, wc_ref, o_ref):
    pa = jnp.dot(ya_ref[...], wa_ref[...], preferred_element_type=F32)
    pb = jnp.dot(yb_ref[...], wb_ref[...], preferred_element_type=F32)
    pc = jnp.dot(yc_ref[...], wc_ref[...], preferred_element_type=F32)
    o_ref[...] = (g0_ref[...].astype(F32) * pa + g1_ref[...].astype(F32) * pb
                  + g2_ref[...].astype(F32) * pc).astype(BF16)


def _merge(ya, yb, yc, proj, w_pa, w_pb, w_pc, layer, *, bm, bn):
    t = ya.shape[0]
    d = w_pa.shape[2]
    gate_spec = lambda k: pl.BlockSpec((bm, bn), lambda i, j: (i, (OFF_GATE + k * d) // bn + j))
    return pl.pallas_call(
        _merge_kernel,
        out_shape=jax.ShapeDtypeStruct((t, d), BF16),
        grid=(t // bm, d // bn),
        in_specs=[
            pl.BlockSpec((bm, ya.shape[1]), lambda i, j: (i, 0)),
            pl.BlockSpec((bm, yb.shape[1]), lambda i, j: (i, 0)),
            pl.BlockSpec((bm, yc.shape[1]), lambda i, j: (i, 0)),
            gate_spec(0), gate_spec(1), gate_spec(2),
            pl.BlockSpec((None, w_pa.shape[1], bn), lambda i, j: (layer, 0, j)),
            pl.BlockSpec((None, w_pb.shape[1], bn), lambda i, j: (layer, 0, j)),
            pl.BlockSpec((None, w_pc.shape[1], bn), lambda i, j: (layer, 0, j)),
        ],
        out_specs=pl.BlockSpec((bm, bn), lambda i, j: (i, j)),
        compiler_params=_cparams(2),
        name="merge",
    )(ya, yb, yc, proj, proj, proj, w_pa, w_pb, w_pc)


def _outproj_kernel(m_ref, w_ref, h_ref, o_ref):
    o_ref[...] = h_ref[...] + jnp.dot(m_ref[...], w_ref[...], preferred_element_type=F32)


def _outproj(merged, w_out, h, layer, *, bm, bn):
    t, d = h.shape
    return pl.pallas_call(
        _outproj_kernel,
        out_shape=jax.ShapeDtypeStruct((t, d), F32),
        grid=(t // bm, d // bn),
        in_specs=[
            pl.BlockSpec((bm, merged.shape[1]), lambda i, j: (i, 0)),
            pl.BlockSpec((None, merged.shape[1], bn), lambda i, j: (layer, 0, j)),
            pl.BlockSpec((bm, bn), lambda i, j: (i, j)),
        ],
        out_specs=pl.BlockSpec((bm, bn), lambda i, j: (i, j)),
        compiler_params=_cparams(2),
        name="outproj",
    )(merged, w_out, h)


def _final_norm_kernel(x_ref, g_ref, o_ref):
    o_ref[...] = _rmsnorm_f32(x_ref[...], g_ref[...])


def _final_norm(h, g, *, bm):
    t, d = h.shape
    return pl.pallas_call(
        _final_norm_kernel,
        out_shape=jax.ShapeDtypeStruct((t, d), F32),
        grid=(t // bm,),
        in_specs=[pl.BlockSpec((bm, d), lambda i: (i, 0)), pl.BlockSpec((1, d), lambda i: (0, 0))],
        out_specs=pl.BlockSpec((bm, d), lambda i: (i, 0)),
        compiler_params=_cparams(1),
        name="final_norm",
    )(h, g)


def kernel(x, rel_bias, final_norm, ffn1_norm, ffn1_wi, ffn1_wo, mix_norm, w_in, sgu_ln_g, sgu_ln_b, sgu_w, sgu_b, diff_lambda, diff_subln, conv_w, conv_b, dt_bias, a_log, d_skip, ssm_norm, w_pa, w_pb, w_pc, w_out, ffn2_norm, ffn2_wi, ffn2_wo):
    batch, seq, d = x.shape
    depth = ffn1_wi.shape[0]
    t = batch * seq
    heads_c = dt_bias.shape[1]
    a_width = sgu_ln_g.shape[1]
    c_inner = ssm_norm.shape[1]
    conv_ch = conv_w.shape[2]
    d_ff = ffn1_wo.shape[1]

    bm = min(1024, t)
    bf = 512
    bq = 512
    bk = 512
    assert t % bm == 0 and d_ff % bf == 0 and seq % bk == 0 and bk % bq == 0 and bq % LANES == 0
    assert w_in.shape[2] == OFF_GATE + heads_c + 3 * d and conv_ch == OFF_AU

    o_au, o_av, o_q, o_z, o_xbc = 0, a_width, 2 * a_width, 5 * a_width, 5 * a_width + c_inner
    o_dt = o_xbc + conv_ch
    o_g = o_dt + heads_c
    w_main = jnp.concatenate(
        [w_in[:, :, o_xbc:o_dt], w_in[:, :, o_au:o_av], w_in[:, :, o_z:o_xbc], w_in[:, :, o_av:o_q],
         w_in[:, :, o_q:o_z], w_in[:, :, o_g:]], axis=2).astype(BF16)
    w_dt = jnp.pad(w_in[:, :, o_dt:o_g], ((0, 0), (0, 0), (0, LANES - heads_c))).astype(BF16)
    def blocked(wi):
        return jnp.swapaxes(wi.reshape(depth, d, 2 * d_ff // bf, bf), 1, 2).astype(BF16)

    wi1, wo1 = blocked(ffn1_wi), ffn1_wo.astype(BF16)
    wi2, wo2 = blocked(ffn2_wi), ffn2_wo.astype(BF16)
    wpa, wpb, wpc, wout = w_pa.astype(BF16), w_pb.astype(BF16), w_pc.astype(BF16), w_out.astype(BF16)

    bsb = jnp.repeat(jnp.swapaxes(sgu_b, 1, 2), a_width // A_GROUPS, axis=2)
    pad_h = ((0, 0), (0, LANES - heads_c))
    dtb = jnp.pad(dt_bias, pad_h)
    alog = jnp.pad(a_log, pad_h)
    dsk_x = jnp.repeat(d_skip, C_HEAD_DIM, axis=1)
    expand = np.zeros((LANES, c_inner), np.float32)
    expand[np.arange(c_inner) // C_HEAD_DIM, np.arange(c_inner)] = 1.0
    expand = jnp.asarray(expand, BF16)
    bias_tiles = _bias_tiles(rel_bias, _bucket_tiles(bq, bk))
    subln_t = jnp.broadcast_to(diff_subln[:, :, None], diff_subln.shape + (bq,))

    h = x.reshape(t, d)
    for l in range(depth):
        lam_init = 0.8 - 0.6 * math.exp(-0.3 * l)
        h = _ffn(h, ffn1_norm[l][None], wi1, wo1, l, bm=bm)
        proj, dt_raw = _inproj(h, mix_norm[l][None], w_main, w_dt, l, bm=bm, bn=2048)
        ya = _sgu(proj, sgu_ln_g[l][None], sgu_ln_b[l][None], sgu_w[l], bsb[l], rows=2 * CHUNK)
        yb = _attn(proj, bias_tiles, diff_lambda[l], subln_t[l],
                   batch=batch, seq=seq, bq=bq, hpb=2, lam_init=lam_init)
        yc = _mamba(proj, dt_raw, conv_w[l], conv_b[l][None], dtb[l][None], alog[l][None],
                    dsk_x[l][None], ssm_norm[l][None], expand, batch=batch, seq=seq, rows=2 * CHUNK)
        merged = _merge(ya, yb, yc, proj, wpa, wpb, wpc, l, bm=bm, bn=512)
        h = _outproj(merged, wout, h, l, bm=bm, bn=1024)
        h = _ffn(h, ffn2_norm[l][None], wi2, wo2, l, bm=bm)
    return _final_norm(h, final_norm[None], bm=bm).reshape(batch, seq, d)
```

```python
import functools
import math

import numpy as np
import jax
import jax.numpy as jnp
from jax import lax
from jax.experimental import pallas as pl
from jax.experimental.pallas import tpu as pltpu

F32 = jnp.float32
BF16 = jnp.bfloat16

EPS = 1e-6
CHUNK = 128
LANES = 128
BF16_ROWS = 16
A_GROUPS = 8
B_HEAD_DIM = 64
C_HEAD_DIM = 64
C_GROUPS = 4
C_STATE = 128
C_CONV = 4
REL_BUCKETS = 32
REL_MAX_DIST = 128
REL_EXACT = REL_BUCKETS // 2
NEG_BIG = -1e30
LOG2E = math.log2(math.e)
VMEM_LIMIT_V7X = 56 * 1024 * 1024
BIG_VMEM_LIMIT_V7X = 60 * 1024 * 1024

OFF_XBC, OFF_AU, OFF_Z, OFF_AV, OFF_Q, OFF_K, OFF_V, OFF_GATE = 0, 3072, 4096, 6144, 7168, 8192, 9216, 10240


def _cparams(n_axes, vmem_limit=VMEM_LIMIT_V7X):
    return pltpu.CompilerParams(dimension_semantics=("arbitrary",) * n_axes, vmem_limit_bytes=vmem_limit)


def _rmsnorm_f32(x, g):
    ms = jnp.mean(x * x, axis=-1, keepdims=True)
    return (x * lax.rsqrt(ms + EPS)) * g


def _ffn_kernel(x_ref, g_ref, wg_ref, wu_ref, wo_ref, wi_src_ref, wo_src_ref, o_ref, wi_cast_ref, wo_cast_ref,
                xn_ref):
    j = pl.program_id(1)

    @pl.when(j == 0)
    def _():
        x = x_ref[...]
        xn_ref[...] = _rmsnorm_f32(x, g_ref[...]).astype(BF16)
        o_ref[...] = x

    xn = xn_ref[...]
    gate = jnp.dot(xn, wg_ref[...], preferred_element_type=F32)
    up = jnp.dot(xn, wu_ref[...], preferred_element_type=F32)
    act = ((0.5 * gate) * jax.nn.sigmoid(gate) * up).astype(BF16)
    o_ref[...] += jnp.dot(act, wo_ref[...], preferred_element_type=F32)
    wi_cast_ref[...] = wi_src_ref[...].astype(BF16)
    wo_cast_ref[...] = wo_src_ref[...].astype(BF16)


def _ffn(h, g, wi, wo, wi_f32, wo_f32, cast_layer, *, bm, bf):
    t, d = h.shape
    d_ff = wo.shape[0]
    ni, nj = t // bm, d_ff // bf
    wi_tile = (d // ni, 2 * d_ff // nj)
    wo_tile = (d_ff // nj, d // ni)
    assert d % ni == 0 and wi_tile[0] % BF16_ROWS == 0 and wi_tile[1] % LANES == 0 and wo_tile[1] % LANES == 0
    return pl.pallas_call(
        _ffn_kernel,
        out_shape=(jax.ShapeDtypeStruct((t, d), F32), jax.ShapeDtypeStruct(wi.shape, BF16),
                   jax.ShapeDtypeStruct(wo.shape, BF16)),
        grid=(ni, nj),
        in_specs=[
            pl.BlockSpec((bm, d), lambda i, j: (i, 0)),
            pl.BlockSpec((1, d), lambda i, j: (0, 0)),
            pl.BlockSpec((d, bf), lambda i, j: (0, j)),
            pl.BlockSpec((d, bf), lambda i, j: (0, j + nj)),
            pl.BlockSpec((bf, d), lambda i, j: (j, 0)),
            pl.BlockSpec((None,) + wi_tile, lambda i, j: (cast_layer, i, j)),
            pl.BlockSpec((None,) + wo_tile, lambda i, j: (cast_layer, j, i)),
        ],
        out_specs=(pl.BlockSpec((bm, d), lambda i, j: (i, 0)),
                   pl.BlockSpec(wi_tile, lambda i, j: (i, j)),
                   pl.BlockSpec(wo_tile, lambda i, j: (j, i))),
        scratch_shapes=[pltpu.VMEM((bm, d), BF16)],
        compiler_params=_cparams(2, vmem_limit=BIG_VMEM_LIMIT_V7X),
        name="ffn",
    )(h, g, wi, wi, wo, wi_f32, wo_f32)


def _inproj_kernel(x_ref, g_ref, w_ref, wdt_ref, o_ref, dt_ref, xn_ref, *, nj_plain):
    j = pl.program_id(1)

    @pl.when(j == 0)
    def _():
        xn = _rmsnorm_f32(x_ref[...], g_ref[...]).astype(BF16)
        xn_ref[...] = xn
        dt_ref[...] = jnp.dot(xn, wdt_ref[...], preferred_element_type=F32)

    @pl.when(j < nj_plain)
    def _():
        o_ref[...] = jnp.dot(xn_ref[...], w_ref[...], preferred_element_type=F32).astype(BF16)

    @pl.when(j >= nj_plain)
    def _():
        o_ref[...] = jax.nn.sigmoid(jnp.dot(xn_ref[...], w_ref[...], preferred_element_type=F32)).astype(BF16)


def _inproj(h, g, w_main, w_dt, layer, *, bm, bn):
    t, d = h.shape
    n = w_main.shape[2]
    return pl.pallas_call(
        functools.partial(_inproj_kernel, nj_plain=OFF_GATE // bn),
        out_shape=(jax.ShapeDtypeStruct((t, n), BF16), jax.ShapeDtypeStruct((t, LANES), F32)),
        grid=(t // bm, n // bn),
        in_specs=[
            pl.BlockSpec((bm, d), lambda i, j: (i, 0)),
            pl.BlockSpec((1, d), lambda i, j: (0, 0)),
            pl.BlockSpec((None, d, bn), lambda i, j: (layer, 0, j)),
            pl.BlockSpec((None, d, LANES), lambda i, j: (layer, 0, 0)),
        ],
        out_specs=(pl.BlockSpec((bm, bn), lambda i, j: (i, j)),
                   pl.BlockSpec((bm, LANES), lambda i, j: (i, 0))),
        scratch_shapes=[pltpu.VMEM((bm, d), BF16)],
        compiler_params=_cparams(2),
        name="inproj",
    )(h, g, w_main, w_dt)


def _sgu_kernel(u_ref, v_ref, lng_ref, lnb_ref, w_ref, bsb_ref, o_ref):
    rows = u_ref.shape[0]
    u = jax.nn.gelu(u_ref[...].astype(F32))
    v = jax.nn.gelu(v_ref[...].astype(F32))
    mu = jnp.mean(v, axis=-1, keepdims=True)
    var = jnp.mean(jnp.square(v - mu), axis=-1, keepdims=True)
    vn = (((v - mu) * lax.rsqrt(var + EPS)) * lng_ref[...] + lnb_ref[...]).astype(BF16)
    r_i = lax.broadcasted_iota(jnp.int32, (CHUNK, CHUNK), 0)
    c_i = lax.broadcasted_iota(jnp.int32, (CHUNK, CHUNK), 1)
    tri = r_i >= c_i
    for g in range(A_GROUPS):
        wm = jnp.where(tri, w_ref[g], 0.0).astype(BF16)
        cs = slice(g * LANES, (g + 1) * LANES)
        for c in range(rows // CHUNK):
            rs = slice(c * CHUNK, (c + 1) * CHUNK)
            mixed = jnp.dot(wm, vn[rs, cs], preferred_element_type=F32) + bsb_ref[:, cs]
            o_ref[rs, cs] = (u[rs, cs] * mixed).astype(BF16)


def _sgu(proj, ln_g, ln_b, w_s, bsb, *, rows):
    t = proj.shape[0]
    aw = ln_g.shape[1]
    return pl.pallas_call(
        _sgu_kernel,
        out_shape=jax.ShapeDtypeStruct((t, aw), BF16),
        grid=(t // rows,),
        in_specs=[
            pl.BlockSpec((rows, aw), lambda i: (i, OFF_AU // aw)),
            pl.BlockSpec((rows, aw), lambda i: (i, OFF_AV // aw)),
            pl.BlockSpec((1, aw), lambda i: (0, 0)),
            pl.BlockSpec((1, aw), lambda i: (0, 0)),
            pl.BlockSpec((A_GROUPS, CHUNK, CHUNK), lambda i: (0, 0, 0)),
            pl.BlockSpec((CHUNK, aw), lambda i: (0, 0)),
        ],
        out_specs=pl.BlockSpec((rows, aw), lambda i: (i, 0)),
        compiler_params=_cparams(1),
        name="sgu",
    )(proj, proj, ln_g, ln_b, w_s, bsb)


def _t5_bucket_np(dist):
    n = np.maximum(dist, 0)
    nf = np.maximum(n, 1).astype(np.float64)
    large = REL_EXACT + (np.log(nf / REL_EXACT) / math.log(REL_MAX_DIST / REL_EXACT)
                         * (REL_BUCKETS - REL_EXACT)).astype(np.int32)
    large = np.minimum(large, REL_BUCKETS - 1)
    return np.where(n < REL_EXACT, n, large).astype(np.int32)


def _bucket_tiles(bq, bk):
    last_bucket_from = int(np.argmax(_t5_bucket_np(np.arange(4 * REL_MAX_DIST)) == REL_BUCKETS - 1))
    n_tiles = -(-(bk - 1 + last_bucket_from) // bq)
    c = np.arange(bk)[:, None]
    r = np.arange(bq)[None, :]
    tiles = []
    for t in range(n_tiles):
        dist = t * bq + r - c
        tiles.append(np.where(dist >= 0, _t5_bucket_np(dist), REL_BUCKETS))
    assert _t5_bucket_np(np.array([n_tiles * bq - (bk - 1)]))[0] == REL_BUCKETS - 1
    return np.stack(tiles).astype(np.int32)


BIAS_ROWS = 64


def _bias_tiles_kernel(rb_ref, bkt_ref, o_ref, *, present):
    hd = pl.program_id(0)
    n_bias, bk, bq = o_ref.shape
    for t in range(n_bias - 1):
        for rc in range(bk // BIAS_ROWS):
            rs = slice(rc * BIAS_ROWS, (rc + 1) * BIAS_ROWS)
            bt = bkt_ref[t, rs, :]
            tile = jnp.where(bt == REL_BUCKETS, NEG_BIG, 0.0).astype(F32)
            for b in present[t][rc]:
                tile = jnp.where(bt == b, rb_ref[b, hd] * LOG2E, tile)
            o_ref[t, rs, :] = tile
    o_ref[n_bias - 1] = jnp.full((bk, bq), rb_ref[REL_BUCKETS - 1, hd] * LOG2E, F32)


def _bias_tiles(rel_bias, buckets_np):
    n_tiles, bk, bq = buckets_np.shape
    heads = rel_bias.shape[1]
    present = tuple(tuple(tuple(int(b) for b in np.unique(buckets_np[t, rc * BIAS_ROWS:(rc + 1) * BIAS_ROWS])
                                if b < REL_BUCKETS)
                          for rc in range(bk // BIAS_ROWS)) for t in range(n_tiles))
    return pl.pallas_call(
        functools.partial(_bias_tiles_kernel, present=present),
        out_shape=jax.ShapeDtypeStruct((heads, n_tiles + 1, bk, bq), F32),
        grid=(heads,),
        in_specs=[pl.BlockSpec(memory_space=pltpu.SMEM),
                  pl.BlockSpec((n_tiles, bk, bq), lambda h: (0, 0, 0))],
        out_specs=pl.BlockSpec((None, n_tiles + 1, bk, bq), lambda h: (h, 0, 0, 0)),
        compiler_params=_cparams(1),
        name="bias_tiles",
    )(rel_bias, jnp.asarray(buckets_np))


def _attn_kernel(lam_ref, sub_ref, bias_ref, q_ref, k_ref, v_ref, o_ref,
                 vt_ref, qqt_ref, s_ref, p_ref, cm_ref, m_ref, al_ref, acc_ref, *, lam_init):
    hw = 2 * B_HEAD_DIM
    hpb, bk = s_ref.shape[0], s_ref.shape[1]
    bq = s_ref.shape[2] // 2
    seq = k_ref.shape[0]
    nq = seq // bq
    n_bias = bias_ref.shape[1]
    cols = [slice(hh * hw, (hh + 1) * hw) for hh in range(hpb)]
    last_block = lambda i: (i * bq) // bk
    n_pairs = sum(last_block(i) + 1 for i in range(nq))

    pad_row = lax.broadcasted_iota(jnp.int32, (vt_ref.shape[2] - hw, bk), 0)
    ones_row = jnp.where(pad_row == 0, 1.0, 0.0).astype(BF16)
    dim = lax.broadcasted_iota(jnp.int32, (hw, bq), 0)
    for hh in range(hpb):
        for c in range(seq // LANES):
            blk, off = divmod(c * LANES, bk)
            vt_ref[hh, blk, 0:hw, off:off + LANES] = (
                v_ref[c * LANES:(c + 1) * LANES, cols[hh]].astype(F32).T.astype(BF16))
        for blk in range(seq // bk):
            vt_ref[hh, blk, hw:, :] = ones_row
        for i in range(nq):
            qt = (q_ref[i * bq:(i + 1) * bq, cols[hh]].astype(F32) * (B_HEAD_DIM ** -0.5 * LOG2E)).T
            qqt_ref[hh, i] = jnp.concatenate([jnp.where(dim < B_HEAD_DIM, qt, 0.0),
                                              jnp.where(dim >= B_HEAD_DIM, qt, 0.0)], axis=1).astype(BF16)
    m_ref[...] = jnp.full(m_ref.shape, NEG_BIG, F32)
    al_ref[...] = jnp.ones(al_ref.shape, F32)
    acc_ref[...] = jnp.zeros(acc_ref.shape, F32)
    p_ref[...] = jnp.zeros(p_ref.shape, BF16)

    def scores(hh, i, j):
        kb = k_ref[pl.ds(pl.multiple_of(j * bk, bk), bk), cols[hh]]
        s = jnp.dot(kb, qqt_ref[hh, i], preferred_element_type=F32)
        bias = bias_ref[hh, jnp.minimum((i * bq - j * bk) // bq, n_bias - 1)]
        s = s + jnp.concatenate([bias, bias], axis=1)
        s_ref[hh] = s
        cm_ref[hh] = jnp.max(s, axis=0, keepdims=True)

    def probabilities(hh, i):
        m_prev = m_ref[hh, i]
        m_next = jnp.maximum(m_prev, cm_ref[hh])
        m_ref[hh, i] = m_next
        al_ref[hh] = jnp.exp2(m_prev - m_next)
        p_ref[hh] = jnp.exp2(s_ref[hh] - m_next).astype(BF16)

    def weighted_values(hh, i, j):
        acc_ref[hh, i] = al_ref[hh] * acc_ref[hh, i] + jnp.dot(vt_ref[hh, j], p_ref[hh],
                                                                preferred_element_type=F32)

    for hh in range(hpb):
        scores(hh, 0, 0)

    def body(t, carry):
        ip, jp, ic, jc = carry
        wrap = jc >= last_block(ic)
        i_next = jnp.where(wrap, ic + 1, ic)
        j_next = jnp.where(wrap, 0, jc + 1)
        for hh in range(hpb):
            weighted_values(hh, ip, jp)
        for hh in range(hpb):
            probabilities(hh, ic)
        for hh in range(hpb):
            scores(hh, i_next, j_next)
        return ic, jc, i_next, j_next

    zero = jnp.int32(0)
    ip, jp, ic, jc = lax.fori_loop(0, n_pairs - 1, body, (zero, zero, zero, zero))
    for hh in range(hpb):
        weighted_values(hh, ip, jp)
        probabilities(hh, ic)
    for hh in range(hpb):
        weighted_values(hh, ic, jc)

    lam_p = lam_ref[...]
    lam = (jnp.exp(jnp.sum(lam_p[0:1] * lam_p[1:2], axis=1, keepdims=True))
           - jnp.exp(jnp.sum(lam_p[2:3] * lam_p[3:4], axis=1, keepdims=True)) + lam_init)
    for hh in range(hpb):
        for i in range(nq):
            o = acc_ref[hh, i, 0:hw, :] * (1.0 / acc_ref[hh, i, hw:hw + 1, :])
            attn = o[:, :bq] - lam * o[:, bq:]
            ms = jnp.mean(attn * attn, axis=0, keepdims=True)
            y = ((attn * lax.rsqrt(ms + EPS)) * sub_ref[...]) * (1.0 - lam_init)
            o_ref[i * bq:(i + 1) * bq, cols[hh]] = y.T.astype(BF16)


def _attn(proj, bias_tiles, lam_p, subln_t, *, batch, seq, bq, hpb, lam_init):
    t = proj.shape[0]
    hw = 2 * B_HEAD_DIM
    heads, n_bias, bk, _ = bias_tiles.shape
    nq = seq // bq
    bw = hpb * hw
    return pl.pallas_call(
        functools.partial(_attn_kernel, lam_init=lam_init),
        out_shape=jax.ShapeDtypeStruct((t, heads * hw), BF16),
        grid=(batch, heads // hpb),
        in_specs=[
            pl.BlockSpec((4, B_HEAD_DIM), lambda b, h: (0, 0)),
            pl.BlockSpec((hw, bq), lambda b, h: (0, 0)),
            pl.BlockSpec((hpb, n_bias, bk, bq), lambda b, h: (h, 0, 0, 0)),
            pl.BlockSpec((seq, bw), lambda b, h: (b, OFF_Q // bw + h)),
            pl.BlockSpec((seq, bw), lambda b, h: (b, OFF_K // bw + h)),
            pl.BlockSpec((seq, bw), lambda b, h: (b, OFF_V // bw + h)),
        ],
        out_specs=pl.BlockSpec((seq, bw), lambda b, h: (b, h)),
        scratch_shapes=[pltpu.VMEM((hpb, seq // bk, hw + BF16_ROWS, bk), BF16),
                        pltpu.VMEM((hpb, nq, hw, 2 * bq), BF16),
                        pltpu.VMEM((hpb, bk, 2 * bq), F32),
                        pltpu.VMEM((hpb, bk, 2 * bq), BF16),
                        pltpu.VMEM((hpb, 1, 2 * bq), F32),
                        pltpu.VMEM((hpb, nq, 1, 2 * bq), F32),
                        pltpu.VMEM((hpb, 1, 2 * bq), F32),
                        pltpu.VMEM((hpb, nq, hw + BF16_ROWS, 2 * bq), F32)],
        compiler_params=_cparams(2, vmem_limit=BIG_VMEM_LIMIT_V7X),
        name="diff_attn",
    )(lam_p, subln_t, bias_tiles, proj, proj, proj)


def _split_dot(x, e_bf16, passes):
    out = None
    r = x
    for _ in range(passes):
        hi = r.astype(BF16)
        part = jnp.dot(hi, e_bf16, preferred_element_type=F32)
        out = part if out is None else out + part
        r = r - hi.astype(F32)
    return out


def _mamba_kernel(xbc_ref, z_ref, dt_ref, cw_ref, cb_ref, dtb_ref, alog_ref, dsk_ref, ng_ref, e_ref,
                  o_ref, tail_ref, st_ref):
    rows, c_inner = z_ref.shape
    gw = c_inner // C_GROUPS
    heads_per_group = gw // C_HEAD_DIM
    c = pl.program_id(1)

    @pl.when(c == 0)
    def _():
        tail_ref[...] = jnp.zeros_like(tail_ref)
        st_ref[...] = jnp.zeros_like(st_ref)

    r_i = lax.broadcasted_iota(jnp.int32, (CHUNK, CHUNK), 0)
    c_i = lax.broadcasted_iota(jnp.int32, (CHUNK, CHUNK), 1)
    tri = r_i >= c_i
    lower = tri.astype(F32)
    upper = (r_i <= c_i).astype(F32)
    shift = jnp.concatenate([(r_i - c_i == d).astype(BF16) for d in range(C_CONV - 1, 0, -1)], axis=0)
    row8 = lax.broadcasted_iota(jnp.int32, (8, xbc_ref.shape[1]), 0)
    lane_lo = lax.broadcasted_iota(jnp.int32, (CHUNK, LANES), 1) < C_HEAD_DIM
    neg_a = -jnp.exp(alog_ref[...]) * LOG2E
    e = e_ref[...]
    hp = lax.Precision.HIGHEST
    prev = st_ref[...]

    for u in range(rows // CHUNK):
        rs = slice(u * CHUNK, (u + 1) * CHUNK)
        x_cur = xbc_ref[rs, :]
        delayed = jnp.dot(shift, x_cur, preferred_element_type=F32)
        if u == 0:
            tail = tail_ref[8:16, :]
        else:
            tail = xbc_ref[u * CHUNK - 16:u * CHUNK, :].astype(F32)[8:16, :]
        conv = cb_ref[...] + cw_ref[C_CONV - 1:C_CONV, :] * x_cur.astype(F32)
        conv_top = jnp.zeros_like(tail)
        for k in range(C_CONV - 1):
            d = C_CONV - 1 - k
            conv = conv + cw_ref[k:k + 1, :] * delayed[k * CHUNK:(k + 1) * CHUNK]
            conv_top = conv_top + cw_ref[k:k + 1, :] * jnp.where(row8 < d, pltpu.roll(tail, d, 0), 0.0)
        conv = jnp.concatenate([conv[0:8] + conv_top, conv[8:]], axis=0)
        xbc = conv * jax.nn.sigmoid(conv)
        xs = xbc[:, :c_inner]
        bm = xbc[:, c_inner:c_inner + C_GROUPS * C_STATE]
        cm = xbc[:, c_inner + C_GROUPS * C_STATE:]

        dt = jax.nn.softplus(dt_ref[rs, :] + dtb_ref[...])
        a = dt * neg_a
        a_cs = jnp.dot(lower, a, precision=hp, preferred_element_type=F32)
        a_cs_t = jnp.dot(a.T, upper, precision=hp, preferred_element_type=F32)

        dt_x = _split_dot(dt, e, 2)
        acs_x = _split_dot(a_cs, e, 3)
        last_x = acs_x[CHUNK - 1:CHUNK, :]
        xd = xs * dt_x
        xd_b = xd.astype(BF16)
        xdd_b = (xd * jnp.exp2(last_x - acs_x)).astype(BF16)
        prev_b = prev.astype(BF16)
        eacs = jnp.exp2(acs_x)

        y_parts = []
        st_parts = []
        for g in range(C_GROUPS):
            gs = slice(g * gw, (g + 1) * gw)
            b_g = bm[:, g * C_STATE:(g + 1) * C_STATE]
            c_g = cm[:, g * C_STATE:(g + 1) * C_STATE].astype(BF16)
            cb = lax.dot_general(c_g, b_g.astype(BF16), (((1,), (1,)), ((), ())), preferred_element_type=F32)
            st_parts.append(jnp.dot(b_g.T.astype(BF16), xdd_b[:, gs], preferred_element_type=F32))
            y_off = jnp.dot(c_g, prev_b[:, gs], preferred_element_type=F32) * eacs[:, gs]
            tiles = []
            for pr in range(heads_per_group // 2):
                t_idx = g * (heads_per_group // 2) + pr
                xt = xd_b[:, t_idx * LANES:(t_idx + 1) * LANES]
                res = []
                for half in range(2):
                    hd = 2 * t_idx + half
                    seg = a_cs[:, hd:hd + 1] - a_cs_t[hd:hd + 1, :]
                    decay = jnp.where(tri, jnp.exp2(jnp.minimum(seg, 0.0)), 0.0)
                    res.append(jnp.dot((cb * decay).astype(BF16), xt, preferred_element_type=F32))
                tiles.append(jnp.where(lane_lo, res[0], res[1]))
            y_parts.append(jnp.concatenate(tiles, axis=1) + y_off)

        prev = prev * jnp.exp2(last_x) + jnp.concatenate(st_parts, axis=1)
        y = jnp.concatenate(y_parts, axis=1) + dsk_ref[...] * xs
        zf = z_ref[rs, :].astype(F32)
        gated = y * (zf * jax.nn.sigmoid(zf))
        o_ref[rs, :] = _rmsnorm_f32(gated, ng_ref[...]).astype(BF16)

    st_ref[...] = prev
    tail_ref[...] = xbc_ref[rows - 16:rows, :].astype(F32)


def _mamba(proj, dt_raw, conv_w, conv_b, dt_bias, a_log, d_skip_x, norm_g, expand, *, batch, seq, rows):
    t = proj.shape[0]
    c_inner = norm_g.shape[1]
    conv_ch = conv_w.shape[1]
    nc = seq // rows
    row = lambda b, c: (b * nc + c, 0)
    const = lambda b, c: (0, 0)
    return pl.pallas_call(
        _mamba_kernel,
        out_shape=jax.ShapeDtypeStruct((t, c_inner), BF16),
        grid=(batch, nc),
        in_specs=[
            pl.BlockSpec((rows, conv_ch), lambda b, c: (b * nc + c, OFF_XBC // conv_ch)),
            pl.BlockSpec((rows, c_inner), lambda b, c: (b * nc + c, OFF_Z // c_inner)),
            pl.BlockSpec((rows, LANES), row),
            pl.BlockSpec((C_CONV, conv_ch), const),
            pl.BlockSpec((1, conv_ch), const),
            pl.BlockSpec((1, LANES), const),
            pl.BlockSpec((1, LANES), const),
            pl.BlockSpec((1, c_inner), const),
            pl.BlockSpec((1, c_inner), const),
            pl.BlockSpec((LANES, c_inner), const),
        ],
        out_specs=pl.BlockSpec((rows, c_inner), row),
        scratch_shapes=[pltpu.VMEM((16, conv_ch), F32), pltpu.VMEM((C_STATE, c_inner), F32)],
        compiler_params=_cparams(2),
        name="mamba_ssd",
    )(proj, proj, dt_raw, conv_w, conv_b, dt_bias, a_log, d_skip_x, norm_g, expand)


def _merge_kernel(ya_ref, yb_ref, yc_ref, g0_ref, g1_ref, g2_ref, wa_ref, wb_ref, wc_ref, o_ref):
    pa = jnp.dot(ya_ref[...], wa_ref[...], preferred_element_type=F32)
    pb = jnp.dot(yb_ref[...], wb_ref[...], preferred_element_type=F32)
    pc = jnp.dot(yc_ref[...], wc_ref[...], preferred_element_type=F32)
    o_ref[...] = (g0_ref[...].astype(F32) * pa + g1_ref[...].astype(F32) * pb
                  + g2_ref[...].astype(F32) * pc).astype(BF16)


def _merge(ya, yb, yc, proj, w_pa, w_pb, w_pc, layer, *, bm, bn):
    t = ya.shape[0]
    d = w_pa.shape[2]
    gate_spec = lambda k: pl.BlockSpec((bm, bn), lambda i, j: (i, (OFF_GATE + k * d) // bn + j))
    resident = pl.Buffered(1) if bn == d else None
    return pl.pallas_call(
        _merge_kernel,
        out_shape=jax.ShapeDtypeStruct((t, d), BF16),
        grid=(t // bm, d // bn),
        in_specs=[
            pl.BlockSpec((bm, ya.shape[1]), lambda i, j: (i, 0)),
            pl.BlockSpec((bm, yb.shape[1]), lambda i, j: (i, 0)),
            pl.BlockSpec((bm, yc.shape[1]), lambda i, j: (i, 0)),
            gate_spec(0), gate_spec(1), gate_spec(2),
            pl.BlockSpec((None, w_pa.shape[1], bn), lambda i, j: (layer, 0, j), pipeline_mode=resident),
            pl.BlockSpec((None, w_pb.shape[1], bn), lambda i, j: (layer, 0, j), pipeline_mode=resident),
            pl.BlockSpec((None, w_pc.shape[1], bn), lambda i, j: (layer, 0, j), pipeline_mode=resident),
        ],
        out_specs=pl.BlockSpec((bm, bn), lambda i, j: (i, j)),
        compiler_params=_cparams(2),
        name="merge",
    )(ya, yb, yc, proj, proj, proj, w_pa, w_pb, w_pc)


def _outproj_kernel(m_ref, w_ref, h_ref, o_ref):
    o_ref[...] = h_ref[...] + jnp.dot(m_ref[...], w_ref[...], preferred_element_type=F32)


def _outproj(merged, w_out, h, layer, *, bm, bn):
    t, d = h.shape
    return pl.pallas_call(
        _outproj_kernel,
        out_shape=jax.ShapeDtypeStruct((t, d), F32),
        grid=(t // bm, d // bn),
        in_specs=[
            pl.BlockSpec((bm, merged.shape[1]), lambda i, j: (i, 0)),
            pl.BlockSpec((None, merged.shape[1], bn), lambda i, j: (layer, 0, j),
                         pipeline_mode=pl.Buffered(1) if bn == d else None),
            pl.BlockSpec((bm, bn), lambda i, j: (i, j)),
        ],
        out_specs=pl.BlockSpec((bm, bn), lambda i, j: (i, j)),
        compiler_params=_cparams(2),
        name="outproj",
    )(merged, w_out, h)


def _final_norm_kernel(x_ref, g_ref, o_ref):
    o_ref[...] = _rmsnorm_f32(x_ref[...], g_ref[...])


def _final_norm(h, g, *, bm):
    t, d = h.shape
    return pl.pallas_call(
        _final_norm_kernel,
        out_shape=jax.ShapeDtypeStruct((t, d), F32),
        grid=(t // bm,),
        in_specs=[pl.BlockSpec((bm, d), lambda i: (i, 0)), pl.BlockSpec((1, d), lambda i: (0, 0))],
        out_specs=pl.BlockSpec((bm, d), lambda i: (i, 0)),
        compiler_params=_cparams(1),
        name="final_norm",
    )(h, g)


def kernel(x, rel_bias, final_norm, ffn1_norm, ffn1_wi, ffn1_wo, mix_norm, w_in, sgu_ln_g, sgu_ln_b, sgu_w, sgu_b, diff_lambda, diff_subln, conv_w, conv_b, dt_bias, a_log, d_skip, ssm_norm, w_pa, w_pb, w_pc, w_out, ffn2_norm, ffn2_wi, ffn2_wo):
    batch, seq, d = x.shape
    depth = ffn1_wi.shape[0]
    t = batch * seq
    heads_c = dt_bias.shape[1]
    a_width = sgu_ln_g.shape[1]
    c_inner = ssm_norm.shape[1]
    conv_ch = conv_w.shape[2]
    d_ff = ffn1_wo.shape[1]

    bm = min(1024, t)
    bf = 512
    bq = 512
    bk = 512
    assert t % bm == 0 and d_ff % bf == 0 and seq % bk == 0 and bk % bq == 0 and bq % LANES == 0
    assert w_in.shape[2] == OFF_GATE + heads_c + 3 * d and conv_ch == OFF_AU

    o_au, o_av, o_q, o_z, o_xbc = 0, a_width, 2 * a_width, 5 * a_width, 5 * a_width + c_inner
    o_dt = o_xbc + conv_ch
    o_g = o_dt + heads_c
    w_main = jnp.concatenate(
        [w_in[:, :, o_xbc:o_dt], w_in[:, :, o_au:o_av], w_in[:, :, o_z:o_xbc], w_in[:, :, o_av:o_q],
         w_in[:, :, o_q:o_z], w_in[:, :, o_g:]], axis=2).astype(BF16)
    w_dt = jnp.pad(w_in[:, :, o_dt:o_g], ((0, 0), (0, 0), (0, LANES - heads_c))).astype(BF16)
    wi1, wo1 = ffn1_wi[0].astype(BF16), ffn1_wo[0].astype(BF16)
    wi2, wo2 = ffn2_wi[0].astype(BF16), ffn2_wo[0].astype(BF16)
    wpa, wpb, wpc, wout = w_pa.astype(BF16), w_pb.astype(BF16), w_pc.astype(BF16), w_out.astype(BF16)

    bsb = jnp.repeat(jnp.swapaxes(sgu_b, 1, 2), a_width // A_GROUPS, axis=2)
    pad_h = ((0, 0), (0, LANES - heads_c))
    dtb = jnp.pad(dt_bias, pad_h)
    alog = jnp.pad(a_log, pad_h)
    dsk_x = jnp.repeat(d_skip, C_HEAD_DIM, axis=1)
    expand = np.zeros((LANES, c_inner), np.float32)
    expand[np.arange(c_inner) // C_HEAD_DIM, np.arange(c_inner)] = 1.0
    expand = jnp.asarray(expand, BF16)
    bias_tiles = _bias_tiles(rel_bias, _bucket_tiles(bq, bk))
    subln_t = jnp.broadcast_to(diff_subln[:, :, None], diff_subln.shape + (bq,))

    h = x.reshape(t, d)
    for l in range(depth):
        lam_init = 0.8 - 0.6 * math.exp(-0.3 * l)
        nxt = min(l + 1, depth - 1)
        h, wi1, wo1 = _ffn(h, ffn1_norm[l][None], wi1, wo1, ffn1_wi, ffn1_wo, nxt, bm=bm, bf=bf)
        proj, dt_raw = _inproj(h, mix_norm[l][None], w_main, w_dt, l, bm=bm, bn=2048)
        ya = _sgu(proj, sgu_ln_g[l][None], sgu_ln_b[l][None], sgu_w[l], bsb[l], rows=2 * CHUNK)
        yb = _attn(proj, bias_tiles, diff_lambda[l], subln_t[l],
                   batch=batch, seq=seq, bq=bq, hpb=2, lam_init=lam_init)
        yc = _mamba(proj, dt_raw, conv_w[l], conv_b[l][None], dtb[l][None], alog[l][None],
                    dsk_x[l][None], ssm_norm[l][None], expand, batch=batch, seq=seq, rows=2 * CHUNK)
        merged = _merge(ya, yb, yc, proj, wpa, wpb, wpc, l, bm=bm // 2, bn=d)
        h = _outproj(merged, wout, h, l, bm=bm, bn=d)
        h, wi2, wo2 = _ffn(h, ffn2_norm[l][None], wi2, wo2, ffn2_wi, ffn2_wo, nxt, bm=bm, bf=bf)
    return _final_norm(h, final_norm[None], bm=bm).reshape(batch, seq, d)
```

```python
import functools
import math

import numpy as np
import jax
import jax.numpy as jnp
from jax import lax
from jax.experimental import pallas as pl
from jax.experimental.pallas import tpu as pltpu

F32 = jnp.float32
BF16 = jnp.bfloat16

EPS = 1e-6
CHUNK = 128
LANES = 128
BF16_ROWS = 16
A_GROUPS = 8
B_HEAD_DIM = 64
C_HEAD_DIM = 64
C_GROUPS = 4
C_STATE = 128
C_CONV = 4
REL_BUCKETS = 32
REL_MAX_DIST = 128
REL_EXACT = REL_BUCKETS // 2
NEG_BIG = -1e30
LOG2E = math.log2(math.e)
VMEM_LIMIT_V7X = 56 * 1024 * 1024
BIG_VMEM_LIMIT_V7X = 60 * 1024 * 1024

OFF_XBC, OFF_AU, OFF_Z, OFF_AV, OFF_Q, OFF_K, OFF_V, OFF_GATE = 0, 3072, 4096, 6144, 7168, 8192, 9216, 10240


def _cparams(n_axes, vmem_limit=VMEM_LIMIT_V7X):
    return pltpu.CompilerParams(dimension_semantics=("arbitrary",) * n_axes, vmem_limit_bytes=vmem_limit)


def _rmsnorm_f32(x, g):
    ms = jnp.mean(x * x, axis=-1, keepdims=True)
    return (x * lax.rsqrt(ms + EPS)) * g


def _ffn_kernel(x_ref, g_ref, wg_ref, wu_ref, wo_ref, wi_src_ref, wo_src_ref, o_ref, wi_cast_ref, wo_cast_ref,
                xn_ref):
    j = pl.program_id(1)

    @pl.when(j == 0)
    def _():
        x = x_ref[...]
        xn_ref[...] = _rmsnorm_f32(x, g_ref[...]).astype(BF16)
        o_ref[...] = x

    xn = xn_ref[...]
    gate = jnp.dot(xn, wg_ref[...], preferred_element_type=F32)
    up = jnp.dot(xn, wu_ref[...], preferred_element_type=F32)
    act = ((0.5 * gate) * jax.nn.sigmoid(gate) * up).astype(BF16)
    o_ref[...] += jnp.dot(act, wo_ref[...], preferred_element_type=F32)
    wi_cast_ref[...] = wi_src_ref[...].astype(BF16)
    wo_cast_ref[...] = wo_src_ref[...].astype(BF16)


def _ffn(h, g, wi, wo, wi_f32, wo_f32, cast_layer, *, bm, bf):
    t, d = h.shape
    d_ff = wo.shape[0]
    ni, nj = t // bm, d_ff // bf
    wi_tile = (d // ni, 2 * d_ff // nj)
    wo_tile = (d_ff // nj, d // ni)
    assert d % ni == 0 and wi_tile[0] % BF16_ROWS == 0 and wi_tile[1] % LANES == 0 and wo_tile[1] % LANES == 0
    return pl.pallas_call(
        _ffn_kernel,
        out_shape=(jax.ShapeDtypeStruct((t, d), F32), jax.ShapeDtypeStruct(wi.shape, BF16),
                   jax.ShapeDtypeStruct(wo.shape, BF16)),
        grid=(ni, nj),
        in_specs=[
            pl.BlockSpec((bm, d), lambda i, j: (i, 0)),
            pl.BlockSpec((1, d), lambda i, j: (0, 0)),
            pl.BlockSpec((d, bf), lambda i, j: (0, j)),
            pl.BlockSpec((d, bf), lambda i, j: (0, j + nj)),
            pl.BlockSpec((bf, d), lambda i, j: (j, 0)),
            pl.BlockSpec((None,) + wi_tile, lambda i, j: (cast_layer, i, j)),
            pl.BlockSpec((None,) + wo_tile, lambda i, j: (cast_layer, j, i)),
        ],
        out_specs=(pl.BlockSpec((bm, d), lambda i, j: (i, 0)),
                   pl.BlockSpec(wi_tile, lambda i, j: (i, j)),
                   pl.BlockSpec(wo_tile, lambda i, j: (j, i))),
        scratch_shapes=[pltpu.VMEM((bm, d), BF16)],
        compiler_params=_cparams(2, vmem_limit=BIG_VMEM_LIMIT_V7X),
        name="ffn",
    )(h, g, wi, wi, wo, wi_f32, wo_f32)


REORDER_TILE = 1024
MAIN_SRC_TILES = tuple(o // REORDER_TILE + k
                       for o, n in ((7168, 3), (0, 1), (5120, 2), (1024, 1), (2048, 3)) for k in range(n))


def _reorder_tile(i, j, nj, n_tiles):
    step = i * nj + j
    return step // n_tiles, step % n_tiles


def _inproj_kernel(x_ref, g_ref, w_ref, wdt_ref, main_src_ref, gate_src_ref, o_ref, dt_ref, w_cast_ref, xn_ref,
                   *, nj_plain, n_tiles):
    j = pl.program_id(1)
    _, tile = _reorder_tile(pl.program_id(0), j, pl.num_programs(1), n_tiles)
    w_cast_ref[...] = jnp.where(tile < len(MAIN_SRC_TILES), main_src_ref[...].astype(BF16), gate_src_ref[...])


    @pl.when(j == 0)
    def _():
        xn = _rmsnorm_f32(x_ref[...], g_ref[...]).astype(BF16)
        xn_ref[...] = xn
        dt_ref[...] = jnp.dot(xn, wdt_ref[...], preferred_element_type=F32)

    @pl.when(j < nj_plain)
    def _():
        o_ref[...] = jnp.dot(xn_ref[...], w_ref[...], preferred_element_type=F32).astype(BF16)

    @pl.when(j >= nj_plain)
    def _():
        o_ref[...] = jax.nn.sigmoid(jnp.dot(xn_ref[...], w_ref[...], preferred_element_type=F32)).astype(BF16)


def _inproj(h, g, w_main, w_dt, w_in_f32, w_gate, layer, cast_layer, *, bm, bn):
    t, d = h.shape
    n = w_main.shape[1]
    ni, nj = t // bm, n // bn
    n_tiles = n // REORDER_TILE
    n_main = len(MAIN_SRC_TILES)
    assert (ni * nj) % n_tiles == 0 and d % ((ni * nj) // n_tiles) == 0
    rows = d // ((ni * nj) // n_tiles)

    def main_src_map(i, j):
        rb, tile = _reorder_tile(i, j, nj, n_tiles)
        src = jnp.int32(MAIN_SRC_TILES[-1])
        for k in range(n_main - 1):
            src = jnp.where(tile == k, MAIN_SRC_TILES[k], src)
        return cast_layer, rb, src

    def gate_src_map(i, j):
        rb, tile = _reorder_tile(i, j, nj, n_tiles)
        return max(cast_layer - 1, 0), rb, jnp.maximum(tile - n_main, 0)

    return pl.pallas_call(
        functools.partial(_inproj_kernel, nj_plain=OFF_GATE // bn, n_tiles=n_tiles),
        out_shape=(jax.ShapeDtypeStruct((t, n), BF16), jax.ShapeDtypeStruct((t, LANES), F32),
                   jax.ShapeDtypeStruct((d, n), BF16)),
        grid=(ni, nj),
        in_specs=[
            pl.BlockSpec((bm, d), lambda i, j: (i, 0)),
            pl.BlockSpec((1, d), lambda i, j: (0, 0)),
            pl.BlockSpec((d, bn), lambda i, j: (0, j)),
            pl.BlockSpec((None, d, LANES), lambda i, j: (layer, 0, 0)),
            pl.BlockSpec((None, rows, REORDER_TILE), main_src_map),
            pl.BlockSpec((None, rows, REORDER_TILE), gate_src_map),
        ],
        out_specs=(pl.BlockSpec((bm, bn), lambda i, j: (i, j)),
                   pl.BlockSpec((bm, LANES), lambda i, j: (i, 0)),
                   pl.BlockSpec((rows, REORDER_TILE), lambda i, j: _reorder_tile(i, j, nj, n_tiles))),
        scratch_shapes=[pltpu.VMEM((bm, d), BF16)],
        compiler_params=_cparams(2, vmem_limit=BIG_VMEM_LIMIT_V7X),
        name="inproj",
    )(h, g, w_main, w_dt, w_in_f32, w_gate)


def _sgu_kernel(u_ref, v_ref, lng_ref, lnb_ref, w_ref, bsb_ref, o_ref):
    rows = u_ref.shape[0]
    u = jax.nn.gelu(u_ref[...].astype(F32))
    v = jax.nn.gelu(v_ref[...].astype(F32))
    mu = jnp.mean(v, axis=-1, keepdims=True)
    var = jnp.mean(jnp.square(v - mu), axis=-1, keepdims=True)
    vn = (((v - mu) * lax.rsqrt(var + EPS)) * lng_ref[...] + lnb_ref[...]).astype(BF16)
    r_i = lax.broadcasted_iota(jnp.int32, (CHUNK, CHUNK), 0)
    c_i = lax.broadcasted_iota(jnp.int32, (CHUNK, CHUNK), 1)
    tri = r_i >= c_i
    for g in range(A_GROUPS):
        wm = jnp.where(tri, w_ref[g], 0.0).astype(BF16)
        cs = slice(g * LANES, (g + 1) * LANES)
        for c in range(rows // CHUNK):
            rs = slice(c * CHUNK, (c + 1) * CHUNK)
            mixed = jnp.dot(wm, vn[rs, cs], preferred_element_type=F32) + bsb_ref[:, cs]
            o_ref[rs, cs] = (u[rs, cs] * mixed).astype(BF16)


def _sgu(proj, ln_g, ln_b, w_s, bsb, *, rows):
    t = proj.shape[0]
    aw = ln_g.shape[1]
    return pl.pallas_call(
        _sgu_kernel,
        out_shape=jax.ShapeDtypeStruct((t, aw), BF16),
        grid=(t // rows,),
        in_specs=[
            pl.BlockSpec((rows, aw), lambda i: (i, OFF_AU // aw)),
            pl.BlockSpec((rows, aw), lambda i: (i, OFF_AV // aw)),
            pl.BlockSpec((1, aw), lambda i: (0, 0)),
            pl.BlockSpec((1, aw), lambda i: (0, 0)),
            pl.BlockSpec((A_GROUPS, CHUNK, CHUNK), lambda i: (0, 0, 0)),
            pl.BlockSpec((CHUNK, aw), lambda i: (0, 0)),
        ],
        out_specs=pl.BlockSpec((rows, aw), lambda i: (i, 0)),
        compiler_params=_cparams(1),
        name="sgu",
    )(proj, proj, ln_g, ln_b, w_s, bsb)


def _t5_bucket_np(dist):
    n = np.maximum(dist, 0)
    nf = np.maximum(n, 1).astype(np.float64)
    large = REL_EXACT + (np.log(nf / REL_EXACT) / math.log(REL_MAX_DIST / REL_EXACT)
                         * (REL_BUCKETS - REL_EXACT)).astype(np.int32)
    large = np.minimum(large, REL_BUCKETS - 1)
    return np.where(n < REL_EXACT, n, large).astype(np.int32)


def _bucket_tiles(bq, bk):
    last_bucket_from = int(np.argmax(_t5_bucket_np(np.arange(4 * REL_MAX_DIST)) == REL_BUCKETS - 1))
    n_tiles = -(-(bk - 1 + last_bucket_from) // bq)
    c = np.arange(bk)[:, None]
    r = np.arange(bq)[None, :]
    tiles = []
    for t in range(n_tiles):
        dist = t * bq + r - c
        tiles.append(np.where(dist >= 0, _t5_bucket_np(dist), REL_BUCKETS))
    assert _t5_bucket_np(np.array([n_tiles * bq - (bk - 1)]))[0] == REL_BUCKETS - 1
    return np.stack(tiles).astype(np.int32)


BIAS_ROWS = 64


def _bias_tiles_kernel(rb_ref, bkt_ref, o_ref, *, present):
    hd = pl.program_id(0)
    n_bias, bk, bq = o_ref.shape
    for t in range(n_bias - 1):
        for rc in range(bk // BIAS_ROWS):
            rs = slice(rc * BIAS_ROWS, (rc + 1) * BIAS_ROWS)
            bt = bkt_ref[t, rs, :]
            tile = jnp.where(bt == REL_BUCKETS, NEG_BIG, 0.0).astype(F32)
            for b in present[t][rc]:
                tile = jnp.where(bt == b, rb_ref[b, hd] * LOG2E, tile)
            o_ref[t, rs, :] = tile
    o_ref[n_bias - 1] = jnp.full((bk, bq), rb_ref[REL_BUCKETS - 1, hd] * LOG2E, F32)


def _bias_tiles(rel_bias, buckets_np):
    n_tiles, bk, bq = buckets_np.shape
    heads = rel_bias.shape[1]
    present = tuple(tuple(tuple(int(b) for b in np.unique(buckets_np[t, rc * BIAS_ROWS:(rc + 1) * BIAS_ROWS])
                                if b < REL_BUCKETS)
                          for rc in range(bk // BIAS_ROWS)) for t in range(n_tiles))
    return pl.pallas_call(
        functools.partial(_bias_tiles_kernel, present=present),
        out_shape=jax.ShapeDtypeStruct((heads, n_tiles + 1, bk, bq), F32),
        grid=(heads,),
        in_specs=[pl.BlockSpec(memory_space=pltpu.SMEM),
                  pl.BlockSpec((n_tiles, bk, bq), lambda h: (0, 0, 0))],
        out_specs=pl.BlockSpec((None, n_tiles + 1, bk, bq), lambda h: (h, 0, 0, 0)),
        compiler_params=_cparams(1),
        name="bias_tiles",
    )(rel_bias, jnp.asarray(buckets_np))


def _attn_kernel(lam_ref, sub_ref, bias_ref, q_ref, k_ref, v_ref, o_ref,
                 vt_ref, qqt_ref, s_ref, p_ref, cm_ref, m_ref, al_ref, acc_ref, *, lam_init):
    hw = 2 * B_HEAD_DIM
    hpb, bk = s_ref.shape[0], s_ref.shape[1]
    bq = s_ref.shape[2] // 2
    seq = k_ref.shape[0]
    nq = seq // bq
    n_bias = bias_ref.shape[1]
    cols = [slice(hh * hw, (hh + 1) * hw) for hh in range(hpb)]
    last_block = lambda i: (i * bq) // bk
    n_pairs = sum(last_block(i) + 1 for i in range(nq))

    pad_row = lax.broadcasted_iota(jnp.int32, (vt_ref.shape[2] - hw, bk), 0)
    ones_row = jnp.where(pad_row == 0, 1.0, 0.0).astype(BF16)
    dim = lax.broadcasted_iota(jnp.int32, (hw, bq), 0)
    for hh in range(hpb):
        for c in range(seq // LANES):
            blk, off = divmod(c * LANES, bk)
            vt_ref[hh, blk, 0:hw, off:off + LANES] = (
                v_ref[c * LANES:(c + 1) * LANES, cols[hh]].astype(F32).T.astype(BF16))
        for blk in range(seq // bk):
            vt_ref[hh, blk, hw:, :] = ones_row
        for i in range(nq):
            qt = (q_ref[i * bq:(i + 1) * bq, cols[hh]].astype(F32) * (B_HEAD_DIM ** -0.5 * LOG2E)).T
            qqt_ref[hh, i] = jnp.concatenate([jnp.where(dim < B_HEAD_DIM, qt, 0.0),
                                              jnp.where(dim >= B_HEAD_DIM, qt, 0.0)], axis=1).astype(BF16)
    m_ref[...] = jnp.full(m_ref.shape, NEG_BIG, F32)
    al_ref[...] = jnp.ones(al_ref.shape, F32)
    acc_ref[...] = jnp.zeros(acc_ref.shape, F32)
    p_ref[...] = jnp.zeros(p_ref.shape, BF16)

    def scores(hh, i, j):
        kb = k_ref[pl.ds(pl.multiple_of(j * bk, bk), bk), cols[hh]]
        s = jnp.dot(kb, qqt_ref[hh, i], preferred_element_type=F32)
        bias = bias_ref[hh, jnp.minimum((i * bq - j * bk) // bq, n_bias - 1)]
        s = s + jnp.concatenate([bias, bias], axis=1)
        s_ref[hh] = s
        cm_ref[hh] = jnp.max(s, axis=0, keepdims=True)

    def probabilities(hh, i):
        m_prev = m_ref[hh, i]
        m_next = jnp.maximum(m_prev, cm_ref[hh])
        m_ref[hh, i] = m_next
        al_ref[hh] = jnp.exp2(m_prev - m_next)
        p_ref[hh] = jnp.exp2(s_ref[hh] - m_next).astype(BF16)

    def weighted_values(hh, i, j):
        acc_ref[hh, i] = al_ref[hh] * acc_ref[hh, i] + jnp.dot(vt_ref[hh, j], p_ref[hh],
                                                                preferred_element_type=F32)

    for hh in range(hpb):
        scores(hh, 0, 0)

    def body(t, carry):
        ip, jp, ic, jc = carry
        wrap = jc >= last_block(ic)
        i_next = jnp.where(wrap, ic + 1, ic)
        j_next = jnp.where(wrap, 0, jc + 1)
        for hh in range(hpb):
            weighted_values(hh, ip, jp)
        for hh in range(hpb):
            probabilities(hh, ic)
        for hh in range(hpb):
            scores(hh, i_next, j_next)
        return ic, jc, i_next, j_next

    zero = jnp.int32(0)
    ip, jp, ic, jc = lax.fori_loop(0, n_pairs - 1, body, (zero, zero, zero, zero))
    for hh in range(hpb):
        weighted_values(hh, ip, jp)
        probabilities(hh, ic)
    for hh in range(hpb):
        weighted_values(hh, ic, jc)

    lam_p = lam_ref[...]
    lam = (jnp.exp(jnp.sum(lam_p[0:1] * lam_p[1:2], axis=1, keepdims=True))
           - jnp.exp(jnp.sum(lam_p[2:3] * lam_p[3:4], axis=1, keepdims=True)) + lam_init)
    for hh in range(hpb):
        for i in range(nq):
            o = acc_ref[hh, i, 0:hw, :] * (1.0 / acc_ref[hh, i, hw:hw + 1, :])
            attn = o[:, :bq] - lam * o[:, bq:]
            ms = jnp.mean(attn * attn, axis=0, keepdims=True)
            y = ((attn * lax.rsqrt(ms + EPS)) * sub_ref[...]) * (1.0 - lam_init)
            o_ref[i * bq:(i + 1) * bq, cols[hh]] = y.T.astype(BF16)


def _attn(proj, bias_tiles, lam_p, subln_t, *, batch, seq, bq, hpb, lam_init):
    t = proj.shape[0]
    hw = 2 * B_HEAD_DIM
    heads, n_bias, bk, _ = bias_tiles.shape
    nq = seq // bq
    bw = hpb * hw
    return pl.pallas_call(
        functools.partial(_attn_kernel, lam_init=lam_init),
        out_shape=jax.ShapeDtypeStruct((t, heads * hw), BF16),
        grid=(batch, heads // hpb),
        in_specs=[
            pl.BlockSpec((4, B_HEAD_DIM), lambda b, h: (0, 0)),
            pl.BlockSpec((hw, bq), lambda b, h: (0, 0)),
            pl.BlockSpec((hpb, n_bias, bk, bq), lambda b, h: (h, 0, 0, 0)),
            pl.BlockSpec((seq, bw), lambda b, h: (b, OFF_Q // bw + h)),
            pl.BlockSpec((seq, bw), lambda b, h: (b, OFF_K // bw + h)),
            pl.BlockSpec((seq, bw), lambda b, h: (b, OFF_V // bw + h)),
        ],
        out_specs=pl.BlockSpec((seq, bw), lambda b, h: (b, h)),
        scratch_shapes=[pltpu.VMEM((hpb, seq // bk, hw + BF16_ROWS, bk), BF16),
                        pltpu.VMEM((hpb, nq, hw, 2 * bq), BF16),
                        pltpu.VMEM((hpb, bk, 2 * bq), F32),
                        pltpu.VMEM((hpb, bk, 2 * bq), BF16),
                        pltpu.VMEM((hpb, 1, 2 * bq), F32),
                        pltpu.VMEM((hpb, nq, 1, 2 * bq), F32),
                        pltpu.VMEM((hpb, 1, 2 * bq), F32),
                        pltpu.VMEM((hpb, nq, hw + BF16_ROWS, 2 * bq), F32)],
        compiler_params=_cparams(2, vmem_limit=BIG_VMEM_LIMIT_V7X),
        name="diff_attn",
    )(lam_p, subln_t, bias_tiles, proj, proj, proj)


def _split_dot(x, e_bf16, passes):
    out = None
    r = x
    for _ in range(passes):
        hi = r.astype(BF16)
        part = jnp.dot(hi, e_bf16, preferred_element_type=F32)
        out = part if out is None else out + part
        r = r - hi.astype(F32)
    return out


def _mamba_kernel(xbc_ref, z_ref, dt_ref, cw_ref, cb_ref, dtb_ref, alog_ref, dsk_ref, ng_ref, e_ref,
                  o_ref, tail_ref, st_ref):
    rows, c_inner = z_ref.shape
    gw = c_inner // C_GROUPS
    heads_per_group = gw // C_HEAD_DIM
    c = pl.program_id(1)

    @pl.when(c == 0)
    def _():
        tail_ref[...] = jnp.zeros_like(tail_ref)
        st_ref[...] = jnp.zeros_like(st_ref)

    r_i = lax.broadcasted_iota(jnp.int32, (CHUNK, CHUNK), 0)
    c_i = lax.broadcasted_iota(jnp.int32, (CHUNK, CHUNK), 1)
    tri = r_i >= c_i
    lower = tri.astype(F32)
    upper = (r_i <= c_i).astype(F32)
    shift = jnp.concatenate([(r_i - c_i == d).astype(BF16) for d in range(C_CONV - 1, 0, -1)], axis=0)
    row8 = lax.broadcasted_iota(jnp.int32, (8, xbc_ref.shape[1]), 0)
    lane_lo = lax.broadcasted_iota(jnp.int32, (CHUNK, LANES), 1) < C_HEAD_DIM
    neg_a = -jnp.exp(alog_ref[...]) * LOG2E
    e = e_ref[...]
    hp = lax.Precision.HIGHEST
    prev = st_ref[...]

    for u in range(rows // CHUNK):
        rs = slice(u * CHUNK, (u + 1) * CHUNK)
        x_cur = xbc_ref[rs, :]
        delayed = jnp.dot(shift, x_cur, preferred_element_type=F32)
        if u == 0:
            tail = tail_ref[8:16, :]
        else:
            tail = xbc_ref[u * CHUNK - 16:u * CHUNK, :].astype(F32)[8:16, :]
        conv = cb_ref[...] + cw_ref[C_CONV - 1:C_CONV, :] * x_cur.astype(F32)
        conv_top = jnp.zeros_like(tail)
        for k in range(C_CONV - 1):
            d = C_CONV - 1 - k
            conv = conv + cw_ref[k:k + 1, :] * delayed[k * CHUNK:(k + 1) * CHUNK]
            conv_top = conv_top + cw_ref[k:k + 1, :] * jnp.where(row8 < d, pltpu.roll(tail, d, 0), 0.0)
        conv = jnp.concatenate([conv[0:8] + conv_top, conv[8:]], axis=0)
        xbc = conv * jax.nn.sigmoid(conv)
        xs = xbc[:, :c_inner]
        bm = xbc[:, c_inner:c_inner + C_GROUPS * C_STATE]
        cm = xbc[:, c_inner + C_GROUPS * C_STATE:]

        dt = jax.nn.softplus(dt_ref[rs, :] + dtb_ref[...])
        a = dt * neg_a
        a_cs = jnp.dot(lower, a, precision=hp, preferred_element_type=F32)
        a_cs_t = jnp.dot(a.T, upper, precision=hp, preferred_element_type=F32)

        dt_x = _split_dot(dt, e, 2)
        acs_x = _split_dot(a_cs, e, 3)
        last_x = acs_x[CHUNK - 1:CHUNK, :]
        xd = xs * dt_x
        xd_b = xd.astype(BF16)
        xdd_b = (xd * jnp.exp2(last_x - acs_x)).astype(BF16)
        prev_b = prev.astype(BF16)
        eacs = jnp.exp2(acs_x)

        y_parts = []
        st_parts = []
        for g in range(C_GROUPS):
            gs = slice(g * gw, (g + 1) * gw)
            b_g = bm[:, g * C_STATE:(g + 1) * C_STATE]
            c_g = cm[:, g * C_STATE:(g + 1) * C_STATE].astype(BF16)
            cb = lax.dot_general(c_g, b_g.astype(BF16), (((1,), (1,)), ((), ())), preferred_element_type=F32)
            st_parts.append(jnp.dot(b_g.T.astype(BF16), xdd_b[:, gs], preferred_element_type=F32))
            y_off = jnp.dot(c_g, prev_b[:, gs], preferred_element_type=F32) * eacs[:, gs]
            tiles = []
            for pr in range(heads_per_group // 2):
                t_idx = g * (heads_per_group // 2) + pr
                xt = xd_b[:, t_idx * LANES:(t_idx + 1) * LANES]
                res = []
                for half in range(2):
                    hd = 2 * t_idx + half
                    seg = a_cs[:, hd:hd + 1] - a_cs_t[hd:hd + 1, :]
                    decay = jnp.where(tri, jnp.exp2(jnp.minimum(seg, 0.0)), 0.0)
                    res.append(jnp.dot((cb * decay).astype(BF16), xt, preferred_element_type=F32))
                tiles.append(jnp.where(lane_lo, res[0], res[1]))
            y_parts.append(jnp.concatenate(tiles, axis=1) + y_off)

        prev = prev * jnp.exp2(last_x) + jnp.concatenate(st_parts, axis=1)
        y = jnp.concatenate(y_parts, axis=1) + dsk_ref[...] * xs
        zf = z_ref[rs, :].astype(F32)
        gated = y * (zf * jax.nn.sigmoid(zf))
        o_ref[rs, :] = _rmsnorm_f32(gated, ng_ref[...]).astype(BF16)

    st_ref[...] = prev
    tail_ref[...] = xbc_ref[rows - 16:rows, :].astype(F32)


def _mamba(proj, dt_raw, conv_w, conv_b, dt_bias, a_log, d_skip_x, norm_g, expand, *, batch, seq, rows):
    t = proj.shape[0]
    c_inner = norm_g.shape[1]
    conv_ch = conv_w.shape[1]
    nc = seq // rows
    row = lambda b, c: (b * nc + c, 0)
    const = lambda b, c: (0, 0)
    return pl.pallas_call(
        _mamba_kernel,
        out_shape=jax.ShapeDtypeStruct((t, c_inner), BF16),
        grid=(batch, nc),
        in_specs=[
            pl.BlockSpec((rows, conv_ch), lambda b, c: (b * nc + c, OFF_XBC // conv_ch)),
            pl.BlockSpec((rows, c_inner), lambda b, c: (b * nc + c, OFF_Z // c_inner)),
            pl.BlockSpec((rows, LANES), row),
            pl.BlockSpec((C_CONV, conv_ch), const),
            pl.BlockSpec((1, conv_ch), const),
            pl.BlockSpec((1, LANES), const),
            pl.BlockSpec((1, LANES), const),
            pl.BlockSpec((1, c_inner), const),
            pl.BlockSpec((1, c_inner), const),
            pl.BlockSpec((LANES, c_inner), const),
        ],
        out_specs=pl.BlockSpec((rows, c_inner), row),
        scratch_shapes=[pltpu.VMEM((16, conv_ch), F32), pltpu.VMEM((C_STATE, c_inner), F32)],
        compiler_params=_cparams(2),
        name="mamba_ssd",
    )(proj, proj, dt_raw, conv_w, conv_b, dt_bias, a_log, d_skip_x, norm_g, expand)


def _merge_kernel(ya_ref, yb_ref, yc_ref, g0_ref, g1_ref, g2_ref, wa_ref, wb_ref, wc_ref, o_ref):
    pa = jnp.dot(ya_ref[...], wa_ref[...], preferred_element_type=F32)
    pb = jnp.dot(yb_ref[...], wb_ref[...], preferred_element_type=F32)
    pc = jnp.dot(yc_ref[...], wc_ref[...], preferred_element_type=F32)
    o_ref[...] = (g0_ref[...].astype(F32) * pa + g1_ref[...].astype(F32) * pb
                  + g2_ref[...].astype(F32) * pc).astype(BF16)


def _merge(ya, yb, yc, proj, w_pa, w_pb, w_pc, layer, *, bm, bn):
    t = ya.shape[0]
    d = w_pa.shape[2]
    gate_spec = lambda k: pl.BlockSpec((bm, bn), lambda i, j: (i, (OFF_GATE + k * d) // bn + j))
    resident = pl.Buffered(1) if bn == d else None
    return pl.pallas_call(
        _merge_kernel,
        out_shape=jax.ShapeDtypeStruct((t, d), BF16),
        grid=(t // bm, d // bn),
        in_specs=[
            pl.BlockSpec((bm, ya.shape[1]), lambda i, j: (i, 0)),
            pl.BlockSpec((bm, yb.shape[1]), lambda i, j: (i, 0)),
            pl.BlockSpec((bm, yc.shape[1]), lambda i, j: (i, 0)),
            gate_spec(0), gate_spec(1), gate_spec(2),
            pl.BlockSpec((None, w_pa.shape[1], bn), lambda i, j: (layer, 0, j), pipeline_mode=resident),
            pl.BlockSpec((None, w_pb.shape[1], bn), lambda i, j: (layer, 0, j), pipeline_mode=resident),
            pl.BlockSpec((None, w_pc.shape[1], bn), lambda i, j: (layer, 0, j), pipeline_mode=resident),
        ],
        out_specs=pl.BlockSpec((bm, bn), lambda i, j: (i, j)),
        compiler_params=_cparams(2),
        name="merge",
    )(ya, yb, yc, proj, proj, proj, w_pa, w_pb, w_pc)


def _outproj_kernel(m_ref, w_ref, h_ref, o_ref):
    o_ref[...] = h_ref[...] + jnp.dot(m_ref[...], w_ref[...], preferred_element_type=F32)


def _outproj(merged, w_out, h, layer, *, bm, bn):
    t, d = h.shape
    return pl.pallas_call(
        _outproj_kernel,
        out_shape=jax.ShapeDtypeStruct((t, d), F32),
        grid=(t // bm, d // bn),
        in_specs=[
            pl.BlockSpec((bm, merged.shape[1]), lambda i, j: (i, 0)),
            pl.BlockSpec((None, merged.shape[1], bn), lambda i, j: (layer, 0, j),
                         pipeline_mode=pl.Buffered(1) if bn == d else None),
            pl.BlockSpec((bm, bn), lambda i, j: (i, j)),
        ],
        out_specs=pl.BlockSpec((bm, bn), lambda i, j: (i, j)),
        compiler_params=_cparams(2),
        name="outproj",
    )(merged, w_out, h)


def _final_norm_kernel(x_ref, g_ref, o_ref):
    o_ref[...] = _rmsnorm_f32(x_ref[...], g_ref[...])


def _final_norm(h, g, *, bm):
    t, d = h.shape
    return pl.pallas_call(
        _final_norm_kernel,
        out_shape=jax.ShapeDtypeStruct((t, d), F32),
        grid=(t // bm,),
        in_specs=[pl.BlockSpec((bm, d), lambda i: (i, 0)), pl.BlockSpec((1, d), lambda i: (0, 0))],
        out_specs=pl.BlockSpec((bm, d), lambda i: (i, 0)),
        compiler_params=_cparams(1),
        name="final_norm",
    )(h, g)


def kernel(x, rel_bias, final_norm, ffn1_norm, ffn1_wi, ffn1_wo, mix_norm, w_in, sgu_ln_g, sgu_ln_b, sgu_w, sgu_b, diff_lambda, diff_subln, conv_w, conv_b, dt_bias, a_log, d_skip, ssm_norm, w_pa, w_pb, w_pc, w_out, ffn2_norm, ffn2_wi, ffn2_wo):
    batch, seq, d = x.shape
    depth = ffn1_wi.shape[0]
    t = batch * seq
    heads_c = dt_bias.shape[1]
    a_width = sgu_ln_g.shape[1]
    c_inner = ssm_norm.shape[1]
    conv_ch = conv_w.shape[2]
    d_ff = ffn1_wo.shape[1]

    bm = min(1024, t)
    bf = 512
    bq = 512
    bk = 512
    assert t % bm == 0 and d_ff % bf == 0 and seq % bk == 0 and bk % bq == 0 and bq % LANES == 0
    assert w_in.shape[2] == OFF_GATE + heads_c + 3 * d and conv_ch == OFF_AU

    o_au, o_av, o_q, o_z, o_xbc = 0, a_width, 2 * a_width, 5 * a_width, 5 * a_width + c_inner
    o_dt = o_xbc + conv_ch
    o_g = o_dt + heads_c
    assert (o_xbc, o_au, o_z, o_av, o_q) == tuple(MAIN_SRC_TILES[k] * REORDER_TILE for k in (0, 3, 4, 6, 7))
    w0 = w_in[0]
    w_main = jnp.concatenate(
        [w0[:, o_xbc:o_dt], w0[:, o_au:o_av], w0[:, o_z:o_xbc], w0[:, o_av:o_q], w0[:, o_q:o_z], w0[:, o_g:]],
        axis=1).astype(BF16)
    w_gate = w_in[min(1, depth - 1):, :, o_g:].astype(BF16)
    w_dt = jnp.pad(w_in[:, :, o_dt:o_g], ((0, 0), (0, 0), (0, LANES - heads_c))).astype(BF16)
    wi1, wo1 = ffn1_wi[0].astype(BF16), ffn1_wo[0].astype(BF16)
    wi2, wo2 = ffn2_wi[0].astype(BF16), ffn2_wo[0].astype(BF16)
    wpa, wpb, wpc, wout = w_pa.astype(BF16), w_pb.astype(BF16), w_pc.astype(BF16), w_out.astype(BF16)

    bsb = jnp.repeat(jnp.swapaxes(sgu_b, 1, 2), a_width // A_GROUPS, axis=2)
    pad_h = ((0, 0), (0, LANES - heads_c))
    dtb = jnp.pad(dt_bias, pad_h)
    alog = jnp.pad(a_log, pad_h)
    dsk_x = jnp.repeat(d_skip, C_HEAD_DIM, axis=1)
    expand = np.zeros((LANES, c_inner), np.float32)
    expand[np.arange(c_inner) // C_HEAD_DIM, np.arange(c_inner)] = 1.0
    expand = jnp.asarray(expand, BF16)
    bias_tiles = _bias_tiles(rel_bias, _bucket_tiles(bq, bk))
    subln_t = jnp.broadcast_to(diff_subln[:, :, None], diff_subln.shape + (bq,))

    h = x.reshape(t, d)
    for l in range(depth):
        lam_init = 0.8 - 0.6 * math.exp(-0.3 * l)
        nxt = min(l + 1, depth - 1)
        h, wi1, wo1 = _ffn(h, ffn1_norm[l][None], wi1, wo1, ffn1_wi, ffn1_wo, nxt, bm=bm, bf=bf)
        proj, dt_raw, w_main = _inproj(h, mix_norm[l][None], w_main, w_dt, w_in, w_gate, l, nxt, bm=bm, bn=2048)
        ya = _sgu(proj, sgu_ln_g[l][None], sgu_ln_b[l][None], sgu_w[l], bsb[l], rows=2 * CHUNK)
        yb = _attn(proj, bias_tiles, diff_lambda[l], subln_t[l],
                   batch=batch, seq=seq, bq=bq, hpb=2, lam_init=lam_init)
        yc = _mamba(proj, dt_raw, conv_w[l], conv_b[l][None], dtb[l][None], alog[l][None],
                    dsk_x[l][None], ssm_norm[l][None], expand, batch=batch, seq=seq, rows=2 * CHUNK)
        merged = _merge(ya, yb, yc, proj, wpa, wpb, wpc, l, bm=bm // 2, bn=d)
        h = _outproj(merged, wout, h, l, bm=bm, bn=d)
        h, wi2, wo2 = _ffn(h, ffn2_norm[l][None], wi2, wo2, ffn2_wi, ffn2_wo, nxt, bm=bm, bf=bf)
    return _final_norm(h, final_norm[None], bm=bm).reshape(batch, seq, d)
```

```python
import functools
import math

import numpy as np
import jax
import jax.numpy as jnp
from jax import lax
from jax.experimental import pallas as pl
from jax.experimental.pallas import tpu as pltpu

F32 = jnp.float32
BF16 = jnp.bfloat16

EPS = 1e-6
CHUNK = 128
LANES = 128
BF16_ROWS = 16
A_GROUPS = 8
B_HEAD_DIM = 64
C_HEAD_DIM = 64
C_GROUPS = 4
C_STATE = 128
C_CONV = 4
REL_BUCKETS = 32
REL_MAX_DIST = 128
REL_EXACT = REL_BUCKETS // 2
NEG_BIG = -1e30
LOG2E = math.log2(math.e)
VMEM_LIMIT_V7X = 56 * 1024 * 1024
BIG_VMEM_LIMIT_V7X = 60 * 1024 * 1024

OFF_XBC, OFF_AU, OFF_Z, OFF_AV, OFF_Q, OFF_K, OFF_V, OFF_GATE = 0, 3072, 4096, 6144, 7168, 8192, 9216, 10240


def _cparams(n_axes, vmem_limit=VMEM_LIMIT_V7X):
    return pltpu.CompilerParams(dimension_semantics=("arbitrary",) * n_axes, vmem_limit_bytes=vmem_limit)


def _rmsnorm_f32(x, g):
    ms = jnp.mean(x * x, axis=-1, keepdims=True)
    return (x * lax.rsqrt(ms + EPS)) * g


def _ffn_kernel(x_ref, g_ref, wg_ref, wu_ref, wo_ref, wi_src_ref, wo_src_ref, o_ref, wi_cast_ref, wo_cast_ref,
                xn_ref):
    j = pl.program_id(1)

    @pl.when(j == 0)
    def _():
        x = x_ref[...]
        xn_ref[...] = _rmsnorm_f32(x, g_ref[...]).astype(BF16)
        o_ref[...] = x

    xn = xn_ref[...]
    gate = jnp.dot(xn, wg_ref[...], preferred_element_type=F32)
    up = jnp.dot(xn, wu_ref[...], preferred_element_type=F32)
    act = ((0.5 * gate) * jax.nn.sigmoid(gate) * up).astype(BF16)
    o_ref[...] += jnp.dot(act, wo_ref[...], preferred_element_type=F32)
    wi_cast_ref[...] = wi_src_ref[...].astype(BF16)
    wo_cast_ref[...] = wo_src_ref[...].astype(BF16)


def _ffn(h, g, wi, wo, wi_f32, wo_f32, cast_layer, *, bm, bf):
    t, d = h.shape
    d_ff = wo.shape[0]
    ni, nj = t // bm, d_ff // bf
    wi_tile = (d // ni, 2 * d_ff // nj)
    wo_tile = (d_ff // nj, d // ni)
    assert d % ni == 0 and wi_tile[0] % BF16_ROWS == 0 and wi_tile[1] % LANES == 0 and wo_tile[1] % LANES == 0
    return pl.pallas_call(
        _ffn_kernel,
        out_shape=(jax.ShapeDtypeStruct((t, d), F32), jax.ShapeDtypeStruct(wi.shape, BF16),
                   jax.ShapeDtypeStruct(wo.shape, BF16)),
        grid=(ni, nj),
        in_specs=[
            pl.BlockSpec((bm, d), lambda i, j: (i, 0)),
            pl.BlockSpec((1, d), lambda i, j: (0, 0)),
            pl.BlockSpec((d, bf), lambda i, j: (0, j)),
            pl.BlockSpec((d, bf), lambda i, j: (0, j + nj)),
            pl.BlockSpec((bf, d), lambda i, j: (j, 0)),
            pl.BlockSpec((None,) + wi_tile, lambda i, j: (cast_layer, i, j)),
            pl.BlockSpec((None,) + wo_tile, lambda i, j: (cast_layer, j, i)),
        ],
        out_specs=(pl.BlockSpec((bm, d), lambda i, j: (i, 0)),
                   pl.BlockSpec(wi_tile, lambda i, j: (i, j)),
                   pl.BlockSpec(wo_tile, lambda i, j: (j, i))),
        scratch_shapes=[pltpu.VMEM((bm, d), BF16)],
        compiler_params=_cparams(2, vmem_limit=BIG_VMEM_LIMIT_V7X),
        name="ffn",
    )(h, g, wi, wi, wo, wi_f32, wo_f32)


def _inproj_kernel(x_ref, g_ref, w_ref, wdt_ref, o_ref, dt_ref, xn_ref, *, nj_plain):
    j = pl.program_id(1)

    @pl.when(j == 0)
    def _():
        xn = _rmsnorm_f32(x_ref[...], g_ref[...]).astype(BF16)
        xn_ref[...] = xn
        dt_ref[...] = jnp.dot(xn, wdt_ref[...], preferred_element_type=F32)

    @pl.when(j < nj_plain)
    def _():
        o_ref[...] = jnp.dot(xn_ref[...], w_ref[...], preferred_element_type=F32).astype(BF16)

    @pl.when(j >= nj_plain)
    def _():
        o_ref[...] = jax.nn.sigmoid(jnp.dot(xn_ref[...], w_ref[...], preferred_element_type=F32)).astype(BF16)


def _inproj(h, g, w_main, w_dt, layer, *, bm, bn):
    t, d = h.shape
    n = w_main.shape[2]
    return pl.pallas_call(
        functools.partial(_inproj_kernel, nj_plain=OFF_GATE // bn),
        out_shape=(jax.ShapeDtypeStruct((t, n), BF16), jax.ShapeDtypeStruct((t, LANES), F32)),
        grid=(t // bm, n // bn),
        in_specs=[
            pl.BlockSpec((bm, d), lambda i, j: (i, 0)),
            pl.BlockSpec((1, d), lambda i, j: (0, 0)),
            pl.BlockSpec((None, d, bn), lambda i, j: (layer, 0, j)),
            pl.BlockSpec((None, d, LANES), lambda i, j: (layer, 0, 0)),
        ],
        out_specs=(pl.BlockSpec((bm, bn), lambda i, j: (i, j)),
                   pl.BlockSpec((bm, LANES), lambda i, j: (i, 0))),
        scratch_shapes=[pltpu.VMEM((bm, d), BF16)],
        compiler_params=_cparams(2),
        name="inproj",
    )(h, g, w_main, w_dt)


def _sgu_kernel(u_ref, v_ref, lng_ref, lnb_ref, w_ref, bsb_ref, o_ref):
    rows = u_ref.shape[0]
    u = jax.nn.gelu(u_ref[...].astype(F32))
    v = jax.nn.gelu(v_ref[...].astype(F32))
    mu = jnp.mean(v, axis=-1, keepdims=True)
    var = jnp.mean(jnp.square(v - mu), axis=-1, keepdims=True)
    vn = (((v - mu) * lax.rsqrt(var + EPS)) * lng_ref[...] + lnb_ref[...]).astype(BF16)
    r_i = lax.broadcasted_iota(jnp.int32, (CHUNK, CHUNK), 0)
    c_i = lax.broadcasted_iota(jnp.int32, (CHUNK, CHUNK), 1)
    tri = r_i >= c_i
    for g in range(A_GROUPS):
        wm = jnp.where(tri, w_ref[g], 0.0).astype(BF16)
        cs = slice(g * LANES, (g + 1) * LANES)
        for c in range(rows // CHUNK):
            rs = slice(c * CHUNK, (c + 1) * CHUNK)
            mixed = jnp.dot(wm, vn[rs, cs], preferred_element_type=F32) + bsb_ref[:, cs]
            o_ref[rs, cs] = (u[rs, cs] * mixed).astype(BF16)


def _sgu(proj, ln_g, ln_b, w_s, bsb, *, rows):
    t = proj.shape[0]
    aw = ln_g.shape[1]
    return pl.pallas_call(
        _sgu_kernel,
        out_shape=jax.ShapeDtypeStruct((t, aw), BF16),
        grid=(t // rows,),
        in_specs=[
            pl.BlockSpec((rows, aw), lambda i: (i, OFF_AU // aw)),
            pl.BlockSpec((rows, aw), lambda i: (i, OFF_AV // aw)),
            pl.BlockSpec((1, aw), lambda i: (0, 0)),
            pl.BlockSpec((1, aw), lambda i: (0, 0)),
            pl.BlockSpec((A_GROUPS, CHUNK, CHUNK), lambda i: (0, 0, 0)),
            pl.BlockSpec((CHUNK, aw), lambda i: (0, 0)),
        ],
        out_specs=pl.BlockSpec((rows, aw), lambda i: (i, 0)),
        compiler_params=_cparams(1),
        name="sgu",
    )(proj, proj, ln_g, ln_b, w_s, bsb)


def _t5_bucket_np(dist):
    n = np.maximum(dist, 0)
    nf = np.maximum(n, 1).astype(np.float64)
    large = REL_EXACT + (np.log(nf / REL_EXACT) / math.log(REL_MAX_DIST / REL_EXACT)
                         * (REL_BUCKETS - REL_EXACT)).astype(np.int32)
    large = np.minimum(large, REL_BUCKETS - 1)
    return np.where(n < REL_EXACT, n, large).astype(np.int32)


def _bucket_tiles(bq, bk):
    last_bucket_from = int(np.argmax(_t5_bucket_np(np.arange(4 * REL_MAX_DIST)) == REL_BUCKETS - 1))
    n_tiles = -(-(bk - 1 + last_bucket_from) // bq)
    c = np.arange(bk)[:, None]
    r = np.arange(bq)[None, :]
    tiles = []
    for t in range(n_tiles):
        dist = t * bq + r - c
        tiles.append(np.where(dist >= 0, _t5_bucket_np(dist), REL_BUCKETS))
    assert _t5_bucket_np(np.array([n_tiles * bq - (bk - 1)]))[0] == REL_BUCKETS - 1
    return np.stack(tiles).astype(np.int32)


BIAS_ROWS = 64


def _bias_tiles_kernel(rb_ref, bkt_ref, o_ref, *, present):
    hd = pl.program_id(0)
    n_bias, bk, bq = o_ref.shape
    for t in range(n_bias - 1):
        for rc in range(bk // BIAS_ROWS):
            rs = slice(rc * BIAS_ROWS, (rc + 1) * BIAS_ROWS)
            bt = bkt_ref[t, rs, :]
            tile = jnp.where(bt == REL_BUCKETS, NEG_BIG, 0.0).astype(F32)
            for b in present[t][rc]:
                tile = jnp.where(bt == b, rb_ref[b, hd] * LOG2E, tile)
            o_ref[t, rs, :] = tile
    o_ref[n_bias - 1] = jnp.full((bk, bq), rb_ref[REL_BUCKETS - 1, hd] * LOG2E, F32)


def _bias_tiles(rel_bias, buckets_np):
    n_tiles, bk, bq = buckets_np.shape
    heads = rel_bias.shape[1]
    present = tuple(tuple(tuple(int(b) for b in np.unique(buckets_np[t, rc * BIAS_ROWS:(rc + 1) * BIAS_ROWS])
                                if b < REL_BUCKETS)
                          for rc in range(bk // BIAS_ROWS)) for t in range(n_tiles))
    return pl.pallas_call(
        functools.partial(_bias_tiles_kernel, present=present),
        out_shape=jax.ShapeDtypeStruct((heads, n_tiles + 1, bk, bq), F32),
        grid=(heads,),
        in_specs=[pl.BlockSpec(memory_space=pltpu.SMEM),
                  pl.BlockSpec((n_tiles, bk, bq), lambda h: (0, 0, 0))],
        out_specs=pl.BlockSpec((None, n_tiles + 1, bk, bq), lambda h: (h, 0, 0, 0)),
        compiler_params=_cparams(1),
        name="bias_tiles",
    )(rel_bias, jnp.asarray(buckets_np))


def _attn_kernel(lam_ref, sub_ref, bias_ref, q_ref, k_ref, v_ref, o_ref,
                 vt_ref, qqt_ref, s_ref, p_ref, cm_ref, m_ref, al_ref, acc_ref, *, lam_init):
    hw = 2 * B_HEAD_DIM
    hpb, bk = s_ref.shape[0], s_ref.shape[1]
    bq = s_ref.shape[2] // 2
    seq = k_ref.shape[0]
    nq = seq // bq
    n_bias = bias_ref.shape[1]
    cols = [slice(hh * hw, (hh + 1) * hw) for hh in range(hpb)]
    last_block = lambda i: (i * bq) // bk
    n_pairs = sum(last_block(i) + 1 for i in range(nq))

    pad_row = lax.broadcasted_iota(jnp.int32, (vt_ref.shape[2] - hw, bk), 0)
    ones_row = jnp.where(pad_row == 0, 1.0, 0.0).astype(BF16)
    dim = lax.broadcasted_iota(jnp.int32, (hw, bq), 0)
    for hh in range(hpb):
        for c in range(seq // LANES):
            blk, off = divmod(c * LANES, bk)
            vt_ref[hh, blk, 0:hw, off:off + LANES] = (
                v_ref[c * LANES:(c + 1) * LANES, cols[hh]].astype(F32).T.astype(BF16))
        for blk in range(seq // bk):
            vt_ref[hh, blk, hw:, :] = ones_row
        for i in range(nq):
            qt = (q_ref[i * bq:(i + 1) * bq, cols[hh]].astype(F32) * (B_HEAD_DIM ** -0.5 * LOG2E)).T
            qqt_ref[hh, i] = jnp.concatenate([jnp.where(dim < B_HEAD_DIM, qt, 0.0),
                                              jnp.where(dim >= B_HEAD_DIM, qt, 0.0)], axis=1).astype(BF16)
    m_ref[...] = jnp.full(m_ref.shape, NEG_BIG, F32)
    al_ref[...] = jnp.ones(al_ref.shape, F32)
    acc_ref[...] = jnp.zeros(acc_ref.shape, F32)
    p_ref[...] = jnp.zeros(p_ref.shape, BF16)

    def scores(hh, i, j):
        kb = k_ref[pl.ds(pl.multiple_of(j * bk, bk), bk), cols[hh]]
        s = jnp.dot(kb, qqt_ref[hh, i], preferred_element_type=F32)
        bias = bias_ref[hh, jnp.minimum((i * bq - j * bk) // bq, n_bias - 1)]
        s = s + jnp.concatenate([bias, bias], axis=1)
        s_ref[hh] = s
        cm_ref[hh] = jnp.max(s, axis=0, keepdims=True)

    def probabilities(hh, i):
        m_prev = m_ref[hh, i]
        m_next = jnp.maximum(m_prev, cm_ref[hh])
        m_ref[hh, i] = m_next
        al_ref[hh] = jnp.exp2(m_prev - m_next)
        p_ref[hh] = jnp.exp2(s_ref[hh] - m_next).astype(BF16)

    def weighted_values(hh, i, j):
        acc_ref[hh, i] = al_ref[hh] * acc_ref[hh, i] + jnp.dot(vt_ref[hh, j], p_ref[hh],
                                                                preferred_element_type=F32)

    for hh in range(hpb):
        scores(hh, 0, 0)

    def body(t, carry):
        ip, jp, ic, jc = carry
        wrap = jc >= last_block(ic)
        i_next = jnp.where(wrap, ic + 1, ic)
        j_next = jnp.where(wrap, 0, jc + 1)
        for hh in range(hpb):
            weighted_values(hh, ip, jp)
        for hh in range(hpb):
            probabilities(hh, ic)
        for hh in range(hpb):
            scores(hh, i_next, j_next)
        return ic, jc, i_next, j_next

    zero = jnp.int32(0)
    ip, jp, ic, jc = lax.fori_loop(0, n_pairs - 1, body, (zero, zero, zero, zero))
    for hh in range(hpb):
        weighted_values(hh, ip, jp)
        probabilities(hh, ic)
    for hh in range(hpb):
        weighted_values(hh, ic, jc)

    lam_p = lam_ref[...]
    lam = (jnp.exp(jnp.sum(lam_p[0:1] * lam_p[1:2], axis=1, keepdims=True))
           - jnp.exp(jnp.sum(lam_p[2:3] * lam_p[3:4], axis=1, keepdims=True)) + lam_init)
    for hh in range(hpb):
        for i in range(nq):
            o = acc_ref[hh, i, 0:hw, :] * (1.0 / acc_ref[hh, i, hw:hw + 1, :])
            attn = o[:, :bq] - lam * o[:, bq:]
            ms = jnp.mean(attn * attn, axis=0, keepdims=True)
            y = ((attn * lax.rsqrt(ms + EPS)) * sub_ref[...]) * (1.0 - lam_init)
            o_ref[i * bq:(i + 1) * bq, cols[hh]] = y.T.astype(BF16)


def _attn(proj, bias_tiles, lam_p, subln_t, *, batch, seq, bq, hpb, lam_init):
    t = proj.shape[0]
    hw = 2 * B_HEAD_DIM
    heads, n_bias, bk, _ = bias_tiles.shape
    nq = seq // bq
    bw = hpb * hw
    return pl.pallas_call(
        functools.partial(_attn_kernel, lam_init=lam_init),
        out_shape=jax.ShapeDtypeStruct((t, heads * hw), BF16),
        grid=(batch, heads // hpb),
        in_specs=[
            pl.BlockSpec((4, B_HEAD_DIM), lambda b, h: (0, 0)),
            pl.BlockSpec((hw, bq), lambda b, h: (0, 0)),
            pl.BlockSpec((hpb, n_bias, bk, bq), lambda b, h: (h, 0, 0, 0)),
            pl.BlockSpec((seq, bw), lambda b, h: (b, OFF_Q // bw + h)),
            pl.BlockSpec((seq, bw), lambda b, h: (b, OFF_K // bw + h)),
            pl.BlockSpec((seq, bw), lambda b, h: (b, OFF_V // bw + h)),
        ],
        out_specs=pl.BlockSpec((seq, bw), lambda b, h: (b, h)),
        scratch_shapes=[pltpu.VMEM((hpb, seq // bk, hw + BF16_ROWS, bk), BF16),
                        pltpu.VMEM((hpb, nq, hw, 2 * bq), BF16),
                        pltpu.VMEM((hpb, bk, 2 * bq), F32),
                        pltpu.VMEM((hpb, bk, 2 * bq), BF16),
                        pltpu.VMEM((hpb, 1, 2 * bq), F32),
                        pltpu.VMEM((hpb, nq, 1, 2 * bq), F32),
                        pltpu.VMEM((hpb, 1, 2 * bq), F32),
                        pltpu.VMEM((hpb, nq, hw + BF16_ROWS, 2 * bq), F32)],
        compiler_params=_cparams(2, vmem_limit=BIG_VMEM_LIMIT_V7X),
        name="diff_attn",
    )(lam_p, subln_t, bias_tiles, proj, proj, proj)


def _split_dot(x, e_bf16, passes):
    out = None
    r = x
    for _ in range(passes):
        hi = r.astype(BF16)
        part = jnp.dot(hi, e_bf16, preferred_element_type=F32)
        out = part if out is None else out + part
        r = r - hi.astype(F32)
    return out


def _mamba_kernel(xbc_ref, z_ref, dt_ref, cw_ref, cb_ref, dtb_ref, alog_ref, dsk_ref, ng_ref, e_ref,
                  o_ref, tail_ref, st_ref):
    rows, c_inner = z_ref.shape
    gw = c_inner // C_GROUPS
    heads_per_group = gw // C_HEAD_DIM
    c = pl.program_id(1)

    @pl.when(c == 0)
    def _():
        tail_ref[...] = jnp.zeros_like(tail_ref)
        st_ref[...] = jnp.zeros_like(st_ref)

    r_i = lax.broadcasted_iota(jnp.int32, (CHUNK, CHUNK), 0)
    c_i = lax.broadcasted_iota(jnp.int32, (CHUNK, CHUNK), 1)
    tri = r_i >= c_i
    lower = tri.astype(F32)
    upper = (r_i <= c_i).astype(F32)
    shift = jnp.concatenate([(r_i - c_i == d).astype(BF16) for d in range(C_CONV - 1, 0, -1)], axis=0)
    row8 = lax.broadcasted_iota(jnp.int32, (8, xbc_ref.shape[1]), 0)
    lane_lo = lax.broadcasted_iota(jnp.int32, (CHUNK, LANES), 1) < C_HEAD_DIM
    neg_a = -jnp.exp(alog_ref[...]) * LOG2E
    e = e_ref[...]
    hp = lax.Precision.HIGHEST
    prev = st_ref[...]

    for u in range(rows // CHUNK):
        rs = slice(u * CHUNK, (u + 1) * CHUNK)
        x_cur = xbc_ref[rs, :]
        delayed = jnp.dot(shift, x_cur, preferred_element_type=F32)
        if u == 0:
            tail = tail_ref[8:16, :]
        else:
            tail = xbc_ref[u * CHUNK - 16:u * CHUNK, :].astype(F32)[8:16, :]
        conv = cb_ref[...] + cw_ref[C_CONV - 1:C_CONV, :] * x_cur.astype(F32)
        conv_top = jnp.zeros_like(tail)
        for k in range(C_CONV - 1):
            d = C_CONV - 1 - k
            conv = conv + cw_ref[k:k + 1, :] * delayed[k * CHUNK:(k + 1) * CHUNK]
            conv_top = conv_top + cw_ref[k:k + 1, :] * jnp.where(row8 < d, pltpu.roll(tail, d, 0), 0.0)
        conv = jnp.concatenate([conv[0:8] + conv_top, conv[8:]], axis=0)
        xbc = conv * jax.nn.sigmoid(conv)
        xs = xbc[:, :c_inner]
        bm = xbc[:, c_inner:c_inner + C_GROUPS * C_STATE]
        cm = xbc[:, c_inner + C_GROUPS * C_STATE:]

        dt = jax.nn.softplus(dt_ref[rs, :] + dtb_ref[...])
        a = dt * neg_a
        a_cs = jnp.dot(lower, a, precision=hp, preferred_element_type=F32)
        a_cs_t = jnp.dot(a.T, upper, precision=hp, preferred_element_type=F32)

        dt_x = _split_dot(dt, e, 2)
        acs_x = _split_dot(a_cs, e, 3)
        last_x = acs_x[CHUNK - 1:CHUNK, :]
        xd = xs * dt_x
        xd_b = xd.astype(BF16)
        xdd_b = (xd * jnp.exp2(last_x - acs_x)).astype(BF16)
        prev_b = prev.astype(BF16)
        eacs = jnp.exp2(acs_x)

        y_parts = []
        st_parts = []
        for g in range(C_GROUPS):
            gs = slice(g * gw, (g + 1) * gw)
            b_g = bm[:, g * C_STATE:(g + 1) * C_STATE]
            c_g = cm[:, g * C_STATE:(g + 1) * C_STATE].astype(BF16)
            cb = lax.dot_general(c_g, b_g.astype(BF16), (((1,), (1,)), ((), ())), preferred_element_type=F32)
            st_parts.append(jnp.dot(b_g.T.astype(BF16), xdd_b[:, gs], preferred_element_type=F32))
            y_off = jnp.dot(c_g, prev_b[:, gs], preferred_element_type=F32) * eacs[:, gs]
            tiles = []
            for pr in range(heads_per_group // 2):
                t_idx = g * (heads_per_group // 2) + pr
                xt = xd_b[:, t_idx * LANES:(t_idx + 1) * LANES]
                res = []
                for half in range(2):
                    hd = 2 * t_idx + half
                    seg = a_cs[:, hd:hd + 1] - a_cs_t[hd:hd + 1, :]
                    decay = jnp.exp2(jnp.where(tri, seg, NEG_BIG))
                    res.append(jnp.dot((cb * decay).astype(BF16), xt, preferred_element_type=F32))
                tiles.append(jnp.where(lane_lo, res[0], res[1]))
            y_parts.append(jnp.concatenate(tiles, axis=1) + y_off)

        prev = prev * jnp.exp2(last_x) + jnp.concatenate(st_parts, axis=1)
        y = jnp.concatenate(y_parts, axis=1) + dsk_ref[...] * xs
        zf = z_ref[rs, :].astype(F32)
        gated = y * (zf * jax.nn.sigmoid(zf))
        o_ref[rs, :] = _rmsnorm_f32(gated, ng_ref[...]).astype(BF16)

    st_ref[...] = prev
    tail_ref[...] = xbc_ref[rows - 16:rows, :].astype(F32)


def _mamba(proj, dt_raw, conv_w, conv_b, dt_bias, a_log, d_skip_x, norm_g, expand, *, batch, seq, rows):
    t = proj.shape[0]
    c_inner = norm_g.shape[1]
    conv_ch = conv_w.shape[1]
    nc = seq // rows
    row = lambda b, c: (b * nc + c, 0)
    const = lambda b, c: (0, 0)
    return pl.pallas_call(
        _mamba_kernel,
        out_shape=jax.ShapeDtypeStruct((t, c_inner), BF16),
        grid=(batch, nc),
        in_specs=[
            pl.BlockSpec((rows, conv_ch), lambda b, c: (b * nc + c, OFF_XBC // conv_ch)),
            pl.BlockSpec((rows, c_inner), lambda b, c: (b * nc + c, OFF_Z // c_inner)),
            pl.BlockSpec((rows, LANES), row),
            pl.BlockSpec((C_CONV, conv_ch), const),
            pl.BlockSpec((1, conv_ch), const),
            pl.BlockSpec((1, LANES), const),
            pl.BlockSpec((1, LANES), const),
            pl.BlockSpec((1, c_inner), const),
            pl.BlockSpec((1, c_inner), const),
            pl.BlockSpec((LANES, c_inner), const),
        ],
        out_specs=pl.BlockSpec((rows, c_inner), row),
        scratch_shapes=[pltpu.VMEM((16, conv_ch), F32), pltpu.VMEM((C_STATE, c_inner), F32)],
        compiler_params=_cparams(2),
        name="mamba_ssd",
    )(proj, proj, dt_raw, conv_w, conv_b, dt_bias, a_log, d_skip_x, norm_g, expand)


def _merge_kernel(ya_ref, yb_ref, yc_ref, g0_ref, g1_ref, g2_ref, wa_ref, wb_ref, wc_ref, o_ref):
    pa = jnp.dot(ya_ref[...], wa_ref[...], preferred_element_type=F32)
    pb = jnp.dot(yb_ref[...], wb_ref[...], preferred_element_type=F32)
    pc = jnp.dot(yc_ref[...], wc_ref[...], preferred_element_type=F32)
    o_ref[...] = (g0_ref[...].astype(F32) * pa + g1_ref[...].astype(F32) * pb
                  + g2_ref[...].astype(F32) * pc).astype(BF16)


def _merge(ya, yb, yc, proj, w_pa, w_pb, w_pc, layer, *, bm, bn):
    t = ya.shape[0]
    d = w_pa.shape[2]
    gate_spec = lambda k: pl.BlockSpec((bm, bn), lambda i, j: (i, (OFF_GATE + k * d) // bn + j))
    resident = pl.Buffered(1) if bn == d else None
    return pl.pallas_call(
        _merge_kernel,
        out_shape=jax.ShapeDtypeStruct((t, d), BF16),
        grid=(t // bm, d // bn),
        in_specs=[
            pl.BlockSpec((bm, ya.shape[1]), lambda i, j: (i, 0)),
            pl.BlockSpec((bm, yb.shape[1]), lambda i, j: (i, 0)),
            pl.BlockSpec((bm, yc.shape[1]), lambda i, j: (i, 0)),
            gate_spec(0), gate_spec(1), gate_spec(2),
            pl.BlockSpec((None, w_pa.shape[1], bn), lambda i, j: (layer, 0, j), pipeline_mode=resident),
            pl.BlockSpec((None, w_pb.shape[1], bn), lambda i, j: (layer, 0, j), pipeline_mode=resident),
            pl.BlockSpec((None, w_pc.shape[1], bn), lambda i, j: (layer, 0, j), pipeline_mode=resident),
        ],
        out_specs=pl.BlockSpec((bm, bn), lambda i, j: (i, j)),
        compiler_params=_cparams(2),
        name="merge",
    )(ya, yb, yc, proj, proj, proj, w_pa, w_pb, w_pc)


def _outproj_kernel(m_ref, w_ref, h_ref, o_ref):
    o_ref[...] = h_ref[...] + jnp.dot(m_ref[...], w_ref[...], preferred_element_type=F32)


def _outproj(merged, w_out, h, layer, *, bm, bn):
    t, d = h.shape
    return pl.pallas_call(
        _outproj_kernel,
        out_shape=jax.ShapeDtypeStruct((t, d), F32),
        grid=(t // bm, d // bn),
        in_specs=[
            pl.BlockSpec((bm, merged.shape[1]), lambda i, j: (i, 0)),
            pl.BlockSpec((None, merged.shape[1], bn), lambda i, j: (layer, 0, j),
                         pipeline_mode=pl.Buffered(1) if bn == d else None),
            pl.BlockSpec((bm, bn), lambda i, j: (i, j)),
        ],
        out_specs=pl.BlockSpec((bm, bn), lambda i, j: (i, j)),
        compiler_params=_cparams(2),
        name="outproj",
    )(merged, w_out, h)


def _final_norm_kernel(x_ref, g_ref, o_ref):
    o_ref[...] = _rmsnorm_f32(x_ref[...], g_ref[...])


def _final_norm(h, g, *, bm):
    t, d = h.shape
    return pl.pallas_call(
        _final_norm_kernel,
        out_shape=jax.ShapeDtypeStruct((t, d), F32),
        grid=(t // bm,),
        in_specs=[pl.BlockSpec((bm, d), lambda i: (i, 0)), pl.BlockSpec((1, d), lambda i: (0, 0))],
        out_specs=pl.BlockSpec((bm, d), lambda i: (i, 0)),
        compiler_params=_cparams(1),
        name="final_norm",
    )(h, g)


def kernel(x, rel_bias, final_norm, ffn1_norm, ffn1_wi, ffn1_wo, mix_norm, w_in, sgu_ln_g, sgu_ln_b, sgu_w, sgu_b, diff_lambda, diff_subln, conv_w, conv_b, dt_bias, a_log, d_skip, ssm_norm, w_pa, w_pb, w_pc, w_out, ffn2_norm, ffn2_wi, ffn2_wo):
    batch, seq, d = x.shape
    depth = ffn1_wi.shape[0]
    t = batch * seq
    heads_c = dt_bias.shape[1]
    a_width = sgu_ln_g.shape[1]
    c_inner = ssm_norm.shape[1]
    conv_ch = conv_w.shape[2]
    d_ff = ffn1_wo.shape[1]

    bm = min(1024, t)
    bf = 512
    bq = 512
    bk = 512
    assert t % bm == 0 and d_ff % bf == 0 and seq % bk == 0 and bk % bq == 0 and bq % LANES == 0
    assert w_in.shape[2] == OFF_GATE + heads_c + 3 * d and conv_ch == OFF_AU

    o_au, o_av, o_q, o_z, o_xbc = 0, a_width, 2 * a_width, 5 * a_width, 5 * a_width + c_inner
    o_dt = o_xbc + conv_ch
    o_g = o_dt + heads_c
    w_main = jnp.concatenate(
        [w_in[:, :, o_xbc:o_dt], w_in[:, :, o_au:o_av], w_in[:, :, o_z:o_xbc], w_in[:, :, o_av:o_q],
         w_in[:, :, o_q:o_z], w_in[:, :, o_g:]], axis=2).astype(BF16)
    w_dt = jnp.pad(w_in[:, :, o_dt:o_g], ((0, 0), (0, 0), (0, LANES - heads_c))).astype(BF16)
    wi1, wo1 = ffn1_wi[0].astype(BF16), ffn1_wo[0].astype(BF16)
    wi2, wo2 = ffn2_wi[0].astype(BF16), ffn2_wo[0].astype(BF16)
    wpa, wpb, wpc, wout = w_pa.astype(BF16), w_pb.astype(BF16), w_pc.astype(BF16), w_out.astype(BF16)

    bsb = jnp.repeat(jnp.swapaxes(sgu_b, 1, 2), a_width // A_GROUPS, axis=2)
    pad_h = ((0, 0), (0, LANES - heads_c))
    dtb = jnp.pad(dt_bias, pad_h)
    alog = jnp.pad(a_log, pad_h)
    dsk_x = jnp.repeat(d_skip, C_HEAD_DIM, axis=1)
    expand = np.zeros((LANES, c_inner), np.float32)
    expand[np.arange(c_inner) // C_HEAD_DIM, np.arange(c_inner)] = 1.0
    expand = jnp.asarray(expand, BF16)
    bias_tiles = _bias_tiles(rel_bias, _bucket_tiles(bq, bk))
    subln_t = jnp.broadcast_to(diff_subln[:, :, None], diff_subln.shape + (bq,))

    h = x.reshape(t, d)
    for l in range(depth):
        lam_init = 0.8 - 0.6 * math.exp(-0.3 * l)
        nxt = min(l + 1, depth - 1)
        h, wi1, wo1 = _ffn(h, ffn1_norm[l][None], wi1, wo1, ffn1_wi, ffn1_wo, nxt, bm=bm, bf=bf)
        proj, dt_raw = _inproj(h, mix_norm[l][None], w_main, w_dt, l, bm=bm, bn=2048)
        ya = _sgu(proj, sgu_ln_g[l][None], sgu_ln_b[l][None], sgu_w[l], bsb[l], rows=4 * CHUNK)
        yb = _attn(proj, bias_tiles, diff_lambda[l], subln_t[l],
                   batch=batch, seq=seq, bq=bq, hpb=2, lam_init=lam_init)
        yc = _mamba(proj, dt_raw, conv_w[l], conv_b[l][None], dtb[l][None], alog[l][None],
                    dsk_x[l][None], ssm_norm[l][None], expand, batch=batch, seq=seq, rows=4 * CHUNK)
        merged = _merge(ya, yb, yc, proj, wpa, wpb, wpc, l, bm=bm // 2, bn=d)
        h = _outproj(merged, wout, h, l, bm=bm, bn=d)
        h, wi2, wo2 = _ffn(h, ffn2_norm[l][None], wi2, wo2, ffn2_wi, ffn2_wo, nxt, bm=bm, bf=bf)
    return _final_norm(h, final_norm[None], bm=bm).reshape(batch, seq, d)
```

```python
import functools
import math

import numpy as np
import jax
import jax.numpy as jnp
from jax import lax
from jax.experimental import pallas as pl
from jax.experimental.pallas import tpu as pltpu

F32 = jnp.float32
BF16 = jnp.bfloat16

EPS = 1e-6
CHUNK = 128
LANES = 128
BF16_ROWS = 16
A_GROUPS = 8
B_HEAD_DIM = 64
C_HEAD_DIM = 64
C_GROUPS = 4
C_STATE = 128
C_CONV = 4
REL_BUCKETS = 32
REL_MAX_DIST = 128
REL_EXACT = REL_BUCKETS // 2
NEG_BIG = -1e30
LOG2E = math.log2(math.e)
VMEM_LIMIT_V7X = 56 * 1024 * 1024
BIG_VMEM_LIMIT_V7X = 60 * 1024 * 1024

OFF_AU, OFF_AV, OFF_Q, OFF_K, OFF_V, OFF_Z, OFF_XBC, OFF_GATE = 0, 1024, 2048, 3072, 4096, 5120, 7168, 10240
SPLIT_W = 1024


def _cparams(n_axes, vmem_limit=VMEM_LIMIT_V7X):
    return pltpu.CompilerParams(dimension_semantics=("arbitrary",) * n_axes, vmem_limit_bytes=vmem_limit)


def _rmsnorm_f32(x, g):
    ms = jnp.mean(x * x, axis=-1, keepdims=True)
    return (x * lax.rsqrt(ms + EPS)) * g


def _ffn_kernel(x_ref, g_ref, wg_ref, wu_ref, wo_ref, wi_src_ref, wo_src_ref, o_ref, wi_cast_ref, wo_cast_ref,
                xn_ref):
    j = pl.program_id(1)

    @pl.when(j == 0)
    def _():
        x = x_ref[...]
        xn_ref[...] = _rmsnorm_f32(x, g_ref[...]).astype(BF16)
        o_ref[...] = x

    xn = xn_ref[...]
    gate = jnp.dot(xn, wg_ref[...], preferred_element_type=F32)
    up = jnp.dot(xn, wu_ref[...], preferred_element_type=F32)
    act = ((0.5 * gate) * jax.nn.sigmoid(gate) * up).astype(BF16)
    o_ref[...] += jnp.dot(act, wo_ref[...], preferred_element_type=F32)
    wi_cast_ref[...] = wi_src_ref[...].astype(BF16)
    wo_cast_ref[...] = wo_src_ref[...].astype(BF16)


def _ffn(h, g, wi, wo, wi_f32, wo_f32, cast_layer, *, bm, bf):
    t, d = h.shape
    d_ff = wo.shape[0]
    ni, nj = t // bm, d_ff // bf
    wi_tile = (d // ni, 2 * d_ff // nj)
    wo_tile = (d_ff // nj, d // ni)
    assert d % ni == 0 and wi_tile[0] % BF16_ROWS == 0 and wi_tile[1] % LANES == 0 and wo_tile[1] % LANES == 0
    return pl.pallas_call(
        _ffn_kernel,
        out_shape=(jax.ShapeDtypeStruct((t, d), F32), jax.ShapeDtypeStruct(wi.shape, BF16),
                   jax.ShapeDtypeStruct(wo.shape, BF16)),
        grid=(ni, nj),
        in_specs=[
            pl.BlockSpec((bm, d), lambda i, j: (i, 0)),
            pl.BlockSpec((1, d), lambda i, j: (0, 0)),
            pl.BlockSpec((d, bf), lambda i, j: (0, j)),
            pl.BlockSpec((d, bf), lambda i, j: (0, j + nj)),
            pl.BlockSpec((bf, d), lambda i, j: (j, 0)),
            pl.BlockSpec((None,) + wi_tile, lambda i, j: (cast_layer, i, j)),
            pl.BlockSpec((None,) + wo_tile, lambda i, j: (cast_layer, j, i)),
        ],
        out_specs=(pl.BlockSpec((bm, d), lambda i, j: (i, 0)),
                   pl.BlockSpec(wi_tile, lambda i, j: (i, j)),
                   pl.BlockSpec(wo_tile, lambda i, j: (j, i))),
        scratch_shapes=[pltpu.VMEM((bm, d), BF16)],
        compiler_params=_cparams(2, vmem_limit=BIG_VMEM_LIMIT_V7X),
        name="ffn",
    )(h, g, wi, wi, wo, wi_f32, wo_f32)


def _inproj_kernel(x_ref, g_ref, w_ref, wdt_ref, o_ref, dt_ref, xn_ref, *, nj_plain):
    j = pl.program_id(1)
    nt = (((1,), (1,)), ((), ()))

    @pl.when(j == 0)
    def _():
        xn = _rmsnorm_f32(x_ref[...], g_ref[...]).astype(BF16)
        xn_ref[...] = xn
        dt_ref[...] = lax.dot_general(xn, wdt_ref[...], nt, preferred_element_type=F32)


    @pl.when(j < nj_plain)
    def _():
        o_ref[...] = lax.dot_general(xn_ref[...], w_ref[0], nt, preferred_element_type=F32).astype(BF16)

    @pl.when(j >= nj_plain)
    def _():
        o_ref[...] = jax.nn.sigmoid(
            lax.dot_general(xn_ref[...], w_ref[0], nt, preferred_element_type=F32)).astype(BF16)


def _inproj(h, g, w_t, layer, *, bm, bn, n_dt):
    t, d = h.shape
    n = w_t.shape[1] - n_dt
    nj_plain = OFF_GATE // bn
    assert OFF_GATE % bn == 0 and OFF_GATE % LANES == 0 and n % bn == 0 and n_dt <= LANES and n_dt % BF16_ROWS == 0
    return pl.pallas_call(
        functools.partial(_inproj_kernel, nj_plain=nj_plain),
        out_shape=(jax.ShapeDtypeStruct((t, n), BF16), jax.ShapeDtypeStruct((t, LANES), F32)),
        grid=(t // bm, n // bn),
        in_specs=[
            pl.BlockSpec((bm, d), lambda i, j: (i, 0)),
            pl.BlockSpec((1, d), lambda i, j: (0, 0)),
            pl.BlockSpec((pl.Element(1), pl.Element(bn), pl.Element(d)),
                         lambda i, j: (layer, pl.multiple_of(j * bn + jnp.where(j >= nj_plain, n_dt, 0),
                                                             BF16_ROWS), 0)),
            pl.BlockSpec((None, LANES, d), lambda i, j: (layer, OFF_GATE // LANES, 0)),
        ],
        out_specs=(pl.BlockSpec((bm, bn), lambda i, j: (i, j)),
                   pl.BlockSpec((bm, LANES), lambda i, j: (i, 0))),
        scratch_shapes=[pltpu.VMEM((bm, d), BF16)],
        compiler_params=_cparams(2),
        name="inproj",
    )(h, g, w_t, w_t)


def _sgu_kernel(u_ref, v_ref, lng_ref, lnb_ref, w_ref, bsb_ref, o_ref):
    rows = u_ref.shape[0]
    u = jax.nn.gelu(u_ref[...].astype(F32))
    v = jax.nn.gelu(v_ref[...].astype(F32))
    mu = jnp.mean(v, axis=-1, keepdims=True)
    var = jnp.mean(jnp.square(v - mu), axis=-1, keepdims=True)
    vn = (((v - mu) * lax.rsqrt(var + EPS)) * lng_ref[...] + lnb_ref[...]).astype(BF16)
    r_i = lax.broadcasted_iota(jnp.int32, (CHUNK, CHUNK), 0)
    c_i = lax.broadcasted_iota(jnp.int32, (CHUNK, CHUNK), 1)
    tri = r_i >= c_i
    for g in range(A_GROUPS):
        wm = jnp.where(tri, w_ref[g], 0.0).astype(BF16)
        cs = slice(g * LANES, (g + 1) * LANES)
        for c in range(rows // CHUNK):
            rs = slice(c * CHUNK, (c + 1) * CHUNK)
            mixed = jnp.dot(wm, vn[rs, cs], preferred_element_type=F32) + bsb_ref[:, cs]
            o_ref[rs, cs] = (u[rs, cs] * mixed).astype(BF16)


def _sgu(proj, ln_g, ln_b, w_s, bsb, *, rows):
    t = proj.shape[0]
    aw = ln_g.shape[1]
    return pl.pallas_call(
        _sgu_kernel,
        out_shape=jax.ShapeDtypeStruct((t, aw), BF16),
        grid=(t // rows,),
        in_specs=[
            pl.BlockSpec((rows, aw), lambda i: (i, OFF_AU // aw)),
            pl.BlockSpec((rows, aw), lambda i: (i, OFF_AV // aw)),
            pl.BlockSpec((1, aw), lambda i: (0, 0)),
            pl.BlockSpec((1, aw), lambda i: (0, 0)),
            pl.BlockSpec((A_GROUPS, CHUNK, CHUNK), lambda i: (0, 0, 0)),
            pl.BlockSpec((CHUNK, aw), lambda i: (0, 0)),
        ],
        out_specs=pl.BlockSpec((rows, aw), lambda i: (i, 0)),
        compiler_params=_cparams(1),
        name="sgu",
    )(proj, proj, ln_g, ln_b, w_s, bsb)


def _t5_bucket_np(dist):
    n = np.maximum(dist, 0)
    nf = np.maximum(n, 1).astype(np.float64)
    large = REL_EXACT + (np.log(nf / REL_EXACT) / math.log(REL_MAX_DIST / REL_EXACT)
                         * (REL_BUCKETS - REL_EXACT)).astype(np.int32)
    large = np.minimum(large, REL_BUCKETS - 1)
    return np.where(n < REL_EXACT, n, large).astype(np.int32)


def _bucket_tiles(bq, bk):
    last_bucket_from = int(np.argmax(_t5_bucket_np(np.arange(4 * REL_MAX_DIST)) == REL_BUCKETS - 1))
    n_tiles = -(-(bk - 1 + last_bucket_from) // bq)
    c = np.arange(bk)[:, None]
    r = np.arange(bq)[None, :]
    tiles = []
    for t in range(n_tiles):
        dist = t * bq + r - c
        tiles.append(np.where(dist >= 0, _t5_bucket_np(dist), REL_BUCKETS))
    assert _t5_bucket_np(np.array([n_tiles * bq - (bk - 1)]))[0] == REL_BUCKETS - 1
    return np.stack(tiles).astype(np.int32)


BIAS_ROWS = 64


def _bias_tiles_kernel(rb_ref, bkt_ref, o_ref, *, present):
    hd = pl.program_id(0)
    n_bias, bk, bq = o_ref.shape
    for t in range(n_bias - 1):
        for rc in range(bk // BIAS_ROWS):
            rs = slice(rc * BIAS_ROWS, (rc + 1) * BIAS_ROWS)
            bt = bkt_ref[t, rs, :]
            tile = jnp.where(bt == REL_BUCKETS, NEG_BIG, 0.0).astype(F32)
            for b in present[t][rc]:
                tile = jnp.where(bt == b, rb_ref[b, hd] * LOG2E, tile)
            o_ref[t, rs, :] = tile
    o_ref[n_bias - 1] = jnp.full((bk, bq), rb_ref[REL_BUCKETS - 1, hd] * LOG2E, F32)


def _bias_tiles(rel_bias, buckets_np):
    n_tiles, bk, bq = buckets_np.shape
    heads = rel_bias.shape[1]
    present = tuple(tuple(tuple(int(b) for b in np.unique(buckets_np[t, rc * BIAS_ROWS:(rc + 1) * BIAS_ROWS])
                                if b < REL_BUCKETS)
                          for rc in range(bk // BIAS_ROWS)) for t in range(n_tiles))
    return pl.pallas_call(
        functools.partial(_bias_tiles_kernel, present=present),
        out_shape=jax.ShapeDtypeStruct((heads, n_tiles + 1, bk, bq), F32),
        grid=(heads,),
        in_specs=[pl.BlockSpec(memory_space=pltpu.SMEM),
                  pl.BlockSpec((n_tiles, bk, bq), lambda h: (0, 0, 0))],
        out_specs=pl.BlockSpec((None, n_tiles + 1, bk, bq), lambda h: (h, 0, 0, 0)),
        compiler_params=_cparams(1),
        name="bias_tiles",
    )(rel_bias, jnp.asarray(buckets_np))


def _attn_kernel(lam_ref, sub_ref, bias_ref, q_ref, k_ref, v_ref, o_ref,
                 vt_ref, qqt_ref, s_ref, p_ref, cm_ref, m_ref, al_ref, acc_ref, *, lam_init):
    hw = 2 * B_HEAD_DIM
    hpb, bk = s_ref.shape[0], s_ref.shape[1]
    bq = s_ref.shape[2] // 2
    seq = k_ref.shape[0]
    nq = seq // bq
    n_bias = bias_ref.shape[1]
    cols = [slice(hh * hw, (hh + 1) * hw) for hh in range(hpb)]
    last_block = lambda i: (i * bq) // bk
    n_pairs = sum(last_block(i) + 1 for i in range(nq))

    pad_row = lax.broadcasted_iota(jnp.int32, (vt_ref.shape[2] - hw, bk), 0)
    ones_row = jnp.where(pad_row == 0, 1.0, 0.0).astype(BF16)
    dim = lax.broadcasted_iota(jnp.int32, (hw, bq), 0)
    for hh in range(hpb):
        for c in range(seq // LANES):
            blk, off = divmod(c * LANES, bk)
            vt_ref[hh, blk, 0:hw, off:off + LANES] = (
                v_ref[c * LANES:(c + 1) * LANES, cols[hh]].astype(F32).T.astype(BF16))
        for blk in range(seq // bk):
            vt_ref[hh, blk, hw:, :] = ones_row
        for i in range(nq):
            qt = (q_ref[i * bq:(i + 1) * bq, cols[hh]].astype(F32) * (B_HEAD_DIM ** -0.5 * LOG2E)).T
            qqt_ref[hh, i] = jnp.concatenate([jnp.where(dim < B_HEAD_DIM, qt, 0.0),
                                              jnp.where(dim >= B_HEAD_DIM, qt, 0.0)], axis=1).astype(BF16)
    m_ref[...] = jnp.full(m_ref.shape, NEG_BIG, F32)
    al_ref[...] = jnp.ones(al_ref.shape, F32)
    acc_ref[...] = jnp.zeros(acc_ref.shape, F32)
    p_ref[...] = jnp.zeros(p_ref.shape, BF16)

    def scores(hh, i, j):
        kb = k_ref[pl.ds(pl.multiple_of(j * bk, bk), bk), cols[hh]]
        s = jnp.dot(kb, qqt_ref[hh, i], preferred_element_type=F32)
        bias = bias_ref[hh, jnp.minimum((i * bq - j * bk) // bq, n_bias - 1)]
        s = s + jnp.concatenate([bias, bias], axis=1)
        s_ref[hh] = s
        cm_ref[hh] = jnp.max(s, axis=0, keepdims=True)

    def probabilities(hh, i):
        m_prev = m_ref[hh, i]
        m_next = jnp.maximum(m_prev, cm_ref[hh])
        m_ref[hh, i] = m_next
        al_ref[hh] = jnp.exp2(m_prev - m_next)
        p_ref[hh] = jnp.exp2(s_ref[hh] - m_next).astype(BF16)

    def weighted_values(hh, i, j):
        acc_ref[hh, i] = al_ref[hh] * acc_ref[hh, i] + jnp.dot(vt_ref[hh, j], p_ref[hh],
                                                                preferred_element_type=F32)

    for hh in range(hpb):
        scores(hh, 0, 0)

    def body(t, carry):
        ip, jp, ic, jc = carry
        wrap = jc >= last_block(ic)
        i_next = jnp.where(wrap, ic + 1, ic)
        j_next = jnp.where(wrap, 0, jc + 1)
        for hh in range(hpb):
            weighted_values(hh, ip, jp)
        for hh in range(hpb):
            probabilities(hh, ic)
        for hh in range(hpb):
            scores(hh, i_next, j_next)
        return ic, jc, i_next, j_next

    zero = jnp.int32(0)
    ip, jp, ic, jc = lax.fori_loop(0, n_pairs - 1, body, (zero, zero, zero, zero))
    for hh in range(hpb):
        weighted_values(hh, ip, jp)
        probabilities(hh, ic)
    for hh in range(hpb):
        weighted_values(hh, ic, jc)

    lam_p = lam_ref[...]
    lam = (jnp.exp(jnp.sum(lam_p[0:1] * lam_p[1:2], axis=1, keepdims=True))
           - jnp.exp(jnp.sum(lam_p[2:3] * lam_p[3:4], axis=1, keepdims=True)) + lam_init)
    for hh in range(hpb):
        for i in range(nq):
            o = acc_ref[hh, i, 0:hw, :] * (1.0 / acc_ref[hh, i, hw:hw + 1, :])
            attn = o[:, :bq] - lam * o[:, bq:]
            ms = jnp.mean(attn * attn, axis=0, keepdims=True)
            y = ((attn * lax.rsqrt(ms + EPS)) * sub_ref[...]) * (1.0 - lam_init)
            o_ref[i * bq:(i + 1) * bq, cols[hh]] = y.T.astype(BF16)


def _attn(proj, bias_tiles, lam_p, subln_t, *, batch, seq, bq, hpb, lam_init):
    t = proj.shape[0]
    hw = 2 * B_HEAD_DIM
    heads, n_bias, bk, _ = bias_tiles.shape
    nq = seq // bq
    bw = hpb * hw
    return pl.pallas_call(
        functools.partial(_attn_kernel, lam_init=lam_init),
        out_shape=jax.ShapeDtypeStruct((t, heads * hw), BF16),
        grid=(batch, heads // hpb),
        in_specs=[
            pl.BlockSpec((4, B_HEAD_DIM), lambda b, h: (0, 0)),
            pl.BlockSpec((hw, bq), lambda b, h: (0, 0)),
            pl.BlockSpec((hpb, n_bias, bk, bq), lambda b, h: (h, 0, 0, 0)),
            pl.BlockSpec((seq, bw), lambda b, h: (b, OFF_Q // bw + h)),
            pl.BlockSpec((seq, bw), lambda b, h: (b, OFF_K // bw + h)),
            pl.BlockSpec((seq, bw), lambda b, h: (b, OFF_V // bw + h)),
        ],
        out_specs=pl.BlockSpec((seq, bw), lambda b, h: (b, h)),
        scratch_shapes=[pltpu.VMEM((hpb, seq // bk, hw + BF16_ROWS, bk), BF16),
                        pltpu.VMEM((hpb, nq, hw, 2 * bq), BF16),
                        pltpu.VMEM((hpb, bk, 2 * bq), F32),
                        pltpu.VMEM((hpb, bk, 2 * bq), BF16),
                        pltpu.VMEM((hpb, 1, 2 * bq), F32),
                        pltpu.VMEM((hpb, nq, 1, 2 * bq), F32),
                        pltpu.VMEM((hpb, 1, 2 * bq), F32),
                        pltpu.VMEM((hpb, nq, hw + BF16_ROWS, 2 * bq), F32)],
        compiler_params=_cparams(2, vmem_limit=BIG_VMEM_LIMIT_V7X),
        name="diff_attn",
    )(lam_p, subln_t, bias_tiles, proj, proj, proj)


def _split_dot(x, e_bf16, passes):
    out = None
    r = x
    for _ in range(passes):
        hi = r.astype(BF16)
        part = jnp.dot(hi, e_bf16, preferred_element_type=F32)
        out = part if out is None else out + part
        r = r - hi.astype(F32)
    return out


def _mamba_kernel(*refs, n_z, n_xbc):
    z_refs, xbc_refs = refs[:n_z], refs[n_z:n_z + n_xbc]
    (dt_ref, cw_ref, cb_ref, dtb_ref, alog_ref, dsk_ref, ng_ref, e_ref, o_ref, tail_ref, st_ref) = refs[n_z + n_xbc:]
    z_rows = lambda sl: jnp.concatenate([r[sl, :] for r in z_refs], axis=1)
    xbc_rows = lambda sl: jnp.concatenate([r[sl, :] for r in xbc_refs], axis=1)
    rows, c_inner = o_ref.shape
    conv_ch = cw_ref.shape[1]
    heads = c_inner // C_HEAD_DIM
    gw = c_inner // C_GROUPS
    heads_per_group = gw // C_HEAD_DIM
    c = pl.program_id(1)

    @pl.when(c == 0)
    def _():
        tail_ref[...] = jnp.zeros_like(tail_ref)
        st_ref[...] = jnp.zeros_like(st_ref)

    r_i = lax.broadcasted_iota(jnp.int32, (CHUNK, CHUNK), 0)
    c_i = lax.broadcasted_iota(jnp.int32, (CHUNK, CHUNK), 1)
    tri = r_i >= c_i
    lower = tri.astype(F32)
    upper = (r_i <= c_i).astype(F32)
    shift = jnp.concatenate([(r_i - c_i == d).astype(BF16) for d in range(C_CONV - 1, 0, -1)], axis=0)
    row8 = lax.broadcasted_iota(jnp.int32, (8, conv_ch), 0)
    is_head = lax.broadcasted_iota(jnp.int32, (CHUNK, LANES), 1) < heads
    lane_lo = lax.broadcasted_iota(jnp.int32, (CHUNK, LANES), 1) < C_HEAD_DIM
    neg_a = -jnp.exp(alog_ref[...]) * LOG2E
    e = e_ref[...]
    hp = lax.Precision.HIGHEST
    prev = st_ref[...]

    for u in range(rows // CHUNK):
        rs = slice(u * CHUNK, (u + 1) * CHUNK)
        x_cur = xbc_rows(rs)
        delayed = jnp.dot(shift, x_cur, preferred_element_type=F32)
        if u == 0:
            tail = tail_ref[8:16, :]
        else:
            tail = xbc_rows(slice(u * CHUNK - 16, u * CHUNK)).astype(F32)[8:16, :]
        conv = cb_ref[...] + cw_ref[C_CONV - 1:C_CONV, :] * x_cur.astype(F32)
        conv_top = jnp.zeros_like(tail)
        for k in range(C_CONV - 1):
            d = C_CONV - 1 - k
            conv = conv + cw_ref[k:k + 1, :] * delayed[k * CHUNK:(k + 1) * CHUNK]
            conv_top = conv_top + cw_ref[k:k + 1, :] * jnp.where(row8 < d, pltpu.roll(tail, d, 0), 0.0)
        conv = jnp.concatenate([conv[0:8] + conv_top, conv[8:]], axis=0)
        xbc = conv * jax.nn.sigmoid(conv)
        xs = xbc[:, :c_inner]
        bm = xbc[:, c_inner:c_inner + C_GROUPS * C_STATE]
        cm = xbc[:, c_inner + C_GROUPS * C_STATE:]

        dt = jnp.where(is_head, jax.nn.softplus(dt_ref[rs, :] + dtb_ref[...]), 0.0)
        a = dt * neg_a
        a_cs = jnp.dot(lower, a, precision=hp, preferred_element_type=F32)
        a_cs_t = jnp.dot(a.T, upper, precision=hp, preferred_element_type=F32)

        dt_x = _split_dot(dt, e, 2)
        acs_x = _split_dot(a_cs, e, 3)
        last_x = acs_x[CHUNK - 1:CHUNK, :]
        xd = xs * dt_x
        xd_b = xd.astype(BF16)
        xdd_b = (xd * jnp.exp2(last_x - acs_x)).astype(BF16)
        prev_b = prev.astype(BF16)
        eacs = jnp.exp2(acs_x)

        y_parts = []
        st_parts = []
        for g in range(C_GROUPS):
            gs = slice(g * gw, (g + 1) * gw)
            b_g = bm[:, g * C_STATE:(g + 1) * C_STATE]
            c_g = cm[:, g * C_STATE:(g + 1) * C_STATE].astype(BF16)
            cb = lax.dot_general(c_g, b_g.astype(BF16), (((1,), (1,)), ((), ())), preferred_element_type=F32)
            st_parts.append(jnp.dot(b_g.T.astype(BF16), xdd_b[:, gs], preferred_element_type=F32))
            y_off = jnp.dot(c_g, prev_b[:, gs], preferred_element_type=F32) * eacs[:, gs]
            tiles = []
            for pr in range(heads_per_group // 2):
                t_idx = g * (heads_per_group // 2) + pr
                xt = xd_b[:, t_idx * LANES:(t_idx + 1) * LANES]
                res = []
                for half in range(2):
                    hd = 2 * t_idx + half
                    seg = a_cs[:, hd:hd + 1] - a_cs_t[hd:hd + 1, :]
                    decay = jnp.exp2(jnp.where(tri, seg, NEG_BIG))
                    res.append(jnp.dot((cb * decay).astype(BF16), xt, preferred_element_type=F32))
                tiles.append(jnp.where(lane_lo, res[0], res[1]))
            y_parts.append(jnp.concatenate(tiles, axis=1) + y_off)

        prev = prev * jnp.exp2(last_x) + jnp.concatenate(st_parts, axis=1)
        y = jnp.concatenate(y_parts, axis=1) + dsk_ref[...] * xs
        zf = z_rows(rs).astype(F32)
        gated = y * (zf * jax.nn.sigmoid(zf))
        o_ref[rs, :] = _rmsnorm_f32(gated, ng_ref[...]).astype(BF16)

    st_ref[...] = prev
    tail_ref[...] = xbc_rows(slice(rows - 16, rows)).astype(F32)


def _mamba(proj, dt_raw, conv_w, conv_b, dt_bias, a_log, d_skip_x, norm_g, expand, *, batch, seq, rows):
    t = proj.shape[0]
    c_inner = norm_g.shape[1]
    conv_ch = conv_w.shape[1]
    nc = seq // rows
    row = lambda b, c: (b * nc + c, 0)
    const = lambda b, c: (0, 0)
    n_z, n_xbc = c_inner // SPLIT_W, conv_ch // SPLIT_W
    col_block = lambda k: pl.BlockSpec((rows, SPLIT_W), lambda b, c: (b * nc + c, k))
    return pl.pallas_call(
        functools.partial(_mamba_kernel, n_z=n_z, n_xbc=n_xbc),
        out_shape=jax.ShapeDtypeStruct((t, c_inner), BF16),
        grid=(batch, nc),
        in_specs=[col_block(OFF_Z // SPLIT_W + k) for k in range(n_z)]
        + [col_block(OFF_XBC // SPLIT_W + k) for k in range(n_xbc)] + [
            pl.BlockSpec((rows, LANES), row),
            pl.BlockSpec((C_CONV, conv_ch), const),
            pl.BlockSpec((1, conv_ch), const),
            pl.BlockSpec((1, LANES), const),
            pl.BlockSpec((1, LANES), const),
            pl.BlockSpec((1, c_inner), const),
            pl.BlockSpec((1, c_inner), const),
            pl.BlockSpec((LANES, c_inner), const),
        ],
        out_specs=pl.BlockSpec((rows, c_inner), row),
        scratch_shapes=[pltpu.VMEM((16, conv_ch), F32), pltpu.VMEM((C_STATE, c_inner), F32)],
        compiler_params=_cparams(2),
        name="mamba_ssd",
    )(*([proj] * (n_z + n_xbc)), dt_raw, conv_w, conv_b, dt_bias, a_log, d_skip_x, norm_g, expand)


def _merge_kernel(ya_ref, yb_ref, yc_ref, g0_ref, g1_ref, g2_ref, wa_ref, wb_ref, wc_ref, o_ref):
    pa = jnp.dot(ya_ref[...], wa_ref[...], preferred_element_type=F32)
    pb = jnp.dot(yb_ref[...], wb_ref[...], preferred_element_type=F32)
    pc = jnp.dot(yc_ref[...], wc_ref[...], preferred_element_type=F32)
    o_ref[...] = (g0_ref[...].astype(F32) * pa + g1_ref[...].astype(F32) * pb
                  + g2_ref[...].astype(F32) * pc).astype(BF16)


def _merge(ya, yb, yc, proj, w_pa, w_pb, w_pc, layer, *, bm, bn):
    t = ya.shape[0]
    d = w_pa.shape[2]
    gate_spec = lambda k: pl.BlockSpec((bm, bn), lambda i, j: (i, (OFF_GATE + k * d) // bn + j))
    resident = pl.Buffered(1) if bn == d else None
    return pl.pallas_call(
        _merge_kernel,
        out_shape=jax.ShapeDtypeStruct((t, d), BF16),
        grid=(t // bm, d // bn),
        in_specs=[
            pl.BlockSpec((bm, ya.shape[1]), lambda i, j: (i, 0)),
            pl.BlockSpec((bm, yb.shape[1]), lambda i, j: (i, 0)),
            pl.BlockSpec((bm, yc.shape[1]), lambda i, j: (i, 0)),
            gate_spec(0), gate_spec(1), gate_spec(2),
            pl.BlockSpec((None, w_pa.shape[1], bn), lambda i, j: (layer, 0, j), pipeline_mode=resident),
            pl.BlockSpec((None, w_pb.shape[1], bn), lambda i, j: (layer, 0, j), pipeline_mode=resident),
            pl.BlockSpec((None, w_pc.shape[1], bn), lambda i, j: (layer, 0, j), pipeline_mode=resident),
        ],
        out_specs=pl.BlockSpec((bm, bn), lambda i, j: (i, j)),
        compiler_params=_cparams(2),
        name="merge",
    )(ya, yb, yc, proj, proj, proj, w_pa, w_pb, w_pc)


def _outproj_kernel(m_ref, w_ref, h_ref, o_ref):
    o_ref[...] = h_ref[...] + jnp.dot(m_ref[...], w_ref[...], preferred_element_type=F32)


def _outproj(merged, w_out, h, layer, *, bm, bn):
    t, d = h.shape
    return pl.pallas_call(
        _outproj_kernel,
        out_shape=jax.ShapeDtypeStruct((t, d), F32),
        grid=(t // bm, d // bn),
        in_specs=[
            pl.BlockSpec((bm, merged.shape[1]), lambda i, j: (i, 0)),
            pl.BlockSpec((None, merged.shape[1], bn), lambda i, j: (layer, 0, j),
                         pipeline_mode=pl.Buffered(1) if bn == d else None),
            pl.BlockSpec((bm, bn), lambda i, j: (i, j)),
        ],
        out_specs=pl.BlockSpec((bm, bn), lambda i, j: (i, j)),
        compiler_params=_cparams(2),
        name="outproj",
    )(merged, w_out, h)


def _final_norm_kernel(x_ref, g_ref, o_ref):
    o_ref[...] = _rmsnorm_f32(x_ref[...], g_ref[...])


def _final_norm(h, g, *, bm):
    t, d = h.shape
    return pl.pallas_call(
        _final_norm_kernel,
        out_shape=jax.ShapeDtypeStruct((t, d), F32),
        grid=(t // bm,),
        in_specs=[pl.BlockSpec((bm, d), lambda i: (i, 0)), pl.BlockSpec((1, d), lambda i: (0, 0))],
        out_specs=pl.BlockSpec((bm, d), lambda i: (i, 0)),
        compiler_params=_cparams(1),
        name="final_norm",
    )(h, g)


def kernel(x, rel_bias, final_norm, ffn1_norm, ffn1_wi, ffn1_wo, mix_norm, w_in, sgu_ln_g, sgu_ln_b, sgu_w, sgu_b, diff_lambda, diff_subln, conv_w, conv_b, dt_bias, a_log, d_skip, ssm_norm, w_pa, w_pb, w_pc, w_out, ffn2_norm, ffn2_wi, ffn2_wo):
    batch, seq, d = x.shape
    depth = ffn1_wi.shape[0]
    t = batch * seq
    heads_c = dt_bias.shape[1]
    a_width = sgu_ln_g.shape[1]
    c_inner = ssm_norm.shape[1]
    conv_ch = conv_w.shape[2]
    d_ff = ffn1_wo.shape[1]

    bm = min(1024, t)
    bf = 512
    bq = 512
    bk = 512
    assert t % bm == 0 and d_ff % bf == 0 and seq % bk == 0 and bk % bq == 0 and bq % LANES == 0
    assert w_in.shape[2] == OFF_GATE + heads_c + 3 * d
    assert (OFF_AV, OFF_Q, OFF_Z, OFF_XBC, OFF_GATE) == (a_width, 2 * a_width, 5 * a_width, 5 * a_width + c_inner,
                                                         5 * a_width + c_inner + conv_ch)

    w_t = jnp.swapaxes(w_in, 1, 2).astype(BF16)
    wi1, wo1 = ffn1_wi[0].astype(BF16), ffn1_wo[0].astype(BF16)
    wi2, wo2 = ffn2_wi[0].astype(BF16), ffn2_wo[0].astype(BF16)
    wpa, wpb, wpc, wout = w_pa.astype(BF16), w_pb.astype(BF16), w_pc.astype(BF16), w_out.astype(BF16)

    bsb = jnp.repeat(jnp.swapaxes(sgu_b, 1, 2), a_width // A_GROUPS, axis=2)
    pad_h = ((0, 0), (0, LANES - heads_c))
    dtb = jnp.pad(dt_bias, pad_h)
    alog = jnp.pad(a_log, pad_h)
    dsk_x = jnp.repeat(d_skip, C_HEAD_DIM, axis=1)
    expand = np.zeros((LANES, c_inner), np.float32)
    expand[np.arange(c_inner) // C_HEAD_DIM, np.arange(c_inner)] = 1.0
    expand = jnp.asarray(expand, BF16)
    bias_tiles = _bias_tiles(rel_bias, _bucket_tiles(bq, bk))
    subln_t = jnp.broadcast_to(diff_subln[:, :, None], diff_subln.shape + (bq,))

    h = x.reshape(t, d)
    for l in range(depth):
        lam_init = 0.8 - 0.6 * math.exp(-0.3 * l)
        nxt = min(l + 1, depth - 1)
        h, wi1, wo1 = _ffn(h, ffn1_norm[l][None], wi1, wo1, ffn1_wi, ffn1_wo, nxt, bm=bm, bf=bf)
        proj, dt_raw = _inproj(h, mix_norm[l][None], w_t, l, bm=bm, bn=2048, n_dt=heads_c)
        ya = _sgu(proj, sgu_ln_g[l][None], sgu_ln_b[l][None], sgu_w[l], bsb[l], rows=4 * CHUNK)
        yb = _attn(proj, bias_tiles, diff_lambda[l], subln_t[l],
                   batch=batch, seq=seq, bq=bq, hpb=2, lam_init=lam_init)
        yc = _mamba(proj, dt_raw, conv_w[l], conv_b[l][None], dtb[l][None], alog[l][None],
                    dsk_x[l][None], ssm_norm[l][None], expand, batch=batch, seq=seq, rows=4 * CHUNK)
        merged = _merge(ya, yb, yc, proj, wpa, wpb, wpc, l, bm=bm // 2, bn=d)
        h = _outproj(merged, wout, h, l, bm=bm, bn=d)
        h, wi2, wo2 = _ffn(h, ffn2_norm[l][None], wi2, wo2, ffn2_wi, ffn2_wo, nxt, bm=bm, bf=bf)
    return _final_norm(h, final_norm[None], bm=bm).reshape(batch, seq, d)
```

```python
import functools
import math

import numpy as np
import jax
import jax.numpy as jnp
from jax import lax
from jax.experimental import pallas as pl
from jax.experimental.pallas import tpu as pltpu

F32 = jnp.float32
BF16 = jnp.bfloat16

EPS = 1e-6
CHUNK = 128
LANES = 128
BF16_ROWS = 16
A_GROUPS = 8
B_HEAD_DIM = 64
C_HEAD_DIM = 64
C_GROUPS = 4
C_STATE = 128
C_CONV = 4
REL_BUCKETS = 32
REL_MAX_DIST = 128
REL_EXACT = REL_BUCKETS // 2
NEG_BIG = -1e30
LOG2E = math.log2(math.e)
VMEM_LIMIT_V7X = 56 * 1024 * 1024
BIG_VMEM_LIMIT_V7X = 60 * 1024 * 1024

OFF_AU, OFF_AV, OFF_Q, OFF_K, OFF_V, OFF_Z, OFF_XBC, OFF_GATE = 0, 1024, 2048, 3072, 4096, 5120, 7168, 10240
SPLIT_W = 1024


def _cparams(n_axes, vmem_limit=VMEM_LIMIT_V7X):
    return pltpu.CompilerParams(dimension_semantics=("arbitrary",) * n_axes, vmem_limit_bytes=vmem_limit)


def _rmsnorm_f32(x, g):
    ms = jnp.mean(x * x, axis=-1, keepdims=True)
    return (x * lax.rsqrt(ms + EPS)) * g


def _ffn_kernel(x_ref, g_ref, wg_ref, wu_ref, wo_ref, wi_src_ref, wo_src_ref, o_ref, wi_cast_ref, wo_cast_ref,
                xn_ref):
    j = pl.program_id(1)

    @pl.when(j == 0)
    def _():
        x = x_ref[...]
        xn_ref[...] = _rmsnorm_f32(x, g_ref[...]).astype(BF16)
        o_ref[...] = x

    xn = xn_ref[...]
    gate = jnp.dot(xn, wg_ref[...], preferred_element_type=F32)
    up = jnp.dot(xn, wu_ref[...], preferred_element_type=F32)
    act = ((0.5 * gate) * jax.nn.sigmoid(gate) * up).astype(BF16)
    o_ref[...] += jnp.dot(act, wo_ref[...], preferred_element_type=F32)
    wi_cast_ref[...] = wi_src_ref[...].astype(BF16)
    wo_cast_ref[...] = wo_src_ref[...].astype(BF16)


def _ffn(h, g, wi, wo, wi_f32, wo_f32, cast_layer, *, bm, bf):
    t, d = h.shape
    d_ff = wo.shape[0]
    ni, nj = t // bm, d_ff // bf
    wi_tile = (d // ni, 2 * d_ff // nj)
    wo_tile = (d_ff // nj, d // ni)
    assert d % ni == 0 and wi_tile[0] % BF16_ROWS == 0 and wi_tile[1] % LANES == 0 and wo_tile[1] % LANES == 0
    return pl.pallas_call(
        _ffn_kernel,
        out_shape=(jax.ShapeDtypeStruct((t, d), F32), jax.ShapeDtypeStruct(wi.shape, BF16),
                   jax.ShapeDtypeStruct(wo.shape, BF16)),
        grid=(ni, nj),
        in_specs=[
            pl.BlockSpec((bm, d), lambda i, j: (i, 0)),
            pl.BlockSpec((1, d), lambda i, j: (0, 0)),
            pl.BlockSpec((d, bf), lambda i, j: (0, j)),
            pl.BlockSpec((d, bf), lambda i, j: (0, j + nj)),
            pl.BlockSpec((bf, d), lambda i, j: (j, 0)),
            pl.BlockSpec((None,) + wi_tile, lambda i, j: (cast_layer, i, j)),
            pl.BlockSpec((None,) + wo_tile, lambda i, j: (cast_layer, j, i)),
        ],
        out_specs=(pl.BlockSpec((bm, d), lambda i, j: (i, 0)),
                   pl.BlockSpec(wi_tile, lambda i, j: (i, j)),
                   pl.BlockSpec(wo_tile, lambda i, j: (j, i))),
        scratch_shapes=[pltpu.VMEM((bm, d), BF16)],
        compiler_params=_cparams(2, vmem_limit=BIG_VMEM_LIMIT_V7X),
        name="ffn",
    )(h, g, wi, wi, wo, wi_f32, wo_f32)


def _inproj_kernel(x_ref, g_ref, w_ref, wdt_ref, w_src_ref, o_ref, dt_ref, w_cast_ref, xn_ref, *, nj_plain):
    j = pl.program_id(1)
    nt = (((1,), (1,)), ((), ()))

    @pl.when(j == 0)
    def _():
        xn = _rmsnorm_f32(x_ref[...], g_ref[...]).astype(BF16)
        xn_ref[...] = xn
        dt_ref[...] = lax.dot_general(xn, wdt_ref[...], nt, preferred_element_type=F32)

    @pl.when(j < nj_plain)
    def _():
        o_ref[...] = lax.dot_general(xn_ref[...], w_ref[...], nt, preferred_element_type=F32).astype(BF16)
        w_cast_ref[...] = w_src_ref[...].astype(BF16)

    @pl.when(j >= nj_plain)
    def _():
        o_ref[...] = jax.nn.sigmoid(
            lax.dot_general(xn_ref[...], w_ref[...], nt, preferred_element_type=F32)).astype(BF16)
        w_cast_ref[...] = w_src_ref[...].astype(BF16)


def _inproj(h, g, w_t, w_src, cast_layer, *, bm, bn, n_dt):
    t, d = h.shape
    n_in = w_t.shape[0]
    n = n_in - n_dt
    ni, nj = t // bm, n // bn
    nj_plain = OFF_GATE // bn
    assert OFF_GATE % bn == 0 and OFF_GATE % LANES == 0 and n % bn == 0 and n_dt <= LANES and n_dt % BF16_ROWS == 0
    n_cast = max(c for c in range(1, ni * nj + 1) if n_in % c == 0 and (n_in // c) % BF16_ROWS == 0)
    cast_tile = lambda i, j: jnp.minimum(i * nj + j, n_cast - 1)
    return pl.pallas_call(
        functools.partial(_inproj_kernel, nj_plain=nj_plain),
        out_shape=(jax.ShapeDtypeStruct((t, n), BF16), jax.ShapeDtypeStruct((t, LANES), F32),
                   jax.ShapeDtypeStruct((n_in, d), BF16)),
        grid=(ni, nj),
        in_specs=[
            pl.BlockSpec((bm, d), lambda i, j: (i, 0)),
            pl.BlockSpec((1, d), lambda i, j: (0, 0)),
            pl.BlockSpec((pl.Element(bn), pl.Element(d)),
                         lambda i, j: (pl.multiple_of(j * bn + jnp.where(j >= nj_plain, n_dt, 0), BF16_ROWS), 0)),
            pl.BlockSpec((LANES, d), lambda i, j: (OFF_GATE // LANES, 0)),
            pl.BlockSpec((None, n_in // n_cast, d), lambda i, j: (cast_layer, cast_tile(i, j), 0)),
        ],
        out_specs=(pl.BlockSpec((bm, bn), lambda i, j: (i, j)),
                   pl.BlockSpec((bm, LANES), lambda i, j: (i, 0)),
                   pl.BlockSpec((n_in // n_cast, d), lambda i, j: (cast_tile(i, j), 0))),
        scratch_shapes=[pltpu.VMEM((bm, d), BF16)],
        compiler_params=_cparams(2, vmem_limit=BIG_VMEM_LIMIT_V7X),
        name="inproj",
    )(h, g, w_t, w_t, w_src)


def _sgu_kernel(u_ref, v_ref, lng_ref, lnb_ref, w_ref, bsb_ref, o_ref):
    rows = u_ref.shape[0]
    u = jax.nn.gelu(u_ref[...].astype(F32))
    v = jax.nn.gelu(v_ref[...].astype(F32))
    mu = jnp.mean(v, axis=-1, keepdims=True)
    var = jnp.mean(jnp.square(v - mu), axis=-1, keepdims=True)
    vn = (((v - mu) * lax.rsqrt(var + EPS)) * lng_ref[...] + lnb_ref[...]).astype(BF16)
    r_i = lax.broadcasted_iota(jnp.int32, (CHUNK, CHUNK), 0)
    c_i = lax.broadcasted_iota(jnp.int32, (CHUNK, CHUNK), 1)
    tri = r_i >= c_i
    for g in range(A_GROUPS):
        wm = jnp.where(tri, w_ref[g], 0.0).astype(BF16)
        cs = slice(g * LANES, (g + 1) * LANES)
        for c in range(rows // CHUNK):
            rs = slice(c * CHUNK, (c + 1) * CHUNK)
            mixed = jnp.dot(wm, vn[rs, cs], preferred_element_type=F32) + bsb_ref[:, cs]
            o_ref[rs, cs] = (u[rs, cs] * mixed).astype(BF16)


def _sgu(proj, ln_g, ln_b, w_s, bsb, *, rows):
    t = proj.shape[0]
    aw = ln_g.shape[1]
    return pl.pallas_call(
        _sgu_kernel,
        out_shape=jax.ShapeDtypeStruct((t, aw), BF16),
        grid=(t // rows,),
        in_specs=[
            pl.BlockSpec((rows, aw), lambda i: (i, OFF_AU // aw)),
            pl.BlockSpec((rows, aw), lambda i: (i, OFF_AV // aw)),
            pl.BlockSpec((1, aw), lambda i: (0, 0)),
            pl.BlockSpec((1, aw), lambda i: (0, 0)),
            pl.BlockSpec((A_GROUPS, CHUNK, CHUNK), lambda i: (0, 0, 0)),
            pl.BlockSpec((CHUNK, aw), lambda i: (0, 0)),
        ],
        out_specs=pl.BlockSpec((rows, aw), lambda i: (i, 0)),
        compiler_params=_cparams(1),
        name="sgu",
    )(proj, proj, ln_g, ln_b, w_s, bsb)


def _t5_bucket_np(dist):
    n = np.maximum(dist, 0)
    nf = np.maximum(n, 1).astype(np.float64)
    large = REL_EXACT + (np.log(nf / REL_EXACT) / math.log(REL_MAX_DIST / REL_EXACT)
                         * (REL_BUCKETS - REL_EXACT)).astype(np.int32)
    large = np.minimum(large, REL_BUCKETS - 1)
    return np.where(n < REL_EXACT, n, large).astype(np.int32)


def _bucket_tiles(bq, bk):
    last_bucket_from = int(np.argmax(_t5_bucket_np(np.arange(4 * REL_MAX_DIST)) == REL_BUCKETS - 1))
    n_tiles = -(-(bk - 1 + last_bucket_from) // bq)
    c = np.arange(bk)[:, None]
    r = np.arange(bq)[None, :]
    tiles = []
    for t in range(n_tiles):
        dist = t * bq + r - c
        tiles.append(np.where(dist >= 0, _t5_bucket_np(dist), REL_BUCKETS))
    assert _t5_bucket_np(np.array([n_tiles * bq - (bk - 1)]))[0] == REL_BUCKETS - 1
    return np.stack(tiles).astype(np.int32)


BIAS_ROWS = 64


def _bias_tiles_kernel(rb_ref, bkt_ref, o_ref, *, present):
    hd = pl.program_id(0)
    n_bias, bk, bq = o_ref.shape
    for t in range(n_bias - 1):
        for rc in range(bk // BIAS_ROWS):
            rs = slice(rc * BIAS_ROWS, (rc + 1) * BIAS_ROWS)
            bt = bkt_ref[t, rs, :]
            tile = jnp.where(bt == REL_BUCKETS, NEG_BIG, 0.0).astype(F32)
            for b in present[t][rc]:
                tile = jnp.where(bt == b, rb_ref[b, hd] * LOG2E, tile)
            o_ref[t, rs, :] = tile
    o_ref[n_bias - 1] = jnp.full((bk, bq), rb_ref[REL_BUCKETS - 1, hd] * LOG2E, F32)


def _bias_tiles(rel_bias, buckets_np):
    n_tiles, bk, bq = buckets_np.shape
    heads = rel_bias.shape[1]
    present = tuple(tuple(tuple(int(b) for b in np.unique(buckets_np[t, rc * BIAS_ROWS:(rc + 1) * BIAS_ROWS])
                                if b < REL_BUCKETS)
                          for rc in range(bk // BIAS_ROWS)) for t in range(n_tiles))
    return pl.pallas_call(
        functools.partial(_bias_tiles_kernel, present=present),
        out_shape=jax.ShapeDtypeStruct((heads, n_tiles + 1, bk, bq), F32),
        grid=(heads,),
        in_specs=[pl.BlockSpec(memory_space=pltpu.SMEM),
                  pl.BlockSpec((n_tiles, bk, bq), lambda h: (0, 0, 0))],
        out_specs=pl.BlockSpec((None, n_tiles + 1, bk, bq), lambda h: (h, 0, 0, 0)),
        compiler_params=_cparams(1),
        name="bias_tiles",
    )(rel_bias, jnp.asarray(buckets_np))


def _attn_kernel(lam_ref, sub_ref, bias_ref, q_ref, k_ref, v_ref, o_ref,
                 vt_ref, qqt_ref, s_ref, p_ref, cm_ref, m_ref, al_ref, acc_ref, *, lam_init):
    hw = 2 * B_HEAD_DIM
    hpb, bk = s_ref.shape[0], s_ref.shape[1]
    bq = s_ref.shape[2] // 2
    seq = k_ref.shape[0]
    nq = seq // bq
    n_bias = bias_ref.shape[1]
    cols = [slice(hh * hw, (hh + 1) * hw) for hh in range(hpb)]
    last_block = lambda i: (i * bq) // bk
    n_pairs = sum(last_block(i) + 1 for i in range(nq))

    pad_row = lax.broadcasted_iota(jnp.int32, (vt_ref.shape[2] - hw, bk), 0)
    ones_row = jnp.where(pad_row == 0, 1.0, 0.0).astype(BF16)
    dim = lax.broadcasted_iota(jnp.int32, (hw, bq), 0)
    for hh in range(hpb):
        for c in range(seq // LANES):
            blk, off = divmod(c * LANES, bk)
            vt_ref[hh, blk, 0:hw, off:off + LANES] = (
                v_ref[c * LANES:(c + 1) * LANES, cols[hh]].astype(F32).T.astype(BF16))
        for blk in range(seq // bk):
            vt_ref[hh, blk, hw:, :] = ones_row
        for i in range(nq):
            qt = (q_ref[i * bq:(i + 1) * bq, cols[hh]].astype(F32) * (B_HEAD_DIM ** -0.5 * LOG2E)).T
            qqt_ref[hh, i] = jnp.concatenate([jnp.where(dim < B_HEAD_DIM, qt, 0.0),
                                              jnp.where(dim >= B_HEAD_DIM, qt, 0.0)], axis=1).astype(BF16)
    m_ref[...] = jnp.full(m_ref.shape, NEG_BIG, F32)
    al_ref[...] = jnp.ones(al_ref.shape, F32)
    acc_ref[...] = jnp.zeros(acc_ref.shape, F32)
    p_ref[...] = jnp.zeros(p_ref.shape, BF16)

    def scores(hh, i, j):
        kb = k_ref[pl.ds(pl.multiple_of(j * bk, bk), bk), cols[hh]]
        s = jnp.dot(kb, qqt_ref[hh, i], preferred_element_type=F32)
        bias = bias_ref[hh, jnp.minimum((i * bq - j * bk) // bq, n_bias - 1)]
        s = s + jnp.concatenate([bias, bias], axis=1)
        s_ref[hh] = s
        cm_ref[hh] = jnp.max(s, axis=0, keepdims=True)

    def probabilities(hh, i):
        m_prev = m_ref[hh, i]
        m_next = jnp.maximum(m_prev, cm_ref[hh])
        m_ref[hh, i] = m_next
        al_ref[hh] = jnp.exp2(m_prev - m_next)
        p_ref[hh] = jnp.exp2(s_ref[hh] - m_next).astype(BF16)

    def weighted_values(hh, i, j):
        acc_ref[hh, i] = al_ref[hh] * acc_ref[hh, i] + jnp.dot(vt_ref[hh, j], p_ref[hh],
                                                                preferred_element_type=F32)

    for hh in range(hpb):
        scores(hh, 0, 0)

    def body(t, carry):
        ip, jp, ic, jc = carry
        wrap = jc >= last_block(ic)
        i_next = jnp.where(wrap, ic + 1, ic)
        j_next = jnp.where(wrap, 0, jc + 1)
        for hh in range(hpb):
            weighted_values(hh, ip, jp)
        for hh in range(hpb):
            probabilities(hh, ic)
        for hh in range(hpb):
            scores(hh, i_next, j_next)
        return ic, jc, i_next, j_next

    zero = jnp.int32(0)
    ip, jp, ic, jc = lax.fori_loop(0, n_pairs - 1, body, (zero, zero, zero, zero))
    for hh in range(hpb):
        weighted_values(hh, ip, jp)
        probabilities(hh, ic)
    for hh in range(hpb):
        weighted_values(hh, ic, jc)

    lam_p = lam_ref[...]
    lam = (jnp.exp(jnp.sum(lam_p[0:1] * lam_p[1:2], axis=1, keepdims=True))
           - jnp.exp(jnp.sum(lam_p[2:3] * lam_p[3:4], axis=1, keepdims=True)) + lam_init)
    for hh in range(hpb):
        for i in range(nq):
            o = acc_ref[hh, i, 0:hw, :] * (1.0 / acc_ref[hh, i, hw:hw + 1, :])
            attn = o[:, :bq] - lam * o[:, bq:]
            ms = jnp.mean(attn * attn, axis=0, keepdims=True)
            y = ((attn * lax.rsqrt(ms + EPS)) * sub_ref[...]) * (1.0 - lam_init)
            o_ref[i * bq:(i + 1) * bq, cols[hh]] = y.T.astype(BF16)


def _attn(proj, bias_tiles, lam_p, subln_t, *, batch, seq, bq, hpb, lam_init):
    t = proj.shape[0]
    hw = 2 * B_HEAD_DIM
    heads, n_bias, bk, _ = bias_tiles.shape
    nq = seq // bq
    bw = hpb * hw
    return pl.pallas_call(
        functools.partial(_attn_kernel, lam_init=lam_init),
        out_shape=jax.ShapeDtypeStruct((t, heads * hw), BF16),
        grid=(batch, heads // hpb),
        in_specs=[
            pl.BlockSpec((4, B_HEAD_DIM), lambda b, h: (0, 0)),
            pl.BlockSpec((hw, bq), lambda b, h: (0, 0)),
            pl.BlockSpec((hpb, n_bias, bk, bq), lambda b, h: (h, 0, 0, 0)),
            pl.BlockSpec((seq, bw), lambda b, h: (b, OFF_Q // bw + h)),
            pl.BlockSpec((seq, bw), lambda b, h: (b, OFF_K // bw + h)),
            pl.BlockSpec((seq, bw), lambda b, h: (b, OFF_V // bw + h)),
        ],
        out_specs=pl.BlockSpec((seq, bw), lambda b, h: (b, h)),
        scratch_shapes=[pltpu.VMEM((hpb, seq // bk, hw + BF16_ROWS, bk), BF16),
                        pltpu.VMEM((hpb, nq, hw, 2 * bq), BF16),
                        pltpu.VMEM((hpb, bk, 2 * bq), F32),
                        pltpu.VMEM((hpb, bk, 2 * bq), BF16),
                        pltpu.VMEM((hpb, 1, 2 * bq), F32),
                        pltpu.VMEM((hpb, nq, 1, 2 * bq), F32),
                        pltpu.VMEM((hpb, 1, 2 * bq), F32),
                        pltpu.VMEM((hpb, nq, hw + BF16_ROWS, 2 * bq), F32)],
        compiler_params=_cparams(2, vmem_limit=BIG_VMEM_LIMIT_V7X),
        name="diff_attn",
    )(lam_p, subln_t, bias_tiles, proj, proj, proj)


def _split_dot(x, e_bf16, passes):
    out = None
    r = x
    for _ in range(passes):
        hi = r.astype(BF16)
        part = jnp.dot(hi, e_bf16, preferred_element_type=F32)
        out = part if out is None else out + part
        r = r - hi.astype(F32)
    return out


def _mamba_kernel(*refs, n_z, n_xbc):
    z_refs, xbc_refs = refs[:n_z], refs[n_z:n_z + n_xbc]
    (dt_ref, cw_ref, cb_ref, dtb_ref, alog_ref, dsk_ref, ng_ref, e_ref, o_ref, tail_ref, st_ref) = refs[n_z + n_xbc:]
    z_rows = lambda sl: jnp.concatenate([r[sl, :] for r in z_refs], axis=1)
    xbc_rows = lambda sl: jnp.concatenate([r[sl, :] for r in xbc_refs], axis=1)
    rows, c_inner = o_ref.shape
    conv_ch = cw_ref.shape[1]
    heads = c_inner // C_HEAD_DIM
    gw = c_inner // C_GROUPS
    heads_per_group = gw // C_HEAD_DIM
    c = pl.program_id(1)

    @pl.when(c == 0)
    def _():
        tail_ref[...] = jnp.zeros_like(tail_ref)
        st_ref[...] = jnp.zeros_like(st_ref)

    r_i = lax.broadcasted_iota(jnp.int32, (CHUNK, CHUNK), 0)
    c_i = lax.broadcasted_iota(jnp.int32, (CHUNK, CHUNK), 1)
    tri = r_i >= c_i
    lower = tri.astype(F32)
    upper = (r_i <= c_i).astype(F32)
    shift = jnp.concatenate([(r_i - c_i == d).astype(BF16) for d in range(C_CONV - 1, 0, -1)], axis=0)
    row8 = lax.broadcasted_iota(jnp.int32, (8, conv_ch), 0)
    is_head = lax.broadcasted_iota(jnp.int32, (CHUNK, LANES), 1) < heads
    lane_lo = lax.broadcasted_iota(jnp.int32, (CHUNK, LANES), 1) < C_HEAD_DIM
    neg_a = -jnp.exp(alog_ref[...]) * LOG2E
    e = e_ref[...]
    hp = lax.Precision.HIGHEST
    prev = st_ref[...]

    for u in range(rows // CHUNK):
        rs = slice(u * CHUNK, (u + 1) * CHUNK)
        x_cur = xbc_rows(rs)
        delayed = jnp.dot(shift, x_cur, preferred_element_type=F32)
        if u == 0:
            tail = tail_ref[8:16, :]
        else:
            tail = xbc_rows(slice(u * CHUNK - 16, u * CHUNK)).astype(F32)[8:16, :]
        conv = cb_ref[...] + cw_ref[C_CONV - 1:C_CONV, :] * x_cur.astype(F32)
        conv_top = jnp.zeros_like(tail)
        for k in range(C_CONV - 1):
            d = C_CONV - 1 - k
            conv = conv + cw_ref[k:k + 1, :] * delayed[k * CHUNK:(k + 1) * CHUNK]
            conv_top = conv_top + cw_ref[k:k + 1, :] * jnp.where(row8 < d, pltpu.roll(tail, d, 0), 0.0)
        conv = jnp.concatenate([conv[0:8] + conv_top, conv[8:]], axis=0)
        xbc = conv * jax.nn.sigmoid(conv)
        xs = xbc[:, :c_inner]
        bm = xbc[:, c_inner:c_inner + C_GROUPS * C_STATE]
        cm = xbc[:, c_inner + C_GROUPS * C_STATE:]

        dt = jnp.where(is_head, jax.nn.softplus(dt_ref[rs, :] + dtb_ref[...]), 0.0)
        a = dt * neg_a
        a_cs = jnp.dot(lower, a, precision=hp, preferred_element_type=F32)
        a_cs_t = jnp.dot(a.T, upper, precision=hp, preferred_element_type=F32)

        dt_x = _split_dot(dt, e, 2)
        acs_x = _split_dot(a_cs, e, 3)
        last_x = acs_x[CHUNK - 1:CHUNK, :]
        xd = xs * dt_x
        xd_b = xd.astype(BF16)
        xdd_b = (xd * jnp.exp2(last_x - acs_x)).astype(BF16)
        prev_b = prev.astype(BF16)
        eacs = jnp.exp2(acs_x)

        y_parts = []
        st_parts = []
        for g in range(C_GROUPS):
            gs = slice(g * gw, (g + 1) * gw)
            b_g = bm[:, g * C_STATE:(g + 1) * C_STATE]
            c_g = cm[:, g * C_STATE:(g + 1) * C_STATE].astype(BF16)
            cb = lax.dot_general(c_g, b_g.astype(BF16), (((1,), (1,)), ((), ())), preferred_element_type=F32)
            st_parts.append(jnp.dot(b_g.T.astype(BF16), xdd_b[:, gs], preferred_element_type=F32))
            y_off = jnp.dot(c_g, prev_b[:, gs], preferred_element_type=F32) * eacs[:, gs]
            tiles = []
            for pr in range(heads_per_group // 2):
                t_idx = g * (heads_per_group // 2) + pr
                xt = xd_b[:, t_idx * LANES:(t_idx + 1) * LANES]
                res = []
                for half in range(2):
                    hd = 2 * t_idx + half
                    seg = a_cs[:, hd:hd + 1] - a_cs_t[hd:hd + 1, :]
                    decay = jnp.exp2(jnp.where(tri, seg, NEG_BIG))
                    res.append(jnp.dot((cb * decay).astype(BF16), xt, preferred_element_type=F32))
                tiles.append(jnp.where(lane_lo, res[0], res[1]))
            y_parts.append(jnp.concatenate(tiles, axis=1) + y_off)

        prev = prev * jnp.exp2(last_x) + jnp.concatenate(st_parts, axis=1)
        y = jnp.concatenate(y_parts, axis=1) + dsk_ref[...] * xs
        zf = z_rows(rs).astype(F32)
        gated = y * (zf * jax.nn.sigmoid(zf))
        o_ref[rs, :] = _rmsnorm_f32(gated, ng_ref[...]).astype(BF16)

    st_ref[...] = prev
    tail_ref[...] = xbc_rows(slice(rows - 16, rows)).astype(F32)


def _mamba(proj, dt_raw, conv_w, conv_b, dt_bias, a_log, d_skip_x, norm_g, expand, *, batch, seq, rows):
    t = proj.shape[0]
    c_inner = norm_g.shape[1]
    conv_ch = conv_w.shape[1]
    nc = seq // rows
    row = lambda b, c: (b * nc + c, 0)
    const = lambda b, c: (0, 0)
    n_z, n_xbc = c_inner // SPLIT_W, conv_ch // SPLIT_W
    col_block = lambda k: pl.BlockSpec((rows, SPLIT_W), lambda b, c: (b * nc + c, k))
    return pl.pallas_call(
        functools.partial(_mamba_kernel, n_z=n_z, n_xbc=n_xbc),
        out_shape=jax.ShapeDtypeStruct((t, c_inner), BF16),
        grid=(batch, nc),
        in_specs=[col_block(OFF_Z // SPLIT_W + k) for k in range(n_z)]
        + [col_block(OFF_XBC // SPLIT_W + k) for k in range(n_xbc)] + [
            pl.BlockSpec((rows, LANES), row),
            pl.BlockSpec((C_CONV, conv_ch), const),
            pl.BlockSpec((1, conv_ch), const),
            pl.BlockSpec((1, LANES), const),
            pl.BlockSpec((1, LANES), const),
            pl.BlockSpec((1, c_inner), const),
            pl.BlockSpec((1, c_inner), const),
            pl.BlockSpec((LANES, c_inner), const),
        ],
        out_specs=pl.BlockSpec((rows, c_inner), row),
        scratch_shapes=[pltpu.VMEM((16, conv_ch), F32), pltpu.VMEM((C_STATE, c_inner), F32)],
        compiler_params=_cparams(2),
        name="mamba_ssd",
    )(*([proj] * (n_z + n_xbc)), dt_raw, conv_w, conv_b, dt_bias, a_log, d_skip_x, norm_g, expand)


def _merge_kernel(ya_ref, yb_ref, yc_ref, g0_ref, g1_ref, g2_ref, wa_ref, wb_ref, wc_ref, o_ref):
    pa = jnp.dot(ya_ref[...], wa_ref[...], preferred_element_type=F32)
    pb = jnp.dot(yb_ref[...], wb_ref[...], preferred_element_type=F32)
    pc = jnp.dot(yc_ref[...], wc_ref[...], preferred_element_type=F32)
    o_ref[...] = (g0_ref[...].astype(F32) * pa + g1_ref[...].astype(F32) * pb
                  + g2_ref[...].astype(F32) * pc).astype(BF16)


def _merge(ya, yb, yc, proj, w_pa, w_pb, w_pc, layer, *, bm, bn):
    t = ya.shape[0]
    d = w_pa.shape[2]
    gate_spec = lambda k: pl.BlockSpec((bm, bn), lambda i, j: (i, (OFF_GATE + k * d) // bn + j))
    resident = pl.Buffered(1) if bn == d else None
    return pl.pallas_call(
        _merge_kernel,
        out_shape=jax.ShapeDtypeStruct((t, d), BF16),
        grid=(t // bm, d // bn),
        in_specs=[
            pl.BlockSpec((bm, ya.shape[1]), lambda i, j: (i, 0)),
            pl.BlockSpec((bm, yb.shape[1]), lambda i, j: (i, 0)),
            pl.BlockSpec((bm, yc.shape[1]), lambda i, j: (i, 0)),
            gate_spec(0), gate_spec(1), gate_spec(2),
            pl.BlockSpec((None, w_pa.shape[1], bn), lambda i, j: (layer, 0, j), pipeline_mode=resident),
            pl.BlockSpec((None, w_pb.shape[1], bn), lambda i, j: (layer, 0, j), pipeline_mode=resident),
            pl.BlockSpec((None, w_pc.shape[1], bn), lambda i, j: (layer, 0, j), pipeline_mode=resident),
        ],
        out_specs=pl.BlockSpec((bm, bn), lambda i, j: (i, j)),
        compiler_params=_cparams(2),
        name="merge",
    )(ya, yb, yc, proj, proj, proj, w_pa, w_pb, w_pc)


def _outproj_kernel(m_ref, w_ref, h_ref, o_ref):
    o_ref[...] = h_ref[...] + jnp.dot(m_ref[...], w_ref[...], preferred_element_type=F32)


def _outproj(merged, w_out, h, layer, *, bm, bn):
    t, d = h.shape
    return pl.pallas_call(
        _outproj_kernel,
        out_shape=jax.ShapeDtypeStruct((t, d), F32),
        grid=(t // bm, d // bn),
        in_specs=[
            pl.BlockSpec((bm, merged.shape[1]), lambda i, j: (i, 0)),
            pl.BlockSpec((None, merged.shape[1], bn), lambda i, j: (layer, 0, j),
                         pipeline_mode=pl.Buffered(1) if bn == d else None),
            pl.BlockSpec((bm, bn), lambda i, j: (i, j)),
        ],
        out_specs=pl.BlockSpec((bm, bn), lambda i, j: (i, j)),
        compiler_params=_cparams(2),
        name="outproj",
    )(merged, w_out, h)


def _final_norm_kernel(x_ref, g_ref, o_ref):
    o_ref[...] = _rmsnorm_f32(x_ref[...], g_ref[...])


def _final_norm(h, g, *, bm):
    t, d = h.shape
    return pl.pallas_call(
        _final_norm_kernel,
        out_shape=jax.ShapeDtypeStruct((t, d), F32),
        grid=(t // bm,),
        in_specs=[pl.BlockSpec((bm, d), lambda i: (i, 0)), pl.BlockSpec((1, d), lambda i: (0, 0))],
        out_specs=pl.BlockSpec((bm, d), lambda i: (i, 0)),
        compiler_params=_cparams(1),
        name="final_norm",
    )(h, g)


def kernel(x, rel_bias, final_norm, ffn1_norm, ffn1_wi, ffn1_wo, mix_norm, w_in, sgu_ln_g, sgu_ln_b, sgu_w, sgu_b, diff_lambda, diff_subln, conv_w, conv_b, dt_bias, a_log, d_skip, ssm_norm, w_pa, w_pb, w_pc, w_out, ffn2_norm, ffn2_wi, ffn2_wo):
    batch, seq, d = x.shape
    depth = ffn1_wi.shape[0]
    t = batch * seq
    heads_c = dt_bias.shape[1]
    a_width = sgu_ln_g.shape[1]
    c_inner = ssm_norm.shape[1]
    conv_ch = conv_w.shape[2]
    d_ff = ffn1_wo.shape[1]

    bm = min(1024, t)
    bf = 512
    bq = 512
    bk = 512
    assert t % bm == 0 and d_ff % bf == 0 and seq % bk == 0 and bk % bq == 0 and bq % LANES == 0
    assert w_in.shape[2] == OFF_GATE + heads_c + 3 * d
    assert (OFF_AV, OFF_Q, OFF_Z, OFF_XBC, OFF_GATE) == (a_width, 2 * a_width, 5 * a_width, 5 * a_width + c_inner,
                                                         5 * a_width + c_inner + conv_ch)

    w_src = jnp.swapaxes(w_in, 1, 2)
    w_t = w_src[0].astype(BF16)
    wi1, wo1 = ffn1_wi[0].astype(BF16), ffn1_wo[0].astype(BF16)
    wi2, wo2 = ffn2_wi[0].astype(BF16), ffn2_wo[0].astype(BF16)
    wpa, wpb, wpc, wout = w_pa.astype(BF16), w_pb.astype(BF16), w_pc.astype(BF16), w_out.astype(BF16)

    bsb = jnp.repeat(jnp.swapaxes(sgu_b, 1, 2), a_width // A_GROUPS, axis=2)
    pad_h = ((0, 0), (0, LANES - heads_c))
    dtb = jnp.pad(dt_bias, pad_h)
    alog = jnp.pad(a_log, pad_h)
    dsk_x = jnp.repeat(d_skip, C_HEAD_DIM, axis=1)
    expand = np.zeros((LANES, c_inner), np.float32)
    expand[np.arange(c_inner) // C_HEAD_DIM, np.arange(c_inner)] = 1.0
    expand = jnp.asarray(expand, BF16)
    bias_tiles = _bias_tiles(rel_bias, _bucket_tiles(bq, bk))
    subln_t = jnp.broadcast_to(diff_subln[:, :, None], diff_subln.shape + (bq,))

    h = x.reshape(t, d)
    for l in range(depth):
        lam_init = 0.8 - 0.6 * math.exp(-0.3 * l)
        nxt = min(l + 1, depth - 1)
        h, wi1, wo1 = _ffn(h, ffn1_norm[l][None], wi1, wo1, ffn1_wi, ffn1_wo, nxt, bm=bm, bf=bf)
        proj, dt_raw, w_t = _inproj(h, mix_norm[l][None], w_t, w_src, nxt, bm=bm, bn=2048, n_dt=heads_c)
        ya = _sgu(proj, sgu_ln_g[l][None], sgu_ln_b[l][None], sgu_w[l], bsb[l], rows=4 * CHUNK)
        yb = _attn(proj, bias_tiles, diff_lambda[l], subln_t[l],
                   batch=batch, seq=seq, bq=bq, hpb=2, lam_init=lam_init)
        yc = _mamba(proj, dt_raw, conv_w[l], conv_b[l][None], dtb[l][None], alog[l][None],
                    dsk_x[l][None], ssm_norm[l][None], expand, batch=batch, seq=seq, rows=4 * CHUNK)
        merged = _merge(ya, yb, yc, proj, wpa, wpb, wpc, l, bm=bm // 2, bn=d)
        h = _outproj(merged, wout, h, l, bm=bm, bn=d)
        h, wi2, wo2 = _ffn(h, ffn2_norm[l][None], wi2, wo2, ffn2_wi, ffn2_wo, nxt, bm=bm, bf=bf)
    return _final_norm(h, final_norm[None], bm=bm).reshape(batch, seq, d)
```

```python
import functools
import math

import numpy as np
import jax
import jax.numpy as jnp
from jax import lax
from jax.experimental import pallas as pl
from jax.experimental.pallas import tpu as pltpu

F32 = jnp.float32
BF16 = jnp.bfloat16

EPS = 1e-6
CHUNK = 128
LANES = 128
BF16_ROWS = 16
A_GROUPS = 8
B_HEAD_DIM = 64
C_HEAD_DIM = 64
C_GROUPS = 4
C_STATE = 128
C_CONV = 4
REL_BUCKETS = 32
REL_MAX_DIST = 128
REL_EXACT = REL_BUCKETS // 2
NEG_BIG = -1e30
LOG2E = math.log2(math.e)
VMEM_LIMIT_V7X = 56 * 1024 * 1024
BIG_VMEM_LIMIT_V7X = 60 * 1024 * 1024

OFF_AU, OFF_AV, OFF_Q, OFF_K, OFF_V, OFF_Z, OFF_XBC, OFF_GATE = 0, 1024, 2048, 3072, 4096, 5120, 7168, 10240
SPLIT_W = 1024


def _cparams(n_axes, vmem_limit=VMEM_LIMIT_V7X):
    return pltpu.CompilerParams(dimension_semantics=("arbitrary",) * n_axes, vmem_limit_bytes=vmem_limit)


def _rmsnorm_f32(x, g):
    ms = jnp.mean(x * x, axis=-1, keepdims=True)
    return (x * lax.rsqrt(ms + EPS)) * g


def _ffn_kernel(x_ref, g_ref, wg_ref, wu_ref, wo_ref, wi_src_ref, wo_src_ref, *rest, final):
    final_g_ref = rest[0] if final else None
    o_ref, wi_cast_ref, wo_cast_ref, xn_ref = rest[-4:]
    j = pl.program_id(1)

    @pl.when(j == 0)
    def _():
        x = x_ref[...]
        xn_ref[...] = _rmsnorm_f32(x, g_ref[...]).astype(BF16)
        o_ref[...] = x

    xn = xn_ref[...]
    gate = jnp.dot(xn, wg_ref[...], preferred_element_type=F32)
    up = jnp.dot(xn, wu_ref[...], preferred_element_type=F32)
    act = ((0.5 * gate) * jax.nn.sigmoid(gate) * up).astype(BF16)
    o_ref[...] += jnp.dot(act, wo_ref[...], preferred_element_type=F32)
    wi_cast_ref[...] = wi_src_ref[...].astype(BF16)
    wo_cast_ref[...] = wo_src_ref[...].astype(BF16)

    if final:
        @pl.when(j == pl.num_programs(1) - 1)
        def _():
            o_ref[...] = _rmsnorm_f32(o_ref[...], final_g_ref[...])


def _ffn(h, g, wi, wo, wi_f32, wo_f32, cast_layer, final_g=None, *, bm, bf):
    t, d = h.shape
    d_ff = wo.shape[0]
    ni, nj = t // bm, d_ff // bf
    wi_tile = (d // ni, 2 * d_ff // nj)
    wo_tile = (d_ff // nj, d // ni)
    assert d % ni == 0 and wi_tile[0] % BF16_ROWS == 0 and wi_tile[1] % LANES == 0 and wo_tile[1] % LANES == 0
    final = final_g is not None
    return pl.pallas_call(
        functools.partial(_ffn_kernel, final=final),
        out_shape=(jax.ShapeDtypeStruct((t, d), F32), jax.ShapeDtypeStruct(wi.shape, BF16),
                   jax.ShapeDtypeStruct(wo.shape, BF16)),
        grid=(ni, nj),
        in_specs=[
            pl.BlockSpec((bm, d), lambda i, j: (i, 0)),
            pl.BlockSpec((1, d), lambda i, j: (0, 0)),
            pl.BlockSpec((d, bf), lambda i, j: (0, j)),
            pl.BlockSpec((d, bf), lambda i, j: (0, j + nj)),
            pl.BlockSpec((bf, d), lambda i, j: (j, 0)),
            pl.BlockSpec((None,) + wi_tile, lambda i, j: (cast_layer, i, j)),
            pl.BlockSpec((None,) + wo_tile, lambda i, j: (cast_layer, j, i)),
        ] + ([pl.BlockSpec((1, d), lambda i, j: (0, 0))] if final else []),
        out_specs=(pl.BlockSpec((bm, d), lambda i, j: (i, 0)),
                   pl.BlockSpec(wi_tile, lambda i, j: (i, j)),
                   pl.BlockSpec(wo_tile, lambda i, j: (j, i))),
        scratch_shapes=[pltpu.VMEM((bm, d), BF16)],
        compiler_params=_cparams(2, vmem_limit=BIG_VMEM_LIMIT_V7X),
        name="ffn",
    )(h, g, wi, wi, wo, wi_f32, wo_f32, *([final_g] if final else []))


def _inproj_kernel(x_ref, g_ref, w_ref, wdt_ref, w_src_ref, o_ref, dt_ref, w_cast_ref, xn_ref, *, nj_plain):
    j = pl.program_id(1)
    nt = (((1,), (1,)), ((), ()))

    @pl.when(j == 0)
    def _():
        xn = _rmsnorm_f32(x_ref[...], g_ref[...]).astype(BF16)
        xn_ref[...] = xn
        dt_ref[...] = lax.dot_general(xn, wdt_ref[...], nt, preferred_element_type=F32)

    @pl.when(j < nj_plain)
    def _():
        o_ref[...] = lax.dot_general(xn_ref[...], w_ref[...], nt, preferred_element_type=F32).astype(BF16)
        w_cast_ref[...] = w_src_ref[...].astype(BF16)

    @pl.when(j >= nj_plain)
    def _():
        o_ref[...] = jax.nn.sigmoid(
            lax.dot_general(xn_ref[...], w_ref[...], nt, preferred_element_type=F32)).astype(BF16)
        w_cast_ref[...] = w_src_ref[...].astype(BF16)


def _inproj(h, g, w_t, w_src, cast_layer, *, bm, bn, n_dt):
    t, d = h.shape
    n_in = w_t.shape[0]
    n = n_in - n_dt
    ni, nj = t // bm, n // bn
    nj_plain = OFF_GATE // bn
    assert OFF_GATE % bn == 0 and OFF_GATE % LANES == 0 and n % bn == 0 and n_dt <= LANES and n_dt % BF16_ROWS == 0
    n_cast = max(c for c in range(1, ni * nj + 1) if n_in % c == 0 and (n_in // c) % BF16_ROWS == 0)
    cast_tile = lambda i, j: jnp.minimum(i * nj + j, n_cast - 1)
    return pl.pallas_call(
        functools.partial(_inproj_kernel, nj_plain=nj_plain),
        out_shape=(jax.ShapeDtypeStruct((t, n), BF16), jax.ShapeDtypeStruct((t, LANES), F32),
                   jax.ShapeDtypeStruct((n_in, d), BF16)),
        grid=(ni, nj),
        in_specs=[
            pl.BlockSpec((bm, d), lambda i, j: (i, 0)),
            pl.BlockSpec((1, d), lambda i, j: (0, 0)),
            pl.BlockSpec((pl.Element(bn), pl.Element(d)),
                         lambda i, j: (pl.multiple_of(j * bn + jnp.where(j >= nj_plain, n_dt, 0), BF16_ROWS), 0)),
            pl.BlockSpec((LANES, d), lambda i, j: (OFF_GATE // LANES, 0)),
            pl.BlockSpec((None, n_in // n_cast, d), lambda i, j: (cast_layer, cast_tile(i, j), 0)),
        ],
        out_specs=(pl.BlockSpec((bm, bn), lambda i, j: (i, j)),
                   pl.BlockSpec((bm, LANES), lambda i, j: (i, 0)),
                   pl.BlockSpec((n_in // n_cast, d), lambda i, j: (cast_tile(i, j), 0))),
        scratch_shapes=[pltpu.VMEM((bm, d), BF16)],
        compiler_params=_cparams(2, vmem_limit=BIG_VMEM_LIMIT_V7X),
        name="inproj",
    )(h, g, w_t, w_t, w_src)


def _sgu_kernel(u_ref, v_ref, lng_ref, lnb_ref, w_ref, bsb_ref, o_ref):
    rows = u_ref.shape[0]
    u = jax.nn.gelu(u_ref[...].astype(F32))
    v = jax.nn.gelu(v_ref[...].astype(F32))
    mu = jnp.mean(v, axis=-1, keepdims=True)
    var = jnp.mean(jnp.square(v - mu), axis=-1, keepdims=True)
    vn = (((v - mu) * lax.rsqrt(var + EPS)) * lng_ref[...] + lnb_ref[...]).astype(BF16)
    r_i = lax.broadcasted_iota(jnp.int32, (CHUNK, CHUNK), 0)
    c_i = lax.broadcasted_iota(jnp.int32, (CHUNK, CHUNK), 1)
    tri = r_i >= c_i
    for g in range(A_GROUPS):
        wm = jnp.where(tri, w_ref[g], 0.0).astype(BF16)
        cs = slice(g * LANES, (g + 1) * LANES)
        for c in range(rows // CHUNK):
            rs = slice(c * CHUNK, (c + 1) * CHUNK)
            mixed = jnp.dot(wm, vn[rs, cs], preferred_element_type=F32) + bsb_ref[:, cs]
            o_ref[rs, cs] = (u[rs, cs] * mixed).astype(BF16)


def _sgu(proj, ln_g, ln_b, w_s, bsb, *, rows):
    t = proj.shape[0]
    aw = ln_g.shape[1]
    return pl.pallas_call(
        _sgu_kernel,
        out_shape=jax.ShapeDtypeStruct((t, aw), BF16),
        grid=(t // rows,),
        in_specs=[
            pl.BlockSpec((rows, aw), lambda i: (i, OFF_AU // aw)),
            pl.BlockSpec((rows, aw), lambda i: (i, OFF_AV // aw)),
            pl.BlockSpec((1, aw), lambda i: (0, 0)),
            pl.BlockSpec((1, aw), lambda i: (0, 0)),
            pl.BlockSpec((A_GROUPS, CHUNK, CHUNK), lambda i: (0, 0, 0)),
            pl.BlockSpec((CHUNK, aw), lambda i: (0, 0)),
        ],
        out_specs=pl.BlockSpec((rows, aw), lambda i: (i, 0)),
        compiler_params=_cparams(1),
        name="sgu",
    )(proj, proj, ln_g, ln_b, w_s, bsb)


def _t5_bucket_np(dist):
    n = np.maximum(dist, 0)
    nf = np.maximum(n, 1).astype(np.float64)
    large = REL_EXACT + (np.log(nf / REL_EXACT) / math.log(REL_MAX_DIST / REL_EXACT)
                         * (REL_BUCKETS - REL_EXACT)).astype(np.int32)
    large = np.minimum(large, REL_BUCKETS - 1)
    return np.where(n < REL_EXACT, n, large).astype(np.int32)


def _bucket_tiles(bq, bk):
    last_bucket_from = int(np.argmax(_t5_bucket_np(np.arange(4 * REL_MAX_DIST)) == REL_BUCKETS - 1))
    n_tiles = -(-(bk - 1 + last_bucket_from) // bq)
    c = np.arange(bk)[:, None]
    r = np.arange(bq)[None, :]
    tiles = []
    for t in range(n_tiles):
        dist = t * bq + r - c
        tiles.append(np.where(dist >= 0, _t5_bucket_np(dist), REL_BUCKETS))
    assert _t5_bucket_np(np.array([n_tiles * bq - (bk - 1)]))[0] == REL_BUCKETS - 1
    return np.stack(tiles).astype(np.int32)


BIAS_ROWS = 64


def _bias_tiles_kernel(rb_ref, bkt_ref, o_ref, *, present):
    hd = pl.program_id(0)
    n_bias, bk, bq = o_ref.shape
    for t in range(n_bias - 1):
        for rc in range(bk // BIAS_ROWS):
            rs = slice(rc * BIAS_ROWS, (rc + 1) * BIAS_ROWS)
            bt = bkt_ref[t, rs, :]
            tile = jnp.where(bt == REL_BUCKETS, NEG_BIG, 0.0).astype(F32)
            for b in present[t][rc]:
                tile = jnp.where(bt == b, rb_ref[b, hd] * LOG2E, tile)
            o_ref[t, rs, :] = tile
    o_ref[n_bias - 1] = jnp.full((bk, bq), rb_ref[REL_BUCKETS - 1, hd] * LOG2E, F32)


def _bias_tiles(rel_bias, buckets_np):
    n_tiles, bk, bq = buckets_np.shape
    heads = rel_bias.shape[1]
    present = tuple(tuple(tuple(int(b) for b in np.unique(buckets_np[t, rc * BIAS_ROWS:(rc + 1) * BIAS_ROWS])
                                if b < REL_BUCKETS)
                          for rc in range(bk // BIAS_ROWS)) for t in range(n_tiles))
    return pl.pallas_call(
        functools.partial(_bias_tiles_kernel, present=present),
        out_shape=jax.ShapeDtypeStruct((heads, n_tiles + 1, bk, bq), F32),
        grid=(heads,),
        in_specs=[pl.BlockSpec(memory_space=pltpu.SMEM),
                  pl.BlockSpec((n_tiles, bk, bq), lambda h: (0, 0, 0))],
        out_specs=pl.BlockSpec((None, n_tiles + 1, bk, bq), lambda h: (h, 0, 0, 0)),
        compiler_params=_cparams(1),
        name="bias_tiles",
    )(rel_bias, jnp.asarray(buckets_np))


def _attn_kernel(lam_ref, sub_ref, bias_ref, q_ref, k_ref, v_ref, o_ref,
                 vt_ref, qqt_ref, s_ref, p_ref, cm_ref, m_ref, al_ref, acc_ref, *, lam_init):
    hw = 2 * B_HEAD_DIM
    hpb, bk = s_ref.shape[0], s_ref.shape[1]
    bq = s_ref.shape[2] // 2
    seq = k_ref.shape[0]
    nq = seq // bq
    n_bias = bias_ref.shape[1]
    cols = [slice(hh * hw, (hh + 1) * hw) for hh in range(hpb)]
    last_block = lambda i: (i * bq) // bk
    n_pairs = sum(last_block(i) + 1 for i in range(nq))

    pad_row = lax.broadcasted_iota(jnp.int32, (vt_ref.shape[2] - hw, bk), 0)
    ones_row = jnp.where(pad_row == 0, 1.0, 0.0).astype(BF16)
    dim = lax.broadcasted_iota(jnp.int32, (hw, bq), 0)
    for hh in range(hpb):
        for c in range(seq // LANES):
            blk, off = divmod(c * LANES, bk)
            vt_ref[hh, blk, 0:hw, off:off + LANES] = (
                v_ref[c * LANES:(c + 1) * LANES, cols[hh]].astype(F32).T.astype(BF16))
        for blk in range(seq // bk):
            vt_ref[hh, blk, hw:, :] = ones_row
        for i in range(nq):
            qt = (q_ref[i * bq:(i + 1) * bq, cols[hh]].astype(F32) * (B_HEAD_DIM ** -0.5 * LOG2E)).T
            qqt_ref[hh, i] = jnp.concatenate([jnp.where(dim < B_HEAD_DIM, qt, 0.0),
                                              jnp.where(dim >= B_HEAD_DIM, qt, 0.0)], axis=1).astype(BF16)
    m_ref[...] = jnp.full(m_ref.shape, NEG_BIG, F32)
    al_ref[...] = jnp.ones(al_ref.shape, F32)
    acc_ref[...] = jnp.zeros(acc_ref.shape, F32)
    p_ref[...] = jnp.zeros(p_ref.shape, BF16)

    def scores(hh, i, j):
        kb = k_ref[pl.ds(pl.multiple_of(j * bk, bk), bk), cols[hh]]
        s = jnp.dot(kb, qqt_ref[hh, i], preferred_element_type=F32)
        bias = bias_ref[hh, jnp.minimum((i * bq - j * bk) // bq, n_bias - 1)]
        s = s + jnp.concatenate([bias, bias], axis=1)
        s_ref[hh] = s
        cm_ref[hh] = jnp.max(s, axis=0, keepdims=True)

    def probabilities(hh, i):
        m_prev = m_ref[hh, i]
        m_next = jnp.maximum(m_prev, cm_ref[hh])
        m_ref[hh, i] = m_next
        al_ref[hh] = jnp.exp2(m_prev - m_next)
        p_ref[hh] = jnp.exp2(s_ref[hh] - m_next).astype(BF16)

    def weighted_values(hh, i, j):
        acc_ref[hh, i] = al_ref[hh] * acc_ref[hh, i] + jnp.dot(vt_ref[hh, j], p_ref[hh],
                                                                preferred_element_type=F32)

    for hh in range(hpb):
        scores(hh, 0, 0)

    def body(t, carry):
        ip, jp, ic, jc = carry
        wrap = jc >= last_block(ic)
        i_next = jnp.where(wrap, ic + 1, ic)
        j_next = jnp.where(wrap, 0, jc + 1)
        for hh in range(hpb):
            weighted_values(hh, ip, jp)
        for hh in range(hpb):
            probabilities(hh, ic)
        for hh in range(hpb):
            scores(hh, i_next, j_next)
        return ic, jc, i_next, j_next

    zero = jnp.int32(0)
    ip, jp, ic, jc = lax.fori_loop(0, n_pairs - 1, body, (zero, zero, zero, zero))
    for hh in range(hpb):
        weighted_values(hh, ip, jp)
        probabilities(hh, ic)
    for hh in range(hpb):
        weighted_values(hh, ic, jc)

    lam_p = lam_ref[...]
    lam = (jnp.exp(jnp.sum(lam_p[0:1] * lam_p[1:2], axis=1, keepdims=True))
           - jnp.exp(jnp.sum(lam_p[2:3] * lam_p[3:4], axis=1, keepdims=True)) + lam_init)
    for hh in range(hpb):
        for i in range(nq):
            o = acc_ref[hh, i, 0:hw, :] * (1.0 / acc_ref[hh, i, hw:hw + 1, :])
            attn = o[:, :bq] - lam * o[:, bq:]
            ms = jnp.mean(attn * attn, axis=0, keepdims=True)
            y = ((attn * lax.rsqrt(ms + EPS)) * sub_ref[...]) * (1.0 - lam_init)
            o_ref[i * bq:(i + 1) * bq, cols[hh]] = y.T.astype(BF16)


def _attn(proj, bias_tiles, lam_p, subln_t, *, batch, seq, bq, hpb, lam_init):
    t = proj.shape[0]
    hw = 2 * B_HEAD_DIM
    heads, n_bias, bk, _ = bias_tiles.shape
    nq = seq // bq
    bw = hpb * hw
    return pl.pallas_call(
        functools.partial(_attn_kernel, lam_init=lam_init),
        out_shape=jax.ShapeDtypeStruct((t, heads * hw), BF16),
        grid=(batch, heads // hpb),
        in_specs=[
            pl.BlockSpec((4, B_HEAD_DIM), lambda b, h: (0, 0)),
            pl.BlockSpec((hw, bq), lambda b, h: (0, 0)),
            pl.BlockSpec((hpb, n_bias, bk, bq), lambda b, h: (h, 0, 0, 0)),
            pl.BlockSpec((seq, bw), lambda b, h: (b, OFF_Q // bw + h)),
            pl.BlockSpec((seq, bw), lambda b, h: (b, OFF_K // bw + h)),
            pl.BlockSpec((seq, bw), lambda b, h: (b, OFF_V // bw + h)),
        ],
        out_specs=pl.BlockSpec((seq, bw), lambda b, h: (b, h)),
        scratch_shapes=[pltpu.VMEM((hpb, seq // bk, hw + BF16_ROWS, bk), BF16),
                        pltpu.VMEM((hpb, nq, hw, 2 * bq), BF16),
                        pltpu.VMEM((hpb, bk, 2 * bq), F32),
                        pltpu.VMEM((hpb, bk, 2 * bq), BF16),
                        pltpu.VMEM((hpb, 1, 2 * bq), F32),
                        pltpu.VMEM((hpb, nq, 1, 2 * bq), F32),
                        pltpu.VMEM((hpb, 1, 2 * bq), F32),
                        pltpu.VMEM((hpb, nq, hw + BF16_ROWS, 2 * bq), F32)],
        compiler_params=_cparams(2, vmem_limit=BIG_VMEM_LIMIT_V7X),
        name="diff_attn",
    )(lam_p, subln_t, bias_tiles, proj, proj, proj)


def _split_dot(x, e_bf16, passes):
    out = None
    r = x
    for _ in range(passes):
        hi = r.astype(BF16)
        part = jnp.dot(hi, e_bf16, preferred_element_type=F32)
        out = part if out is None else out + part
        r = r - hi.astype(F32)
    return out


def _mamba_kernel(*refs, n_z, n_xbc):
    z_refs, xbc_refs = refs[:n_z], refs[n_z:n_z + n_xbc]
    (dt_ref, cw_ref, cb_ref, dtb_ref, alog_ref, dsk_ref, ng_ref, e_ref, o_ref, tail_ref, st_ref) = refs[n_z + n_xbc:]
    z_rows = lambda sl: jnp.concatenate([r[sl, :] for r in z_refs], axis=1)
    xbc_rows = lambda sl: jnp.concatenate([r[sl, :] for r in xbc_refs], axis=1)
    rows, c_inner = o_ref.shape
    conv_ch = cw_ref.shape[1]
    heads = c_inner // C_HEAD_DIM
    gw = c_inner // C_GROUPS
    heads_per_group = gw // C_HEAD_DIM
    c = pl.program_id(1)

    @pl.when(c == 0)
    def _():
        tail_ref[...] = jnp.zeros_like(tail_ref)
        st_ref[...] = jnp.zeros_like(st_ref)

    r_i = lax.broadcasted_iota(jnp.int32, (CHUNK, CHUNK), 0)
    c_i = lax.broadcasted_iota(jnp.int32, (CHUNK, CHUNK), 1)
    tri = r_i >= c_i
    lower = tri.astype(F32)
    upper = (r_i <= c_i).astype(F32)
    shift = jnp.concatenate([(r_i - c_i == d).astype(BF16) for d in range(C_CONV - 1, 0, -1)], axis=0)
    row8 = lax.broadcasted_iota(jnp.int32, (8, conv_ch), 0)
    is_head = lax.broadcasted_iota(jnp.int32, (CHUNK, LANES), 1) < heads
    lane_lo = lax.broadcasted_iota(jnp.int32, (CHUNK, LANES), 1) < C_HEAD_DIM
    neg_a = -jnp.exp(alog_ref[...]) * LOG2E
    e = e_ref[...]
    hp = lax.Precision.HIGHEST
    prev = st_ref[...]

    for u in range(rows // CHUNK):
        rs = slice(u * CHUNK, (u + 1) * CHUNK)
        x_cur = xbc_rows(rs)
        delayed = jnp.dot(shift, x_cur, preferred_element_type=F32)
        if u == 0:
            tail = tail_ref[8:16, :]
        else:
            tail = xbc_rows(slice(u * CHUNK - 16, u * CHUNK)).astype(F32)[8:16, :]
        conv = cb_ref[...] + cw_ref[C_CONV - 1:C_CONV, :] * x_cur.astype(F32)
        conv_top = jnp.zeros_like(tail)
        for k in range(C_CONV - 1):
            d = C_CONV - 1 - k
            conv = conv + cw_ref[k:k + 1, :] * delayed[k * CHUNK:(k + 1) * CHUNK]
            conv_top = conv_top + cw_ref[k:k + 1, :] * jnp.where(row8 < d, pltpu.roll(tail, d, 0), 0.0)
        conv = jnp.concatenate([conv[0:8] + conv_top, conv[8:]], axis=0)
        xbc = conv * jax.nn.sigmoid(conv)
        xs = xbc[:, :c_inner]
        bm = xbc[:, c_inner:c_inner + C_GROUPS * C_STATE]
        cm = xbc[:, c_inner + C_GROUPS * C_STATE:]

        dt = jnp.where(is_head, jax.nn.softplus(dt_ref[rs, :] + dtb_ref[...]), 0.0)
        a = dt * neg_a
        a_cs = jnp.dot(lower, a, precision=hp, preferred_element_type=F32)
        a_cs_t = jnp.dot(a.T, upper, precision=hp, preferred_element_type=F32)

        dt_x = _split_dot(dt, e, 2)
        acs_x = _split_dot(a_cs, e, 3)
        last_x = acs_x[CHUNK - 1:CHUNK, :]
        xd = xs * dt_x
        xd_b = xd.astype(BF16)
        xdd_b = (xd * jnp.exp2(last_x - acs_x)).astype(BF16)
        prev_b = prev.astype(BF16)
        eacs = jnp.exp2(acs_x)

        y_parts = []
        st_parts = []
        for g in range(C_GROUPS):
            gs = slice(g * gw, (g + 1) * gw)
            b_g = bm[:, g * C_STATE:(g + 1) * C_STATE]
            c_g = cm[:, g * C_STATE:(g + 1) * C_STATE].astype(BF16)
            cb = lax.dot_general(c_g, b_g.astype(BF16), (((1,), (1,)), ((), ())), preferred_element_type=F32)
            st_parts.append(jnp.dot(b_g.T.astype(BF16), xdd_b[:, gs], preferred_element_type=F32))
            y_off = jnp.dot(c_g, prev_b[:, gs], preferred_element_type=F32) * eacs[:, gs]
            tiles = []
            for pr in range(heads_per_group // 2):
                t_idx = g * (heads_per_group // 2) + pr
                xt = xd_b[:, t_idx * LANES:(t_idx + 1) * LANES]
                res = []
                for half in range(2):
                    hd = 2 * t_idx + half
                    seg = a_cs[:, hd:hd + 1] - a_cs_t[hd:hd + 1, :]
                    decay = jnp.exp2(jnp.where(tri, seg, NEG_BIG))
                    res.append(jnp.dot((cb * decay).astype(BF16), xt, preferred_element_type=F32))
                tiles.append(jnp.where(lane_lo, res[0], res[1]))
            y_parts.append(jnp.concatenate(tiles, axis=1) + y_off)

        prev = prev * jnp.exp2(last_x) + jnp.concatenate(st_parts, axis=1)
        y = jnp.concatenate(y_parts, axis=1) + dsk_ref[...] * xs
        zf = z_rows(rs).astype(F32)
        gated = y * (zf * jax.nn.sigmoid(zf))
        o_ref[rs, :] = _rmsnorm_f32(gated, ng_ref[...]).astype(BF16)

    st_ref[...] = prev
    tail_ref[...] = xbc_rows(slice(rows - 16, rows)).astype(F32)


def _mamba(proj, dt_raw, conv_w, conv_b, dt_bias, a_log, d_skip_x, norm_g, expand, *, batch, seq, rows):
    t = proj.shape[0]
    c_inner = norm_g.shape[1]
    conv_ch = conv_w.shape[1]
    nc = seq // rows
    row = lambda b, c: (b * nc + c, 0)
    const = lambda b, c: (0, 0)
    n_z, n_xbc = c_inner // SPLIT_W, conv_ch // SPLIT_W
    col_block = lambda k: pl.BlockSpec((rows, SPLIT_W), lambda b, c: (b * nc + c, k))
    return pl.pallas_call(
        functools.partial(_mamba_kernel, n_z=n_z, n_xbc=n_xbc),
        out_shape=jax.ShapeDtypeStruct((t, c_inner), BF16),
        grid=(batch, nc),
        in_specs=[col_block(OFF_Z // SPLIT_W + k) for k in range(n_z)]
        + [col_block(OFF_XBC // SPLIT_W + k) for k in range(n_xbc)] + [
            pl.BlockSpec((rows, LANES), row),
            pl.BlockSpec((C_CONV, conv_ch), const),
            pl.BlockSpec((1, conv_ch), const),
            pl.BlockSpec((1, LANES), const),
            pl.BlockSpec((1, LANES), const),
            pl.BlockSpec((1, c_inner), const),
            pl.BlockSpec((1, c_inner), const),
            pl.BlockSpec((LANES, c_inner), const),
        ],
        out_specs=pl.BlockSpec((rows, c_inner), row),
        scratch_shapes=[pltpu.VMEM((16, conv_ch), F32), pltpu.VMEM((C_STATE, c_inner), F32)],
        compiler_params=_cparams(2),
        name="mamba_ssd",
    )(*([proj] * (n_z + n_xbc)), dt_raw, conv_w, conv_b, dt_bias, a_log, d_skip_x, norm_g, expand)


def _merge_kernel(ya_ref, yb_ref, yc_ref, g0_ref, g1_ref, g2_ref, wa_ref, wb_ref, wc_ref, o_ref):
    pa = jnp.dot(ya_ref[...], wa_ref[...], preferred_element_type=F32)
    pb = jnp.dot(yb_ref[...], wb_ref[...], preferred_element_type=F32)
    pc = jnp.dot(yc_ref[...], wc_ref[...], preferred_element_type=F32)
    o_ref[...] = (g0_ref[...].astype(F32) * pa + g1_ref[...].astype(F32) * pb
                  + g2_ref[...].astype(F32) * pc).astype(BF16)


def _merge(ya, yb, yc, proj, w_pa, w_pb, w_pc, layer, *, bm, bn):
    t = ya.shape[0]
    d = w_pa.shape[2]
    gate_spec = lambda k: pl.BlockSpec((bm, bn), lambda i, j: (i, (OFF_GATE + k * d) // bn + j))
    resident = pl.Buffered(1) if bn == d else None
    return pl.pallas_call(
        _merge_kernel,
        out_shape=jax.ShapeDtypeStruct((t, d), BF16),
        grid=(t // bm, d // bn),
        in_specs=[
            pl.BlockSpec((bm, ya.shape[1]), lambda i, j: (i, 0)),
            pl.BlockSpec((bm, yb.shape[1]), lambda i, j: (i, 0)),
            pl.BlockSpec((bm, yc.shape[1]), lambda i, j: (i, 0)),
            gate_spec(0), gate_spec(1), gate_spec(2),
            pl.BlockSpec((None, w_pa.shape[1], bn), lambda i, j: (layer, 0, j), pipeline_mode=resident),
            pl.BlockSpec((None, w_pb.shape[1], bn), lambda i, j: (layer, 0, j), pipeline_mode=resident),
            pl.BlockSpec((None, w_pc.shape[1], bn), lambda i, j: (layer, 0, j), pipeline_mode=resident),
        ],
        out_specs=pl.BlockSpec((bm, bn), lambda i, j: (i, j)),
        compiler_params=_cparams(2),
        name="merge",
    )(ya, yb, yc, proj, proj, proj, w_pa, w_pb, w_pc)


def _outproj_kernel(m_ref, w_ref, h_ref, o_ref):
    o_ref[...] = h_ref[...] + jnp.dot(m_ref[...], w_ref[...], preferred_element_type=F32)


def _outproj(merged, w_out, h, layer, *, bm, bn):
    t, d = h.shape
    return pl.pallas_call(
        _outproj_kernel,
        out_shape=jax.ShapeDtypeStruct((t, d), F32),
        grid=(t // bm, d // bn),
        in_specs=[
            pl.BlockSpec((bm, merged.shape[1]), lambda i, j: (i, 0)),
            pl.BlockSpec((None, merged.shape[1], bn), lambda i, j: (layer, 0, j),
                         pipeline_mode=pl.Buffered(1) if bn == d else None),
            pl.BlockSpec((bm, bn), lambda i, j: (i, j)),
        ],
        out_specs=pl.BlockSpec((bm, bn), lambda i, j: (i, j)),
        compiler_params=_cparams(2),
        name="outproj",
    )(merged, w_out, h)


def kernel(x, rel_bias, final_norm, ffn1_norm, ffn1_wi, ffn1_wo, mix_norm, w_in, sgu_ln_g, sgu_ln_b, sgu_w, sgu_b, diff_lambda, diff_subln, conv_w, conv_b, dt_bias, a_log, d_skip, ssm_norm, w_pa, w_pb, w_pc, w_out, ffn2_norm, ffn2_wi, ffn2_wo):
    batch, seq, d = x.shape
    depth = ffn1_wi.shape[0]
    t = batch * seq
    heads_c = dt_bias.shape[1]
    a_width = sgu_ln_g.shape[1]
    c_inner = ssm_norm.shape[1]
    conv_ch = conv_w.shape[2]
    d_ff = ffn1_wo.shape[1]

    bm = min(1024, t)
    bf = 512
    bq = 512
    bk = 512
    assert t % bm == 0 and d_ff % bf == 0 and seq % bk == 0 and bk % bq == 0 and bq % LANES == 0
    assert w_in.shape[2] == OFF_GATE + heads_c + 3 * d
    assert (OFF_AV, OFF_Q, OFF_Z, OFF_XBC, OFF_GATE) == (a_width, 2 * a_width, 5 * a_width, 5 * a_width + c_inner,
                                                         5 * a_width + c_inner + conv_ch)

    w_src = jnp.swapaxes(w_in, 1, 2)
    w_t = w_src[0].astype(BF16)
    wi1, wo1 = ffn1_wi[0].astype(BF16), ffn1_wo[0].astype(BF16)
    wi2, wo2 = ffn2_wi[0].astype(BF16), ffn2_wo[0].astype(BF16)
    wpa, wpb, wpc, wout = w_pa.astype(BF16), w_pb.astype(BF16), w_pc.astype(BF16), w_out.astype(BF16)

    bsb = jnp.repeat(jnp.swapaxes(sgu_b, 1, 2), a_width // A_GROUPS, axis=2)
    pad_h = ((0, 0), (0, LANES - heads_c))
    dtb = jnp.pad(dt_bias, pad_h)
    alog = jnp.pad(a_log, pad_h)
    dsk_x = jnp.repeat(d_skip, C_HEAD_DIM, axis=1)
    expand = np.zeros((LANES, c_inner), np.float32)
    expand[np.arange(c_inner) // C_HEAD_DIM, np.arange(c_inner)] = 1.0
    expand = jnp.asarray(expand, BF16)
    bias_tiles = _bias_tiles(rel_bias, _bucket_tiles(bq, bk))
    subln_t = jnp.broadcast_to(diff_subln[:, :, None], diff_subln.shape + (bq,))

    h = x.reshape(t, d)
    for l in range(depth):
        lam_init = 0.8 - 0.6 * math.exp(-0.3 * l)
        nxt = min(l + 1, depth - 1)
        h, wi1, wo1 = _ffn(h, ffn1_norm[l][None], wi1, wo1, ffn1_wi, ffn1_wo, nxt, bm=bm, bf=bf)
        proj, dt_raw, w_t = _inproj(h, mix_norm[l][None], w_t, w_src, nxt, bm=bm, bn=2048, n_dt=heads_c)
        ya = _sgu(proj, sgu_ln_g[l][None], sgu_ln_b[l][None], sgu_w[l], bsb[l], rows=4 * CHUNK)
        yb = _attn(proj, bias_tiles, diff_lambda[l], subln_t[l],
                   batch=batch, seq=seq, bq=bq, hpb=2, lam_init=lam_init)
        yc = _mamba(proj, dt_raw, conv_w[l], conv_b[l][None], dtb[l][None], alog[l][None],
                    dsk_x[l][None], ssm_norm[l][None], expand, batch=batch, seq=seq, rows=4 * CHUNK)
        merged = _merge(ya, yb, yc, proj, wpa, wpb, wpc, l, bm=bm // 2, bn=d)
        h = _outproj(merged, wout, h, l, bm=bm, bn=d)
        h, wi2, wo2 = _ffn(h, ffn2_norm[l][None], wi2, wo2, ffn2_wi, ffn2_wo, nxt,
                           final_norm[None] if l == depth - 1 else None, bm=bm, bf=bf)
    return h.reshape(batch, seq, d)
```

```python
import functools
import math

import numpy as np
import jax
import jax.numpy as jnp
from jax import lax
from jax.experimental import pallas as pl
from jax.experimental.pallas import tpu as pltpu

F32 = jnp.float32
BF16 = jnp.bfloat16

EPS = 1e-6
CHUNK = 128
LANES = 128
BF16_ROWS = 16
A_GROUPS = 8
B_HEAD_DIM = 64
C_HEAD_DIM = 64
C_GROUPS = 4
C_STATE = 128
C_CONV = 4
REL_BUCKETS = 32
REL_MAX_DIST = 128
REL_EXACT = REL_BUCKETS // 2
NEG_BIG = -1e30
LOG2E = math.log2(math.e)
VMEM_LIMIT_V7X = 56 * 1024 * 1024
BIG_VMEM_LIMIT_V7X = 60 * 1024 * 1024

OFF_AU, OFF_AV, OFF_Q, OFF_K, OFF_V, OFF_Z, OFF_XBC, OFF_GATE = 0, 1024, 2048, 3072, 4096, 5120, 7168, 10240
SPLIT_W = 1024


def _cparams(n_axes, vmem_limit=VMEM_LIMIT_V7X):
    return pltpu.CompilerParams(dimension_semantics=("arbitrary",) * n_axes, vmem_limit_bytes=vmem_limit)


def _rmsnorm_f32(x, g):
    ms = jnp.mean(x * x, axis=-1, keepdims=True)
    return (x * lax.rsqrt(ms + EPS)) * g


def _ffn_kernel(x_ref, g_ref, wg_ref, wu_ref, wo_ref, wi_src_ref, wo_src_ref, o_ref, wi_cast_ref, wo_cast_ref,
                xn_ref):
    j = pl.program_id(1)

    @pl.when(j == 0)
    def _():
        x = x_ref[...]
        xn_ref[...] = _rmsnorm_f32(x, g_ref[...]).astype(BF16)
        o_ref[...] = x

    xn = xn_ref[...]
    gate = jnp.dot(xn, wg_ref[...], preferred_element_type=F32)
    up = jnp.dot(xn, wu_ref[...], preferred_element_type=F32)
    act = ((0.5 * gate) * jax.nn.sigmoid(gate) * up).astype(BF16)
    o_ref[...] += jnp.dot(act, wo_ref[...], preferred_element_type=F32)
    wi_cast_ref[...] = wi_src_ref[...].astype(BF16)
    wo_cast_ref[...] = wo_src_ref[...].astype(BF16)


def _ffn(h, g, wi, wo, wi_f32, wo_f32, cast_layer, *, bm, bf):
    t, d = h.shape
    d_ff = wo.shape[0]
    ni, nj = t // bm, d_ff // bf
    wi_tile = (d // ni, 2 * d_ff // nj)
    wo_tile = (d_ff // nj, d // ni)
    assert d % ni == 0 and wi_tile[0] % BF16_ROWS == 0 and wi_tile[1] % LANES == 0 and wo_tile[1] % LANES == 0
    return pl.pallas_call(
        _ffn_kernel,
        out_shape=(jax.ShapeDtypeStruct((t, d), F32), jax.ShapeDtypeStruct(wi.shape, BF16),
                   jax.ShapeDtypeStruct(wo.shape, BF16)),
        grid=(ni, nj),
        in_specs=[
            pl.BlockSpec((bm, d), lambda i, j: (i, 0)),
            pl.BlockSpec((1, d), lambda i, j: (0, 0)),
            pl.BlockSpec((d, bf), lambda i, j: (0, j)),
            pl.BlockSpec((d, bf), lambda i, j: (0, j + nj)),
            pl.BlockSpec((bf, d), lambda i, j: (j, 0)),
            pl.BlockSpec((None,) + wi_tile, lambda i, j: (cast_layer, i, j)),
            pl.BlockSpec((None,) + wo_tile, lambda i, j: (cast_layer, j, i)),
        ],
        out_specs=(pl.BlockSpec((bm, d), lambda i, j: (i, 0)),
                   pl.BlockSpec(wi_tile, lambda i, j: (i, j)),
                   pl.BlockSpec(wo_tile, lambda i, j: (j, i))),
        scratch_shapes=[pltpu.VMEM((bm, d), BF16)],
        compiler_params=_cparams(2, vmem_limit=BIG_VMEM_LIMIT_V7X),
        name="ffn",
    )(h, g, wi, wi, wo, wi_f32, wo_f32)


def _inproj_kernel(x_ref, g_ref, w_ref, wdt_ref, w_src_ref, o_ref, dt_ref, w_cast_ref, xn_ref, *, nj_plain):
    j = pl.program_id(1)
    nt = (((1,), (1,)), ((), ()))

    @pl.when(j == 0)
    def _():
        xn = _rmsnorm_f32(x_ref[...], g_ref[...]).astype(BF16)
        xn_ref[...] = xn
        dt_ref[...] = lax.dot_general(xn, wdt_ref[...], nt, preferred_element_type=F32)

    @pl.when(j < nj_plain)
    def _():
        o_ref[...] = lax.dot_general(xn_ref[...], w_ref[...], nt, preferred_element_type=F32).astype(BF16)
        w_cast_ref[...] = w_src_ref[...].astype(BF16)

    @pl.when(j >= nj_plain)
    def _():
        o_ref[...] = jax.nn.sigmoid(
            lax.dot_general(xn_ref[...], w_ref[...], nt, preferred_element_type=F32)).astype(BF16)
        w_cast_ref[...] = w_src_ref[...].astype(BF16)


def _inproj(h, g, w_t, w_src, cast_layer, *, bm, bn, n_dt):
    t, d = h.shape
    n_in = w_t.shape[0]
    n = n_in - n_dt
    ni, nj = t // bm, n // bn
    nj_plain = OFF_GATE // bn
    assert OFF_GATE % bn == 0 and OFF_GATE % LANES == 0 and n % bn == 0 and n_dt <= LANES and n_dt % BF16_ROWS == 0
    n_cast = max(c for c in range(1, ni * nj + 1) if n_in % c == 0 and (n_in // c) % BF16_ROWS == 0)
    cast_tile = lambda i, j: jnp.minimum(i * nj + j, n_cast - 1)
    return pl.pallas_call(
        functools.partial(_inproj_kernel, nj_plain=nj_plain),
        out_shape=(jax.ShapeDtypeStruct((t, n), BF16), jax.ShapeDtypeStruct((t, LANES), F32),
                   jax.ShapeDtypeStruct((n_in, d), BF16)),
        grid=(ni, nj),
        in_specs=[
            pl.BlockSpec((bm, d), lambda i, j: (i, 0)),
            pl.BlockSpec((1, d), lambda i, j: (0, 0)),
            pl.BlockSpec((pl.Element(bn), pl.Element(d)),
                         lambda i, j: (pl.multiple_of(j * bn + jnp.where(j >= nj_plain, n_dt, 0), BF16_ROWS), 0)),
            pl.BlockSpec((LANES, d), lambda i, j: (OFF_GATE // LANES, 0)),
            pl.BlockSpec((None, n_in // n_cast, d), lambda i, j: (cast_layer, cast_tile(i, j), 0)),
        ],
        out_specs=(pl.BlockSpec((bm, bn), lambda i, j: (i, j)),
                   pl.BlockSpec((bm, LANES), lambda i, j: (i, 0)),
                   pl.BlockSpec((n_in // n_cast, d), lambda i, j: (cast_tile(i, j), 0))),
        scratch_shapes=[pltpu.VMEM((bm, d), BF16)],
        compiler_params=_cparams(2, vmem_limit=BIG_VMEM_LIMIT_V7X),
        name="inproj",
    )(h, g, w_t, w_t, w_src)


def _sgu_kernel(u_ref, v_ref, lng_ref, lnb_ref, w_ref, bsb_ref, o_ref):
    rows = u_ref.shape[0]
    u = jax.nn.gelu(u_ref[...].astype(F32))
    v = jax.nn.gelu(v_ref[...].astype(F32))
    mu = jnp.mean(v, axis=-1, keepdims=True)
    var = jnp.mean(jnp.square(v - mu), axis=-1, keepdims=True)
    vn = (((v - mu) * lax.rsqrt(var + EPS)) * lng_ref[...] + lnb_ref[...]).astype(BF16)
    r_i = lax.broadcasted_iota(jnp.int32, (CHUNK, CHUNK), 0)
    c_i = lax.broadcasted_iota(jnp.int32, (CHUNK, CHUNK), 1)
    tri = r_i >= c_i
    for g in range(A_GROUPS):
        wm = jnp.where(tri, w_ref[g], 0.0).astype(BF16)
        cs = slice(g * LANES, (g + 1) * LANES)
        for c in range(rows // CHUNK):
            rs = slice(c * CHUNK, (c + 1) * CHUNK)
            mixed = jnp.dot(wm, vn[rs, cs], preferred_element_type=F32) + bsb_ref[:, cs]
            o_ref[rs, cs] = (u[rs, cs] * mixed).astype(BF16)


def _sgu(proj, ln_g, ln_b, w_s, bsb, *, rows):
    t = proj.shape[0]
    aw = ln_g.shape[1]
    return pl.pallas_call(
        _sgu_kernel,
        out_shape=jax.ShapeDtypeStruct((t, aw), BF16),
        grid=(t // rows,),
        in_specs=[
            pl.BlockSpec((rows, aw), lambda i: (i, OFF_AU // aw)),
            pl.BlockSpec((rows, aw), lambda i: (i, OFF_AV // aw)),
            pl.BlockSpec((1, aw), lambda i: (0, 0)),
            pl.BlockSpec((1, aw), lambda i: (0, 0)),
            pl.BlockSpec((A_GROUPS, CHUNK, CHUNK), lambda i: (0, 0, 0)),
            pl.BlockSpec((CHUNK, aw), lambda i: (0, 0)),
        ],
        out_specs=pl.BlockSpec((rows, aw), lambda i: (i, 0)),
        compiler_params=_cparams(1),
        name="sgu",
    )(proj, proj, ln_g, ln_b, w_s, bsb)


def _t5_bucket_np(dist):
    n = np.maximum(dist, 0)
    nf = np.maximum(n, 1).astype(np.float64)
    large = REL_EXACT + (np.log(nf / REL_EXACT) / math.log(REL_MAX_DIST / REL_EXACT)
                         * (REL_BUCKETS - REL_EXACT)).astype(np.int32)
    large = np.minimum(large, REL_BUCKETS - 1)
    return np.where(n < REL_EXACT, n, large).astype(np.int32)


def _bucket_tiles(bq, bk):
    last_bucket_from = int(np.argmax(_t5_bucket_np(np.arange(4 * REL_MAX_DIST)) == REL_BUCKETS - 1))
    n_tiles = -(-(bk - 1 + last_bucket_from) // bq)
    c = np.arange(bk)[:, None]
    r = np.arange(bq)[None, :]
    tiles = []
    for t in range(n_tiles):
        dist = t * bq + r - c
        tiles.append(np.where(dist >= 0, _t5_bucket_np(dist), REL_BUCKETS))
    assert _t5_bucket_np(np.array([n_tiles * bq - (bk - 1)]))[0] == REL_BUCKETS - 1
    return np.stack(tiles).astype(np.int32)


BIAS_ROWS = 64


def _bias_tiles_kernel(rb_ref, bkt_ref, o_ref, *, present):
    hd = pl.program_id(0)
    n_bias, bk, bq = o_ref.shape
    for t in range(n_bias - 1):
        for rc in range(bk // BIAS_ROWS):
            rs = slice(rc * BIAS_ROWS, (rc + 1) * BIAS_ROWS)
            bt = bkt_ref[t, rs, :]
            tile = jnp.where(bt == REL_BUCKETS, NEG_BIG, 0.0).astype(F32)
            for b in present[t][rc]:
                tile = jnp.where(bt == b, rb_ref[b, hd] * LOG2E, tile)
            o_ref[t, rs, :] = tile
    o_ref[n_bias - 1] = jnp.full((bk, bq), rb_ref[REL_BUCKETS - 1, hd] * LOG2E, F32)


def _bias_tiles(rel_bias, buckets_np):
    n_tiles, bk, bq = buckets_np.shape
    heads = rel_bias.shape[1]
    present = tuple(tuple(tuple(int(b) for b in np.unique(buckets_np[t, rc * BIAS_ROWS:(rc + 1) * BIAS_ROWS])
                                if b < REL_BUCKETS)
                          for rc in range(bk // BIAS_ROWS)) for t in range(n_tiles))
    return pl.pallas_call(
        functools.partial(_bias_tiles_kernel, present=present),
        out_shape=jax.ShapeDtypeStruct((heads, n_tiles + 1, bk, bq), F32),
        grid=(heads,),
        in_specs=[pl.BlockSpec(memory_space=pltpu.SMEM),
                  pl.BlockSpec((n_tiles, bk, bq), lambda h: (0, 0, 0))],
        out_specs=pl.BlockSpec((None, n_tiles + 1, bk, bq), lambda h: (h, 0, 0, 0)),
        compiler_params=_cparams(1),
        name="bias_tiles",
    )(rel_bias, jnp.asarray(buckets_np))


def _attn_kernel(lam_ref, sub_ref, bias_ref, q_ref, k_ref, v_ref, o_ref,
                 vt_ref, qqt_ref, s_ref, p_ref, cm_ref, m_ref, al_ref, acc_ref, *, lam_init):
    hw = 2 * B_HEAD_DIM
    hpb, bk = s_ref.shape[0], s_ref.shape[1]
    bq = s_ref.shape[2] // 2
    seq = k_ref.shape[0]
    nq = seq // bq
    n_bias = bias_ref.shape[1]
    cols = [slice(hh * hw, (hh + 1) * hw) for hh in range(hpb)]
    last_block = lambda i: (i * bq) // bk
    n_pairs = sum(last_block(i) + 1 for i in range(nq))

    pad_row = lax.broadcasted_iota(jnp.int32, (vt_ref.shape[2] - hw, bk), 0)
    ones_row = jnp.where(pad_row == 0, 1.0, 0.0).astype(BF16)
    dim = lax.broadcasted_iota(jnp.int32, (hw, bq), 0)
    for hh in range(hpb):
        for c in range(seq // LANES):
            blk, off = divmod(c * LANES, bk)
            vt_ref[hh, blk, 0:hw, off:off + LANES] = (
                v_ref[c * LANES:(c + 1) * LANES, cols[hh]].astype(F32).T.astype(BF16))
        for blk in range(seq // bk):
            vt_ref[hh, blk, hw:, :] = ones_row
        for i in range(nq):
            qt = (q_ref[i * bq:(i + 1) * bq, cols[hh]].astype(F32) * (B_HEAD_DIM ** -0.5 * LOG2E)).T
            qqt_ref[hh, i] = jnp.concatenate([jnp.where(dim < B_HEAD_DIM, qt, 0.0),
                                              jnp.where(dim >= B_HEAD_DIM, qt, 0.0)], axis=1).astype(BF16)
    m_ref[...] = jnp.full(m_ref.shape, NEG_BIG, F32)
    al_ref[...] = jnp.ones(al_ref.shape, F32)
    acc_ref[...] = jnp.zeros(acc_ref.shape, F32)
    p_ref[...] = jnp.zeros(p_ref.shape, BF16)

    far_tile = n_bias - 1
    bias_tile = lambda i, j: jnp.minimum((i * bq - j * bk) // bq, far_tile)
    far_bias = [jnp.concatenate([bias_ref[hh, far_tile, 0:1, :]] * 2, axis=1) for hh in range(hpb)]

    def scores(hh, i, j, far):
        kb = k_ref[pl.ds(pl.multiple_of(j * bk, bk), bk), cols[hh]]
        s = jnp.dot(kb, qqt_ref[hh, i], preferred_element_type=F32)
        if far:
            s_ref[hh] = s
            cm_ref[hh] = jnp.max(s, axis=0, keepdims=True) + far_bias[hh]
        else:
            bias = bias_ref[hh, bias_tile(i, j)]
            s = s + jnp.concatenate([bias, bias], axis=1)
            s_ref[hh] = s
            cm_ref[hh] = jnp.max(s, axis=0, keepdims=True)

    def probabilities(hh, i, far):
        m_prev = m_ref[hh, i]
        m_next = jnp.maximum(m_prev, cm_ref[hh])
        m_ref[hh, i] = m_next
        al_ref[hh] = jnp.exp2(m_prev - m_next)
        pending = jnp.where(far == 1, far_bias[hh], 0.0)
        p_ref[hh] = jnp.exp2(s_ref[hh] - (m_next - pending)).astype(BF16)

    def weighted_values(hh, i, j):
        acc_ref[hh, i] = al_ref[hh] * acc_ref[hh, i] + jnp.dot(vt_ref[hh, j], p_ref[hh],
                                                                preferred_element_type=F32)

    for hh in range(hpb):
        scores(hh, 0, 0, False)

    def body(t, carry):
        ip, jp, ic, jc, cur_far = carry
        wrap = jc >= last_block(ic)
        i_next = jnp.where(wrap, ic + 1, ic)
        j_next = jnp.where(wrap, 0, jc + 1)
        next_far = (bias_tile(i_next, j_next) == far_tile).astype(jnp.int32)

        def step(far):
            for hh in range(hpb):
                weighted_values(hh, ip, jp)
            for hh in range(hpb):
                probabilities(hh, ic, cur_far)
            for hh in range(hpb):
                scores(hh, i_next, j_next, far)

        pl.when(next_far == 1)(functools.partial(step, True))
        pl.when(next_far == 0)(functools.partial(step, False))
        return ic, jc, i_next, j_next, next_far

    zero = jnp.int32(0)
    ip, jp, ic, jc, cur_far = lax.fori_loop(0, n_pairs - 1, body, (zero, zero, zero, zero, zero))
    for hh in range(hpb):
        weighted_values(hh, ip, jp)
        probabilities(hh, ic, cur_far)
    for hh in range(hpb):
        weighted_values(hh, ic, jc)

    lam_p = lam_ref[...]
    lam = (jnp.exp(jnp.sum(lam_p[0:1] * lam_p[1:2], axis=1, keepdims=True))
           - jnp.exp(jnp.sum(lam_p[2:3] * lam_p[3:4], axis=1, keepdims=True)) + lam_init)
    for hh in range(hpb):
        for i in range(nq):
            o = acc_ref[hh, i, 0:hw, :] * (1.0 / acc_ref[hh, i, hw:hw + 1, :])
            attn = o[:, :bq] - lam * o[:, bq:]
            ms = jnp.mean(attn * attn, axis=0, keepdims=True)
            y = ((attn * lax.rsqrt(ms + EPS)) * sub_ref[...]) * (1.0 - lam_init)
            o_ref[i * bq:(i + 1) * bq, cols[hh]] = y.T.astype(BF16)


def _attn(proj, bias_tiles, lam_p, subln_t, *, batch, seq, bq, hpb, lam_init):
    t = proj.shape[0]
    hw = 2 * B_HEAD_DIM
    heads, n_bias, bk, _ = bias_tiles.shape
    nq = seq // bq
    bw = hpb * hw
    return pl.pallas_call(
        functools.partial(_attn_kernel, lam_init=lam_init),
        out_shape=jax.ShapeDtypeStruct((t, heads * hw), BF16),
        grid=(batch, heads // hpb),
        in_specs=[
            pl.BlockSpec((4, B_HEAD_DIM), lambda b, h: (0, 0)),
            pl.BlockSpec((hw, bq), lambda b, h: (0, 0)),
            pl.BlockSpec((hpb, n_bias, bk, bq), lambda b, h: (h, 0, 0, 0)),
            pl.BlockSpec((seq, bw), lambda b, h: (b, OFF_Q // bw + h)),
            pl.BlockSpec((seq, bw), lambda b, h: (b, OFF_K // bw + h)),
            pl.BlockSpec((seq, bw), lambda b, h: (b, OFF_V // bw + h)),
        ],
        out_specs=pl.BlockSpec((seq, bw), lambda b, h: (b, h)),
        scratch_shapes=[pltpu.VMEM((hpb, seq // bk, hw + BF16_ROWS, bk), BF16),
                        pltpu.VMEM((hpb, nq, hw, 2 * bq), BF16),
                        pltpu.VMEM((hpb, bk, 2 * bq), F32),
                        pltpu.VMEM((hpb, bk, 2 * bq), BF16),
                        pltpu.VMEM((hpb, 1, 2 * bq), F32),
                        pltpu.VMEM((hpb, nq, 1, 2 * bq), F32),
                        pltpu.VMEM((hpb, 1, 2 * bq), F32),
                        pltpu.VMEM((hpb, nq, hw + BF16_ROWS, 2 * bq), F32)],
        compiler_params=_cparams(2, vmem_limit=BIG_VMEM_LIMIT_V7X),
        name="diff_attn",
    )(lam_p, subln_t, bias_tiles, proj, proj, proj)


def _split_dot(x, e_bf16, passes):
    out = None
    r = x
    for _ in range(passes):
        hi = r.astype(BF16)
        part = jnp.dot(hi, e_bf16, preferred_element_type=F32)
        out = part if out is None else out + part
        r = r - hi.astype(F32)
    return out


def _mamba_kernel(*refs, n_z, n_xbc):
    z_refs, xbc_refs = refs[:n_z], refs[n_z:n_z + n_xbc]
    (dt_ref, cw_ref, cb_ref, dtb_ref, alog_ref, dsk_ref, ng_ref, e_ref, o_ref, tail_ref, st_ref) = refs[n_z + n_xbc:]
    z_rows = lambda sl: jnp.concatenate([r[sl, :] for r in z_refs], axis=1)
    xbc_rows = lambda sl: jnp.concatenate([r[sl, :] for r in xbc_refs], axis=1)
    rows, c_inner = o_ref.shape
    conv_ch = cw_ref.shape[1]
    heads = c_inner // C_HEAD_DIM
    gw = c_inner // C_GROUPS
    heads_per_group = gw // C_HEAD_DIM
    c = pl.program_id(1)

    @pl.when(c == 0)
    def _():
        tail_ref[...] = jnp.zeros_like(tail_ref)
        st_ref[...] = jnp.zeros_like(st_ref)

    r_i = lax.broadcasted_iota(jnp.int32, (CHUNK, CHUNK), 0)
    c_i = lax.broadcasted_iota(jnp.int32, (CHUNK, CHUNK), 1)
    tri = r_i >= c_i
    lower = tri.astype(F32)
    upper = (r_i <= c_i).astype(F32)
    shift = jnp.concatenate([(r_i - c_i == d).astype(BF16) for d in range(C_CONV - 1, 0, -1)], axis=0)
    row8 = lax.broadcasted_iota(jnp.int32, (8, conv_ch), 0)
    is_head = lax.broadcasted_iota(jnp.int32, (CHUNK, LANES), 1) < heads
    lane_lo = lax.broadcasted_iota(jnp.int32, (CHUNK, LANES), 1) < C_HEAD_DIM
    neg_a = -jnp.exp(alog_ref[...]) * LOG2E
    e = e_ref[...]
    hp = lax.Precision.HIGHEST
    prev = st_ref[...]

    for u in range(rows // CHUNK):
        rs = slice(u * CHUNK, (u + 1) * CHUNK)
        x_cur = xbc_rows(rs)
        delayed = jnp.dot(shift, x_cur, preferred_element_type=F32)
        if u == 0:
            tail = tail_ref[8:16, :]
        else:
            tail = xbc_rows(slice(u * CHUNK - 16, u * CHUNK)).astype(F32)[8:16, :]
        conv = cb_ref[...] + cw_ref[C_CONV - 1:C_CONV, :] * x_cur.astype(F32)
        conv_top = jnp.zeros_like(tail)
        for k in range(C_CONV - 1):
            d = C_CONV - 1 - k
            conv = conv + cw_ref[k:k + 1, :] * delayed[k * CHUNK:(k + 1) * CHUNK]
            conv_top = conv_top + cw_ref[k:k + 1, :] * jnp.where(row8 < d, pltpu.roll(tail, d, 0), 0.0)
        conv = jnp.concatenate([conv[0:8] + conv_top, conv[8:]], axis=0)
        xbc = conv * jax.nn.sigmoid(conv)
        xs = xbc[:, :c_inner]
        bm = xbc[:, c_inner:c_inner + C_GROUPS * C_STATE]
        cm = xbc[:, c_inner + C_GROUPS * C_STATE:]

        dt = jnp.where(is_head, jax.nn.softplus(dt_ref[rs, :] + dtb_ref[...]), 0.0)
        a = dt * neg_a
        a_cs = jnp.dot(lower, a, precision=hp, preferred_element_type=F32)
        a_cs_t = jnp.dot(a.T, upper, precision=hp, preferred_element_type=F32)

        dt_x = _split_dot(dt, e, 2)
        acs_x = _split_dot(a_cs, e, 3)
        last_x = acs_x[CHUNK - 1:CHUNK, :]
        xd = xs * dt_x
        xd_b = xd.astype(BF16)
        xdd_b = (xd * jnp.exp2(last_x - acs_x)).astype(BF16)
        prev_b = prev.astype(BF16)
        eacs = jnp.exp2(acs_x)

        y_parts = []
        st_parts = []
        for g in range(C_GROUPS):
            gs = slice(g * gw, (g + 1) * gw)
            b_g = bm[:, g * C_STATE:(g + 1) * C_STATE]
            c_g = cm[:, g * C_STATE:(g + 1) * C_STATE].astype(BF16)
            cb = lax.dot_general(c_g, b_g.astype(BF16), (((1,), (1,)), ((), ())), preferred_element_type=F32)
            st_parts.append(jnp.dot(b_g.T.astype(BF16), xdd_b[:, gs], preferred_element_type=F32))
            y_off = jnp.dot(c_g, prev_b[:, gs], preferred_element_type=F32) * eacs[:, gs]
            tiles = []
            for pr in range(heads_per_group // 2):
                t_idx = g * (heads_per_group // 2) + pr
                xt = xd_b[:, t_idx * LANES:(t_idx + 1) * LANES]
                res = []
                for half in range(2):
                    hd = 2 * t_idx + half
                    seg = a_cs[:, hd:hd + 1] - a_cs_t[hd:hd + 1, :]
                    decay = jnp.exp2(jnp.where(tri, seg, NEG_BIG))
                    res.append(jnp.dot((cb * decay).astype(BF16), xt, preferred_element_type=F32))
                tiles.append(jnp.where(lane_lo, res[0], res[1]))
            y_parts.append(jnp.concatenate(tiles, axis=1) + y_off)

        prev = prev * jnp.exp2(last_x) + jnp.concatenate(st_parts, axis=1)
        y = jnp.concatenate(y_parts, axis=1) + dsk_ref[...] * xs
        zf = z_rows(rs).astype(F32)
        gated = y * (zf * jax.nn.sigmoid(zf))
        o_ref[rs, :] = _rmsnorm_f32(gated, ng_ref[...]).astype(BF16)

    st_ref[...] = prev
    tail_ref[...] = xbc_rows(slice(rows - 16, rows)).astype(F32)


def _mamba(proj, dt_raw, conv_w, conv_b, dt_bias, a_log, d_skip_x, norm_g, expand, *, batch, seq, rows):
    t = proj.shape[0]
    c_inner = norm_g.shape[1]
    conv_ch = conv_w.shape[1]
    nc = seq // rows
    row = lambda b, c: (b * nc + c, 0)
    const = lambda b, c: (0, 0)
    n_z, n_xbc = c_inner // SPLIT_W, conv_ch // SPLIT_W
    col_block = lambda k: pl.BlockSpec((rows, SPLIT_W), lambda b, c: (b * nc + c, k))
    return pl.pallas_call(
        functools.partial(_mamba_kernel, n_z=n_z, n_xbc=n_xbc),
        out_shape=jax.ShapeDtypeStruct((t, c_inner), BF16),
        grid=(batch, nc),
        in_specs=[col_block(OFF_Z // SPLIT_W + k) for k in range(n_z)]
        + [col_block(OFF_XBC // SPLIT_W + k) for k in range(n_xbc)] + [
            pl.BlockSpec((rows, LANES), row),
            pl.BlockSpec((C_CONV, conv_ch), const),
            pl.BlockSpec((1, conv_ch), const),
            pl.BlockSpec((1, LANES), const),
            pl.BlockSpec((1, LANES), const),
            pl.BlockSpec((1, c_inner), const),
            pl.BlockSpec((1, c_inner), const),
            pl.BlockSpec((LANES, c_inner), const),
        ],
        out_specs=pl.BlockSpec((rows, c_inner), row),
        scratch_shapes=[pltpu.VMEM((16, conv_ch), F32), pltpu.VMEM((C_STATE, c_inner), F32)],
        compiler_params=_cparams(2),
        name="mamba_ssd",
    )(*([proj] * (n_z + n_xbc)), dt_raw, conv_w, conv_b, dt_bias, a_log, d_skip_x, norm_g, expand)


def _merge_kernel(ya_ref, yb_ref, yc_ref, g0_ref, g1_ref, g2_ref, wa_ref, wb_ref, wc_ref, o_ref):
    pa = jnp.dot(ya_ref[...], wa_ref[...], preferred_element_type=F32)
    pb = jnp.dot(yb_ref[...], wb_ref[...], preferred_element_type=F32)
    pc = jnp.dot(yc_ref[...], wc_ref[...], preferred_element_type=F32)
    o_ref[...] = (g0_ref[...].astype(F32) * pa + g1_ref[...].astype(F32) * pb
                  + g2_ref[...].astype(F32) * pc).astype(BF16)


def _merge(ya, yb, yc, proj, w_pa, w_pb, w_pc, layer, *, bm, bn):
    t = ya.shape[0]
    d = w_pa.shape[2]
    gate_spec = lambda k: pl.BlockSpec((bm, bn), lambda i, j: (i, (OFF_GATE + k * d) // bn + j))
    resident = pl.Buffered(1) if bn == d else None
    return pl.pallas_call(
        _merge_kernel,
        out_shape=jax.ShapeDtypeStruct((t, d), BF16),
        grid=(t // bm, d // bn),
        in_specs=[
            pl.BlockSpec((bm, ya.shape[1]), lambda i, j: (i, 0)),
            pl.BlockSpec((bm, yb.shape[1]), lambda i, j: (i, 0)),
            pl.BlockSpec((bm, yc.shape[1]), lambda i, j: (i, 0)),
            gate_spec(0), gate_spec(1), gate_spec(2),
            pl.BlockSpec((None, w_pa.shape[1], bn), lambda i, j: (layer, 0, j), pipeline_mode=resident),
            pl.BlockSpec((None, w_pb.shape[1], bn), lambda i, j: (layer, 0, j), pipeline_mode=resident),
            pl.BlockSpec((None, w_pc.shape[1], bn), lambda i, j: (layer, 0, j), pipeline_mode=resident),
        ],
        out_specs=pl.BlockSpec((bm, bn), lambda i, j: (i, j)),
        compiler_params=_cparams(2),
        name="merge",
    )(ya, yb, yc, proj, proj, proj, w_pa, w_pb, w_pc)


def _outproj_kernel(m_ref, w_ref, h_ref, o_ref):
    o_ref[...] = h_ref[...] + jnp.dot(m_ref[...], w_ref[...], preferred_element_type=F32)


def _outproj(merged, w_out, h, layer, *, bm, bn):
    t, d = h.shape
    return pl.pallas_call(
        _outproj_kernel,
        out_shape=jax.ShapeDtypeStruct((t, d), F32),
        grid=(t // bm, d // bn),
        in_specs=[
            pl.BlockSpec((bm, merged.shape[1]), lambda i, j: (i, 0)),
            pl.BlockSpec((None, merged.shape[1], bn), lambda i, j: (layer, 0, j),
                         pipeline_mode=pl.Buffered(1) if bn == d else None),
            pl.BlockSpec((bm, bn), lambda i, j: (i, j)),
        ],
        out_specs=pl.BlockSpec((bm, bn), lambda i, j: (i, j)),
        compiler_params=_cparams(2),
        name="outproj",
    )(merged, w_out, h)


def _final_norm_kernel(x_ref, g_ref, o_ref):
    o_ref[...] = _rmsnorm_f32(x_ref[...], g_ref[...])


def _final_norm(h, g, *, bm):
    t, d = h.shape
    return pl.pallas_call(
        _final_norm_kernel,
        out_shape=jax.ShapeDtypeStruct((t, d), F32),
        grid=(t // bm,),
        in_specs=[pl.BlockSpec((bm, d), lambda i: (i, 0)), pl.BlockSpec((1, d), lambda i: (0, 0))],
        out_specs=pl.BlockSpec((bm, d), lambda i: (i, 0)),
        compiler_params=_cparams(1),
        name="final_norm",
    )(h, g)


def kernel(x, rel_bias, final_norm, ffn1_norm, ffn1_wi, ffn1_wo, mix_norm, w_in, sgu_ln_g, sgu_ln_b, sgu_w, sgu_b, diff_lambda, diff_subln, conv_w, conv_b, dt_bias, a_log, d_skip, ssm_norm, w_pa, w_pb, w_pc, w_out, ffn2_norm, ffn2_wi, ffn2_wo):
    batch, seq, d = x.shape
    depth = ffn1_wi.shape[0]
    t = batch * seq
    heads_c = dt_bias.shape[1]
    a_width = sgu_ln_g.shape[1]
    c_inner = ssm_norm.shape[1]
    conv_ch = conv_w.shape[2]
    d_ff = ffn1_wo.shape[1]

    bm = min(1024, t)
    bf = 512
    bq = 512
    bk = 512
    assert t % bm == 0 and d_ff % bf == 0 and seq % bk == 0 and bk % bq == 0 and bq % LANES == 0
    assert w_in.shape[2] == OFF_GATE + heads_c + 3 * d
    assert (OFF_AV, OFF_Q, OFF_Z, OFF_XBC, OFF_GATE) == (a_width, 2 * a_width, 5 * a_width, 5 * a_width + c_inner,
                                                         5 * a_width + c_inner + conv_ch)

    w_src = jnp.swapaxes(w_in, 1, 2)
    w_t = w_src[0].astype(BF16)
    wi1, wo1 = ffn1_wi[0].astype(BF16), ffn1_wo[0].astype(BF16)
    wi2, wo2 = ffn2_wi[0].astype(BF16), ffn2_wo[0].astype(BF16)
    wpa, wpb, wpc, wout = w_pa.astype(BF16), w_pb.astype(BF16), w_pc.astype(BF16), w_out.astype(BF16)

    bsb = jnp.repeat(jnp.swapaxes(sgu_b, 1, 2), a_width // A_GROUPS, axis=2)
    pad_h = ((0, 0), (0, LANES - heads_c))
    dtb = jnp.pad(dt_bias, pad_h)
    alog = jnp.pad(a_log, pad_h)
    dsk_x = jnp.repeat(d_skip, C_HEAD_DIM, axis=1)
    expand = np.zeros((LANES, c_inner), np.float32)
    expand[np.arange(c_inner) // C_HEAD_DIM, np.arange(c_inner)] = 1.0
    expand = jnp.asarray(expand, BF16)
    bias_tiles = _bias_tiles(rel_bias, _bucket_tiles(bq, bk))
    subln_t = jnp.broadcast_to(diff_subln[:, :, None], diff_subln.shape + (bq,))

    h = x.reshape(t, d)
    for l in range(depth):
        lam_init = 0.8 - 0.6 * math.exp(-0.3 * l)
        nxt = min(l + 1, depth - 1)
        h, wi1, wo1 = _ffn(h, ffn1_norm[l][None], wi1, wo1, ffn1_wi, ffn1_wo, nxt, bm=bm, bf=bf)
        proj, dt_raw, w_t = _inproj(h, mix_norm[l][None], w_t, w_src, nxt, bm=bm, bn=2048, n_dt=heads_c)
        ya = _sgu(proj, sgu_ln_g[l][None], sgu_ln_b[l][None], sgu_w[l], bsb[l], rows=4 * CHUNK)
        yb = _attn(proj, bias_tiles, diff_lambda[l], subln_t[l],
                   batch=batch, seq=seq, bq=bq, hpb=2, lam_init=lam_init)
        yc = _mamba(proj, dt_raw, conv_w[l], conv_b[l][None], dtb[l][None], alog[l][None],
                    dsk_x[l][None], ssm_norm[l][None], expand, batch=batch, seq=seq, rows=4 * CHUNK)
        merged = _merge(ya, yb, yc, proj, wpa, wpb, wpc, l, bm=bm // 2, bn=d)
        h = _outproj(merged, wout, h, l, bm=bm, bn=d)
        h, wi2, wo2 = _ffn(h, ffn2_norm[l][None], wi2, wo2, ffn2_wi, ffn2_wo, nxt, bm=bm, bf=bf)
    return _final_norm(h, final_norm[None], bm=bm).reshape(batch, seq, d)
```

```python
import functools
import math

import numpy as np
import jax
import jax.numpy as jnp
from jax import lax
from jax.experimental import pallas as pl
from jax.experimental.pallas import tpu as pltpu

F32 = jnp.float32
BF16 = jnp.bfloat16

EPS = 1e-6
CHUNK = 128
LANES = 128
BF16_ROWS = 16
A_GROUPS = 8
B_HEAD_DIM = 64
C_HEAD_DIM = 64
C_GROUPS = 4
C_STATE = 128
C_CONV = 4
REL_BUCKETS = 32
REL_MAX_DIST = 128
REL_EXACT = REL_BUCKETS // 2
NEG_BIG = -1e30
LOG2E = math.log2(math.e)
VMEM_LIMIT_V7X = 56 * 1024 * 1024
BIG_VMEM_LIMIT_V7X = 60 * 1024 * 1024

OFF_AU, OFF_AV, OFF_Q, OFF_K, OFF_V, OFF_Z, OFF_XBC, OFF_GATE = 0, 1024, 2048, 3072, 4096, 5120, 7168, 10240
SPLIT_W = 1024


def _cparams(n_axes, vmem_limit=VMEM_LIMIT_V7X):
    return pltpu.CompilerParams(dimension_semantics=("arbitrary",) * n_axes, vmem_limit_bytes=vmem_limit)


def _rmsnorm_f32(x, g):
    ms = jnp.mean(x * x, axis=-1, keepdims=True)
    return (x * lax.rsqrt(ms + EPS)) * g


def _ffn_kernel(x_ref, g_ref, wg_ref, wu_ref, wo_ref, wi_src_ref, wo_src_ref, o_ref, wi_cast_ref, wo_cast_ref,
                xn_ref):
    j = pl.program_id(1)

    @pl.when(j == 0)
    def _():
        x = x_ref[...]
        xn_ref[...] = _rmsnorm_f32(x, g_ref[...]).astype(BF16)
        o_ref[...] = x

    xn = xn_ref[...]
    gate = jnp.dot(xn, wg_ref[...], preferred_element_type=F32)
    up = jnp.dot(xn, wu_ref[...], preferred_element_type=F32)
    act = ((0.5 * gate) * jax.nn.sigmoid(gate) * up).astype(BF16)
    o_ref[...] += jnp.dot(act, wo_ref[...], preferred_element_type=F32)
    wi_cast_ref[...] = wi_src_ref[...].astype(BF16)
    wo_cast_ref[...] = wo_src_ref[...].astype(BF16)


def _ffn(h, g, wi, wo, wi_f32, wo_f32, cast_layer, *, bm, bf):
    t, d = h.shape
    d_ff = wo.shape[0]
    ni, nj = t // bm, d_ff // bf
    wi_tile = (d // ni, 2 * d_ff // nj)
    wo_tile = (d_ff // nj, d // ni)
    assert d % ni == 0 and wi_tile[0] % BF16_ROWS == 0 and wi_tile[1] % LANES == 0 and wo_tile[1] % LANES == 0
    return pl.pallas_call(
        _ffn_kernel,
        out_shape=(jax.ShapeDtypeStruct((t, d), F32), jax.ShapeDtypeStruct(wi.shape, BF16),
                   jax.ShapeDtypeStruct(wo.shape, BF16)),
        grid=(ni, nj),
        in_specs=[
            pl.BlockSpec((bm, d), lambda i, j: (i, 0)),
            pl.BlockSpec((1, d), lambda i, j: (0, 0)),
            pl.BlockSpec((d, bf), lambda i, j: (0, j)),
            pl.BlockSpec((d, bf), lambda i, j: (0, j + nj)),
            pl.BlockSpec((bf, d), lambda i, j: (j, 0)),
            pl.BlockSpec((None,) + wi_tile, lambda i, j: (cast_layer, i, j)),
            pl.BlockSpec((None,) + wo_tile, lambda i, j: (cast_layer, j, i)),
        ],
        out_specs=(pl.BlockSpec((bm, d), lambda i, j: (i, 0)),
                   pl.BlockSpec(wi_tile, lambda i, j: (i, j)),
                   pl.BlockSpec(wo_tile, lambda i, j: (j, i))),
        scratch_shapes=[pltpu.VMEM((bm, d), BF16)],
        compiler_params=_cparams(2, vmem_limit=BIG_VMEM_LIMIT_V7X),
        name="ffn",
    )(h, g, wi, wi, wo, wi_f32, wo_f32)


def _inproj_kernel(x_ref, g_ref, w_ref, wdt_ref, w_src_ref, o_ref, dt_ref, w_cast_ref, xn_ref, *, nj_plain):
    j = pl.program_id(1)
    nt = (((1,), (1,)), ((), ()))

    @pl.when(j == 0)
    def _():
        xn = _rmsnorm_f32(x_ref[...], g_ref[...]).astype(BF16)
        xn_ref[...] = xn
        dt_ref[...] = lax.dot_general(xn, wdt_ref[...], nt, preferred_element_type=F32)

    @pl.when(j < nj_plain)
    def _():
        o_ref[...] = lax.dot_general(xn_ref[...], w_ref[...], nt, preferred_element_type=F32).astype(BF16)
        w_cast_ref[...] = w_src_ref[...].astype(BF16)

    @pl.when(j >= nj_plain)
    def _():
        o_ref[...] = jax.nn.sigmoid(
            lax.dot_general(xn_ref[...], w_ref[...], nt, preferred_element_type=F32)).astype(BF16)
        w_cast_ref[...] = w_src_ref[...].astype(BF16)


def _inproj(h, g, w_t, w_src, cast_layer, *, bm, bn, n_dt):
    t, d = h.shape
    n_in = w_t.shape[0]
    n = n_in - n_dt
    ni, nj = t // bm, n // bn
    nj_plain = OFF_GATE // bn
    assert OFF_GATE % bn == 0 and OFF_GATE % LANES == 0 and n % bn == 0 and n_dt <= LANES and n_dt % BF16_ROWS == 0
    n_cast = max(c for c in range(1, ni * nj + 1) if n_in % c == 0 and (n_in // c) % BF16_ROWS == 0)
    cast_tile = lambda i, j: jnp.minimum(i * nj + j, n_cast - 1)
    return pl.pallas_call(
        functools.partial(_inproj_kernel, nj_plain=nj_plain),
        out_shape=(jax.ShapeDtypeStruct((t, n), BF16), jax.ShapeDtypeStruct((t, LANES), F32),
                   jax.ShapeDtypeStruct((n_in, d), BF16)),
        grid=(ni, nj),
        in_specs=[
            pl.BlockSpec((bm, d), lambda i, j: (i, 0)),
            pl.BlockSpec((1, d), lambda i, j: (0, 0)),
            pl.BlockSpec((pl.Element(bn), pl.Element(d)),
                         lambda i, j: (pl.multiple_of(j * bn + jnp.where(j >= nj_plain, n_dt, 0), BF16_ROWS), 0)),
            pl.BlockSpec((LANES, d), lambda i, j: (OFF_GATE // LANES, 0)),
            pl.BlockSpec((None, n_in // n_cast, d), lambda i, j: (cast_layer, cast_tile(i, j), 0)),
        ],
        out_specs=(pl.BlockSpec((bm, bn), lambda i, j: (i, j)),
                   pl.BlockSpec((bm, LANES), lambda i, j: (i, 0)),
                   pl.BlockSpec((n_in // n_cast, d), lambda i, j: (cast_tile(i, j), 0))),
        scratch_shapes=[pltpu.VMEM((bm, d), BF16)],
        compiler_params=_cparams(2, vmem_limit=BIG_VMEM_LIMIT_V7X),
        name="inproj",
    )(h, g, w_t, w_t, w_src)


def _sgu_kernel(u_ref, v_ref, lng_ref, lnb_ref, w_ref, bsb_ref, o_ref):
    rows = u_ref.shape[0]
    u = jax.nn.gelu(u_ref[...].astype(F32))
    v = jax.nn.gelu(v_ref[...].astype(F32))
    mu = jnp.mean(v, axis=-1, keepdims=True)
    var = jnp.mean(jnp.square(v - mu), axis=-1, keepdims=True)
    vn = (((v - mu) * lax.rsqrt(var + EPS)) * lng_ref[...] + lnb_ref[...]).astype(BF16)
    r_i = lax.broadcasted_iota(jnp.int32, (CHUNK, CHUNK), 0)
    c_i = lax.broadcasted_iota(jnp.int32, (CHUNK, CHUNK), 1)
    tri = r_i >= c_i
    for g in range(A_GROUPS):
        wm = jnp.where(tri, w_ref[g], 0.0).astype(BF16)
        cs = slice(g * LANES, (g + 1) * LANES)
        for c in range(rows // CHUNK):
            rs = slice(c * CHUNK, (c + 1) * CHUNK)
            mixed = jnp.dot(wm, vn[rs, cs], preferred_element_type=F32) + bsb_ref[:, cs]
            o_ref[rs, cs] = (u[rs, cs] * mixed).astype(BF16)


def _t5_bucket_np(dist):
    n = np.maximum(dist, 0)
    nf = np.maximum(n, 1).astype(np.float64)
    large = REL_EXACT + (np.log(nf / REL_EXACT) / math.log(REL_MAX_DIST / REL_EXACT)
                         * (REL_BUCKETS - REL_EXACT)).astype(np.int32)
    large = np.minimum(large, REL_BUCKETS - 1)
    return np.where(n < REL_EXACT, n, large).astype(np.int32)


def _bucket_tiles(bq, bk):
    last_bucket_from = int(np.argmax(_t5_bucket_np(np.arange(4 * REL_MAX_DIST)) == REL_BUCKETS - 1))
    n_tiles = -(-(bk - 1 + last_bucket_from) // bq)
    c = np.arange(bk)[:, None]
    r = np.arange(bq)[None, :]
    tiles = []
    for t in range(n_tiles):
        dist = t * bq + r - c
        tiles.append(np.where(dist >= 0, _t5_bucket_np(dist), REL_BUCKETS))
    assert _t5_bucket_np(np.array([n_tiles * bq - (bk - 1)]))[0] == REL_BUCKETS - 1
    return np.stack(tiles).astype(np.int32)


BIAS_ROWS = 64


def _bias_tiles_kernel(rb_ref, bkt_ref, o_ref, *, present):
    hd = pl.program_id(0)
    n_bias, bk, bq = o_ref.shape
    for t in range(n_bias - 1):
        for rc in range(bk // BIAS_ROWS):
            rs = slice(rc * BIAS_ROWS, (rc + 1) * BIAS_ROWS)
            bt = bkt_ref[t, rs, :]
            tile = jnp.where(bt == REL_BUCKETS, NEG_BIG, 0.0).astype(F32)
            for b in present[t][rc]:
                tile = jnp.where(bt == b, rb_ref[b, hd] * LOG2E, tile)
            o_ref[t, rs, :] = tile
    o_ref[n_bias - 1] = jnp.full((bk, bq), rb_ref[REL_BUCKETS - 1, hd] * LOG2E, F32)


def _bias_tiles(rel_bias, buckets_np):
    n_tiles, bk, bq = buckets_np.shape
    heads = rel_bias.shape[1]
    present = tuple(tuple(tuple(int(b) for b in np.unique(buckets_np[t, rc * BIAS_ROWS:(rc + 1) * BIAS_ROWS])
                                if b < REL_BUCKETS)
                          for rc in range(bk // BIAS_ROWS)) for t in range(n_tiles))
    return pl.pallas_call(
        functools.partial(_bias_tiles_kernel, present=present),
        out_shape=jax.ShapeDtypeStruct((heads, n_tiles + 1, bk, bq), F32),
        grid=(heads,),
        in_specs=[pl.BlockSpec(memory_space=pltpu.SMEM),
                  pl.BlockSpec((n_tiles, bk, bq), lambda h: (0, 0, 0))],
        out_specs=pl.BlockSpec((None, n_tiles + 1, bk, bq), lambda h: (h, 0, 0, 0)),
        compiler_params=_cparams(1),
        name="bias_tiles",
    )(rel_bias, jnp.asarray(buckets_np))


def _attn_kernel(lam_ref, sub_ref, bias_ref, q_ref, k_ref, v_ref, o_ref,
                 vt_ref, qqt_ref, s_ref, p_ref, cm_ref, m_ref, al_ref, acc_ref, *, lam_init):
    hw = 2 * B_HEAD_DIM
    hpb, bk = s_ref.shape[0], s_ref.shape[1]
    bq = s_ref.shape[2] // 2
    seq = k_ref.shape[0]
    nq = seq // bq
    n_bias = bias_ref.shape[1]
    cols = [slice(hh * hw, (hh + 1) * hw) for hh in range(hpb)]
    last_block = lambda i: (i * bq) // bk
    n_pairs = sum(last_block(i) + 1 for i in range(nq))

    pad_row = lax.broadcasted_iota(jnp.int32, (vt_ref.shape[2] - hw, bk), 0)
    ones_row = jnp.where(pad_row == 0, 1.0, 0.0).astype(BF16)
    dim = lax.broadcasted_iota(jnp.int32, (hw, bq), 0)
    for hh in range(hpb):
        for c in range(seq // LANES):
            blk, off = divmod(c * LANES, bk)
            vt_ref[hh, blk, 0:hw, off:off + LANES] = (
                v_ref[c * LANES:(c + 1) * LANES, cols[hh]].astype(F32).T.astype(BF16))
        for blk in range(seq // bk):
            vt_ref[hh, blk, hw:, :] = ones_row
        for i in range(nq):
            qt = (q_ref[i * bq:(i + 1) * bq, cols[hh]].astype(F32) * (B_HEAD_DIM ** -0.5 * LOG2E)).T
            qqt_ref[hh, i] = jnp.concatenate([jnp.where(dim < B_HEAD_DIM, qt, 0.0),
                                              jnp.where(dim >= B_HEAD_DIM, qt, 0.0)], axis=1).astype(BF16)
    m_ref[...] = jnp.full(m_ref.shape, NEG_BIG, F32)
    al_ref[...] = jnp.ones(al_ref.shape, F32)
    acc_ref[...] = jnp.zeros(acc_ref.shape, F32)
    p_ref[...] = jnp.zeros(p_ref.shape, BF16)

    far_tile = n_bias - 1
    bias_tile = lambda i, j: jnp.minimum((i * bq - j * bk) // bq, far_tile)
    far_bias = [jnp.concatenate([bias_ref[hh, far_tile, 0:1, :]] * 2, axis=1) for hh in range(hpb)]

    def scores(hh, i, j, far):
        kb = k_ref[pl.ds(pl.multiple_of(j * bk, bk), bk), cols[hh]]
        s = jnp.dot(kb, qqt_ref[hh, i], preferred_element_type=F32)
        if far:
            s_ref[hh] = s
            cm_ref[hh] = jnp.max(s, axis=0, keepdims=True) + far_bias[hh]
        else:
            bias = bias_ref[hh, bias_tile(i, j)]
            s = s + jnp.concatenate([bias, bias], axis=1)
            s_ref[hh] = s
            cm_ref[hh] = jnp.max(s, axis=0, keepdims=True)

    def probabilities(hh, i, far):
        m_prev = m_ref[hh, i]
        m_next = jnp.maximum(m_prev, cm_ref[hh])
        m_ref[hh, i] = m_next
        al_ref[hh] = jnp.exp2(m_prev - m_next)
        pending = jnp.where(far == 1, far_bias[hh], 0.0)
        p_ref[hh] = jnp.exp2(s_ref[hh] - (m_next - pending)).astype(BF16)

    def weighted_values(hh, i, j):
        acc_ref[hh, i] = al_ref[hh] * acc_ref[hh, i] + jnp.dot(vt_ref[hh, j], p_ref[hh],
                                                                preferred_element_type=F32)

    for hh in range(hpb):
        scores(hh, 0, 0, False)

    def body(t, carry):
        ip, jp, ic, jc, cur_far = carry
        wrap = jc >= last_block(ic)
        i_next = jnp.where(wrap, ic + 1, ic)
        j_next = jnp.where(wrap, 0, jc + 1)
        next_far = (bias_tile(i_next, j_next) == far_tile).astype(jnp.int32)

        def step(far):
            for hh in range(hpb):
                weighted_values(hh, ip, jp)
            for hh in range(hpb):
                probabilities(hh, ic, cur_far)
            for hh in range(hpb):
                scores(hh, i_next, j_next, far)

        pl.when(next_far == 1)(functools.partial(step, True))
        pl.when(next_far == 0)(functools.partial(step, False))
        return ic, jc, i_next, j_next, next_far

    zero = jnp.int32(0)
    ip, jp, ic, jc, cur_far = lax.fori_loop(0, n_pairs - 1, body, (zero, zero, zero, zero, zero))
    for hh in range(hpb):
        weighted_values(hh, ip, jp)
        probabilities(hh, ic, cur_far)
    for hh in range(hpb):
        weighted_values(hh, ic, jc)

    lam_p = lam_ref[...]
    lam = (jnp.exp(jnp.sum(lam_p[0:1] * lam_p[1:2], axis=1, keepdims=True))
           - jnp.exp(jnp.sum(lam_p[2:3] * lam_p[3:4], axis=1, keepdims=True)) + lam_init)
    for hh in range(hpb):
        for i in range(nq):
            o = acc_ref[hh, i, 0:hw, :] * (1.0 / acc_ref[hh, i, hw:hw + 1, :])
            attn = o[:, :bq] - lam * o[:, bq:]
            ms = jnp.mean(attn * attn, axis=0, keepdims=True)
            y = ((attn * lax.rsqrt(ms + EPS)) * sub_ref[...]) * (1.0 - lam_init)
            o_ref[i * bq:(i + 1) * bq, cols[hh]] = y.T.astype(BF16)


def _attn(proj, bias_tiles, lam_p, subln_t, *, batch, seq, bq, hpb, lam_init):
    t = proj.shape[0]
    hw = 2 * B_HEAD_DIM
    heads, n_bias, bk, _ = bias_tiles.shape
    nq = seq // bq
    bw = hpb * hw
    return pl.pallas_call(
        functools.partial(_attn_kernel, lam_init=lam_init),
        out_shape=jax.ShapeDtypeStruct((t, heads * hw), BF16),
        grid=(batch, heads // hpb),
        in_specs=[
            pl.BlockSpec((4, B_HEAD_DIM), lambda b, h: (0, 0)),
            pl.BlockSpec((hw, bq), lambda b, h: (0, 0)),
            pl.BlockSpec((hpb, n_bias, bk, bq), lambda b, h: (h, 0, 0, 0)),
            pl.BlockSpec((seq, bw), lambda b, h: (b, OFF_Q // bw + h)),
            pl.BlockSpec((seq, bw), lambda b, h: (b, OFF_K // bw + h)),
            pl.BlockSpec((seq, bw), lambda b, h: (b, OFF_V // bw + h)),
        ],
        out_specs=pl.BlockSpec((seq, bw), lambda b, h: (b, h)),
        scratch_shapes=[pltpu.VMEM((hpb, seq // bk, hw + BF16_ROWS, bk), BF16),
                        pltpu.VMEM((hpb, nq, hw, 2 * bq), BF16),
                        pltpu.VMEM((hpb, bk, 2 * bq), F32),
                        pltpu.VMEM((hpb, bk, 2 * bq), BF16),
                        pltpu.VMEM((hpb, 1, 2 * bq), F32),
                        pltpu.VMEM((hpb, nq, 1, 2 * bq), F32),
                        pltpu.VMEM((hpb, 1, 2 * bq), F32),
                        pltpu.VMEM((hpb, nq, hw + BF16_ROWS, 2 * bq), F32)],
        compiler_params=_cparams(2, vmem_limit=BIG_VMEM_LIMIT_V7X),
        name="diff_attn",
    )(lam_p, subln_t, bias_tiles, proj, proj, proj)


def _split_dot(x, e_bf16, passes):
    out = None
    r = x
    for _ in range(passes):
        hi = r.astype(BF16)
        part = jnp.dot(hi, e_bf16, preferred_element_type=F32)
        out = part if out is None else out + part
        r = r - hi.astype(F32)
    return out


def _mamba_kernel(*refs, n_z, n_xbc):
    z_refs, xbc_refs = refs[:n_z], refs[n_z:n_z + n_xbc]
    (dt_ref, cw_ref, cb_ref, dtb_ref, alog_ref, dsk_ref, ng_ref, e_ref, o_ref, tail_ref, st_ref) = refs[n_z + n_xbc:]
    z_rows = lambda sl: jnp.concatenate([r[sl, :] for r in z_refs], axis=1)
    xbc_rows = lambda sl: jnp.concatenate([r[sl, :] for r in xbc_refs], axis=1)
    rows, c_inner = o_ref.shape
    conv_ch = cw_ref.shape[1]
    heads = c_inner // C_HEAD_DIM
    gw = c_inner // C_GROUPS
    heads_per_group = gw // C_HEAD_DIM
    c = pl.program_id(1)

    @pl.when(c == 0)
    def _():
        tail_ref[...] = jnp.zeros_like(tail_ref)
        st_ref[...] = jnp.zeros_like(st_ref)

    r_i = lax.broadcasted_iota(jnp.int32, (CHUNK, CHUNK), 0)
    c_i = lax.broadcasted_iota(jnp.int32, (CHUNK, CHUNK), 1)
    tri = r_i >= c_i
    lower = tri.astype(F32)
    upper = (r_i <= c_i).astype(F32)
    shift = jnp.concatenate([(r_i - c_i == d).astype(BF16) for d in range(C_CONV - 1, 0, -1)], axis=0)
    row8 = lax.broadcasted_iota(jnp.int32, (8, conv_ch), 0)
    is_head = lax.broadcasted_iota(jnp.int32, (CHUNK, LANES), 1) < heads
    lane_lo = lax.broadcasted_iota(jnp.int32, (CHUNK, LANES), 1) < C_HEAD_DIM
    neg_a = -jnp.exp(alog_ref[...]) * LOG2E
    e = e_ref[...]
    hp = lax.Precision.HIGHEST
    prev = st_ref[...]

    for u in range(rows // CHUNK):
        rs = slice(u * CHUNK, (u + 1) * CHUNK)
        x_cur = xbc_rows(rs)
        delayed = jnp.dot(shift, x_cur, preferred_element_type=F32)
        if u == 0:
            tail = tail_ref[8:16, :]
        else:
            tail = xbc_rows(slice(u * CHUNK - 16, u * CHUNK)).astype(F32)[8:16, :]
        conv = cb_ref[...] + cw_ref[C_CONV - 1:C_CONV, :] * x_cur.astype(F32)
        conv_top = jnp.zeros_like(tail)
        for k in range(C_CONV - 1):
            d = C_CONV - 1 - k
            conv = conv + cw_ref[k:k + 1, :] * delayed[k * CHUNK:(k + 1) * CHUNK]
            conv_top = conv_top + cw_ref[k:k + 1, :] * jnp.where(row8 < d, pltpu.roll(tail, d, 0), 0.0)
        conv = jnp.concatenate([conv[0:8] + conv_top, conv[8:]], axis=0)
        xbc = conv * jax.nn.sigmoid(conv)
        xs = xbc[:, :c_inner]
        bm = xbc[:, c_inner:c_inner + C_GROUPS * C_STATE]
        cm = xbc[:, c_inner + C_GROUPS * C_STATE:]

        dt = jnp.where(is_head, jax.nn.softplus(dt_ref[rs, :] + dtb_ref[...]), 0.0)
        a = dt * neg_a
        a_cs = jnp.dot(lower, a, precision=hp, preferred_element_type=F32)
        a_cs_t = jnp.dot(a.T, upper, precision=hp, preferred_element_type=F32)

        dt_x = _split_dot(dt, e, 2)
        acs_x = _split_dot(a_cs, e, 3)
        last_x = acs_x[CHUNK - 1:CHUNK, :]
        xd = xs * dt_x
        xd_b = xd.astype(BF16)
        xdd_b = (xd * jnp.exp2(last_x - acs_x)).astype(BF16)
        prev_b = prev.astype(BF16)
        eacs = jnp.exp2(acs_x)

        y_parts = []
        st_parts = []
        for g in range(C_GROUPS):
            gs = slice(g * gw, (g + 1) * gw)
            b_g = bm[:, g * C_STATE:(g + 1) * C_STATE]
            c_g = cm[:, g * C_STATE:(g + 1) * C_STATE].astype(BF16)
            cb = lax.dot_general(c_g, b_g.astype(BF16), (((1,), (1,)), ((), ())), preferred_element_type=F32)
            st_parts.append(jnp.dot(b_g.T.astype(BF16), xdd_b[:, gs], preferred_element_type=F32))
            y_off = jnp.dot(c_g, prev_b[:, gs], preferred_element_type=F32) * eacs[:, gs]
            tiles = []
            for pr in range(heads_per_group // 2):
                t_idx = g * (heads_per_group // 2) + pr
                xt = xd_b[:, t_idx * LANES:(t_idx + 1) * LANES]
                res = []
                for half in range(2):
                    hd = 2 * t_idx + half
                    seg = a_cs[:, hd:hd + 1] - a_cs_t[hd:hd + 1, :]
                    decay = jnp.exp2(jnp.where(tri, seg, NEG_BIG))
                    res.append(jnp.dot((cb * decay).astype(BF16), xt, preferred_element_type=F32))
                tiles.append(jnp.where(lane_lo, res[0], res[1]))
            y_parts.append(jnp.concatenate(tiles, axis=1) + y_off)

        prev = prev * jnp.exp2(last_x) + jnp.concatenate(st_parts, axis=1)
        y = jnp.concatenate(y_parts, axis=1) + dsk_ref[...] * xs
        zf = z_rows(rs).astype(F32)
        gated = y * (zf * jax.nn.sigmoid(zf))
        o_ref[rs, :] = _rmsnorm_f32(gated, ng_ref[...]).astype(BF16)

    st_ref[...] = prev
    tail_ref[...] = xbc_rows(slice(rows - 16, rows)).astype(F32)


def _mamba(proj, dt_raw, conv_w, conv_b, dt_bias, a_log, d_skip_x, norm_g, expand, *, batch, seq, rows):
    t = proj.shape[0]
    c_inner = norm_g.shape[1]
    conv_ch = conv_w.shape[1]
    nc = seq // rows
    row = lambda b, c: (b * nc + c, 0)
    const = lambda b, c: (0, 0)
    n_z, n_xbc = c_inner // SPLIT_W, conv_ch // SPLIT_W
    col_block = lambda k: pl.BlockSpec((rows, SPLIT_W), lambda b, c: (b * nc + c, k))
    return pl.pallas_call(
        functools.partial(_mamba_kernel, n_z=n_z, n_xbc=n_xbc),
        out_shape=jax.ShapeDtypeStruct((t, c_inner), BF16),
        grid=(batch, nc),
        in_specs=[col_block(OFF_Z // SPLIT_W + k) for k in range(n_z)]
        + [col_block(OFF_XBC // SPLIT_W + k) for k in range(n_xbc)] + [
            pl.BlockSpec((rows, LANES), row),
            pl.BlockSpec((C_CONV, conv_ch), const),
            pl.BlockSpec((1, conv_ch), const),
            pl.BlockSpec((1, LANES), const),
            pl.BlockSpec((1, LANES), const),
            pl.BlockSpec((1, c_inner), const),
            pl.BlockSpec((1, c_inner), const),
            pl.BlockSpec((LANES, c_inner), const),
        ],
        out_specs=pl.BlockSpec((rows, c_inner), row),
        scratch_shapes=[pltpu.VMEM((16, conv_ch), F32), pltpu.VMEM((C_STATE, c_inner), F32)],
        compiler_params=_cparams(2),
        name="mamba_ssd",
    )(*([proj] * (n_z + n_xbc)), dt_raw, conv_w, conv_b, dt_bias, a_log, d_skip_x, norm_g, expand)


def _merge_kernel(u_ref, v_ref, lng_ref, lnb_ref, sw_ref, bsb_ref, yb_ref, yc_ref, g0_ref, g1_ref, g2_ref,
                  wa_ref, wb_ref, wc_ref, o_ref, ya_ref):
    pb = jnp.dot(yb_ref[...], wb_ref[...], preferred_element_type=F32)
    pc = jnp.dot(yc_ref[...], wc_ref[...], preferred_element_type=F32)
    _sgu_kernel(u_ref, v_ref, lng_ref, lnb_ref, sw_ref, bsb_ref, ya_ref)
    pa = jnp.dot(ya_ref[...], wa_ref[...], preferred_element_type=F32)
    o_ref[...] = (g0_ref[...].astype(F32) * pa + g1_ref[...].astype(F32) * pb
                  + g2_ref[...].astype(F32) * pc).astype(BF16)


def _merge(ln_g, ln_b, w_s, bsb, yb, yc, proj, w_pa, w_pb, w_pc, layer, *, bm, bn):
    t = yb.shape[0]
    d = w_pa.shape[2]
    aw = ln_g.shape[1]
    gate_spec = lambda k: pl.BlockSpec((bm, bn), lambda i, j: (i, (OFF_GATE + k * d) // bn + j))
    resident = pl.Buffered(1) if bn == d else None
    return pl.pallas_call(
        _merge_kernel,
        out_shape=jax.ShapeDtypeStruct((t, d), BF16),
        grid=(t // bm, d // bn),
        in_specs=[
            pl.BlockSpec((bm, aw), lambda i, j: (i, OFF_AU // aw)),
            pl.BlockSpec((bm, aw), lambda i, j: (i, OFF_AV // aw)),
            pl.BlockSpec((1, aw), lambda i, j: (0, 0)),
            pl.BlockSpec((1, aw), lambda i, j: (0, 0)),
            pl.BlockSpec((A_GROUPS, CHUNK, CHUNK), lambda i, j: (0, 0, 0)),
            pl.BlockSpec((CHUNK, aw), lambda i, j: (0, 0)),
            pl.BlockSpec((bm, yb.shape[1]), lambda i, j: (i, 0)),
            pl.BlockSpec((bm, yc.shape[1]), lambda i, j: (i, 0)),
            gate_spec(0), gate_spec(1), gate_spec(2),
            pl.BlockSpec((None, w_pa.shape[1], bn), lambda i, j: (layer, 0, j), pipeline_mode=resident),
            pl.BlockSpec((None, w_pb.shape[1], bn), lambda i, j: (layer, 0, j), pipeline_mode=resident),
            pl.BlockSpec((None, w_pc.shape[1], bn), lambda i, j: (layer, 0, j), pipeline_mode=resident),
        ],
        out_specs=pl.BlockSpec((bm, bn), lambda i, j: (i, j)),
        scratch_shapes=[pltpu.VMEM((bm, aw), BF16)],
        compiler_params=_cparams(2, vmem_limit=BIG_VMEM_LIMIT_V7X),
        name="merge",
    )(proj, proj, ln_g, ln_b, w_s, bsb, yb, yc, proj, proj, proj, w_pa, w_pb, w_pc)


def _outproj_kernel(m_ref, w_ref, h_ref, o_ref):
    o_ref[...] = h_ref[...] + jnp.dot(m_ref[...], w_ref[...], preferred_element_type=F32)


def _outproj(merged, w_out, h, layer, *, bm, bn):
    t, d = h.shape
    return pl.pallas_call(
        _outproj_kernel,
        out_shape=jax.ShapeDtypeStruct((t, d), F32),
        grid=(t // bm, d // bn),
        in_specs=[
            pl.BlockSpec((bm, merged.shape[1]), lambda i, j: (i, 0)),
            pl.BlockSpec((None, merged.shape[1], bn), lambda i, j: (layer, 0, j),
                         pipeline_mode=pl.Buffered(1) if bn == d else None),
            pl.BlockSpec((bm, bn), lambda i, j: (i, j)),
        ],
        out_specs=pl.BlockSpec((bm, bn), lambda i, j: (i, j)),
        compiler_params=_cparams(2),
        name="outproj",
    )(merged, w_out, h)


def _final_norm_kernel(x_ref, g_ref, o_ref):
    o_ref[...] = _rmsnorm_f32(x_ref[...], g_ref[...])


def _final_norm(h, g, *, bm):
    t, d = h.shape
    return pl.pallas_call(
        _final_norm_kernel,
        out_shape=jax.ShapeDtypeStruct((t, d), F32),
        grid=(t // bm,),
        in_specs=[pl.BlockSpec((bm, d), lambda i: (i, 0)), pl.BlockSpec((1, d), lambda i: (0, 0))],
        out_specs=pl.BlockSpec((bm, d), lambda i: (i, 0)),
        compiler_params=_cparams(1),
        name="final_norm",
    )(h, g)


def kernel(x, rel_bias, final_norm, ffn1_norm, ffn1_wi, ffn1_wo, mix_norm, w_in, sgu_ln_g, sgu_ln_b, sgu_w, sgu_b, diff_lambda, diff_subln, conv_w, conv_b, dt_bias, a_log, d_skip, ssm_norm, w_pa, w_pb, w_pc, w_out, ffn2_norm, ffn2_wi, ffn2_wo):
    batch, seq, d = x.shape
    depth = ffn1_wi.shape[0]
    t = batch * seq
    heads_c = dt_bias.shape[1]
    a_width = sgu_ln_g.shape[1]
    c_inner = ssm_norm.shape[1]
    conv_ch = conv_w.shape[2]
    d_ff = ffn1_wo.shape[1]

    bm = min(1024, t)
    bf = 512
    bq = 512
    bk = 512
    assert t % bm == 0 and d_ff % bf == 0 and seq % bk == 0 and bk % bq == 0 and bq % LANES == 0
    assert w_in.shape[2] == OFF_GATE + heads_c + 3 * d
    assert (OFF_AV, OFF_Q, OFF_Z, OFF_XBC, OFF_GATE) == (a_width, 2 * a_width, 5 * a_width, 5 * a_width + c_inner,
                                                         5 * a_width + c_inner + conv_ch)

    w_src = jnp.swapaxes(w_in, 1, 2)
    w_t = w_src[0].astype(BF16)
    wi1, wo1 = ffn1_wi[0].astype(BF16), ffn1_wo[0].astype(BF16)
    wi2, wo2 = ffn2_wi[0].astype(BF16), ffn2_wo[0].astype(BF16)
    wpa, wpb, wpc, wout = w_pa.astype(BF16), w_pb.astype(BF16), w_pc.astype(BF16), w_out.astype(BF16)

    bsb = jnp.repeat(jnp.swapaxes(sgu_b, 1, 2), a_width // A_GROUPS, axis=2)
    pad_h = ((0, 0), (0, LANES - heads_c))
    dtb = jnp.pad(dt_bias, pad_h)
    alog = jnp.pad(a_log, pad_h)
    dsk_x = jnp.repeat(d_skip, C_HEAD_DIM, axis=1)
    expand = np.zeros((LANES, c_inner), np.float32)
    expand[np.arange(c_inner) // C_HEAD_DIM, np.arange(c_inner)] = 1.0
    expand = jnp.asarray(expand, BF16)
    bias_tiles = _bias_tiles(rel_bias, _bucket_tiles(bq, bk))
    subln_t = jnp.broadcast_to(diff_subln[:, :, None], diff_subln.shape + (bq,))

    h = x.reshape(t, d)
    for l in range(depth):
        lam_init = 0.8 - 0.6 * math.exp(-0.3 * l)
        nxt = min(l + 1, depth - 1)
        h, wi1, wo1 = _ffn(h, ffn1_norm[l][None], wi1, wo1, ffn1_wi, ffn1_wo, nxt, bm=bm, bf=bf)
        proj, dt_raw, w_t = _inproj(h, mix_norm[l][None], w_t, w_src, nxt, bm=bm, bn=2048, n_dt=heads_c)
        yb = _attn(proj, bias_tiles, diff_lambda[l], subln_t[l],
                   batch=batch, seq=seq, bq=bq, hpb=2, lam_init=lam_init)
        yc = _mamba(proj, dt_raw, conv_w[l], conv_b[l][None], dtb[l][None], alog[l][None],
                    dsk_x[l][None], ssm_norm[l][None], expand, batch=batch, seq=seq, rows=4 * CHUNK)
        merged = _merge(sgu_ln_g[l][None], sgu_ln_b[l][None], sgu_w[l], bsb[l], yb, yc, proj, wpa, wpb, wpc, l,
                        bm=bm // 2, bn=d)
        h = _outproj(merged, wout, h, l, bm=bm, bn=d)
        h, wi2, wo2 = _ffn(h, ffn2_norm[l][None], wi2, wo2, ffn2_wi, ffn2_wo, nxt, bm=bm, bf=bf)
    return _final_norm(h, final_norm[None], bm=bm).reshape(batch, seq, d)
```

```python
import functools
import math

import numpy as np
import jax
import jax.numpy as jnp
from jax import lax
from jax.experimental import pallas as pl
from jax.experimental.pallas import tpu as pltpu

F32 = jnp.float32
BF16 = jnp.bfloat16

EPS = 1e-6
CHUNK = 128
LANES = 128
BF16_ROWS = 16
A_GROUPS = 8
B_HEAD_DIM = 64
C_HEAD_DIM = 64
C_GROUPS = 4
C_STATE = 128
C_CONV = 4
REL_BUCKETS = 32
REL_MAX_DIST = 128
REL_EXACT = REL_BUCKETS // 2
NEG_BIG = -1e30
LOG2E = math.log2(math.e)
VMEM_LIMIT_V7X = 56 * 1024 * 1024
BIG_VMEM_LIMIT_V7X = 60 * 1024 * 1024

OFF_AU, OFF_AV, OFF_Q, OFF_K, OFF_V, OFF_Z, OFF_XBC, OFF_GATE = 0, 1024, 2048, 3072, 4096, 5120, 7168, 10240
SPLIT_W = 1024


def _cparams(n_axes, vmem_limit=VMEM_LIMIT_V7X):
    return pltpu.CompilerParams(dimension_semantics=("arbitrary",) * n_axes, vmem_limit_bytes=vmem_limit)


def _rmsnorm_f32(x, g):
    ms = jnp.mean(x * x, axis=-1, keepdims=True)
    return (x * lax.rsqrt(ms + EPS)) * g


def _ffn_kernel(x_ref, g_ref, wg_ref, wu_ref, wo_ref, wi_src_ref, wo_src_ref, o_ref, wi_cast_ref, wo_cast_ref,
                xn_ref):
    j = pl.program_id(1)

    @pl.when(j == 0)
    def _():
        x = x_ref[...]
        xn_ref[...] = _rmsnorm_f32(x, g_ref[...]).astype(BF16)
        o_ref[...] = x

    xn = xn_ref[...]
    gate = jnp.dot(xn, wg_ref[...], preferred_element_type=F32)
    up = jnp.dot(xn, wu_ref[...], preferred_element_type=F32)
    act = ((0.5 * gate) * jax.nn.sigmoid(gate) * up).astype(BF16)
    o_ref[...] += jnp.dot(act, wo_ref[...], preferred_element_type=F32)
    wi_cast_ref[...] = wi_src_ref[...].astype(BF16)
    wo_cast_ref[...] = wo_src_ref[...].astype(BF16)


def _ffn(h, g, wi, wo, wi_f32, wo_f32, cast_layer, *, bm, bf):
    t, d = h.shape
    d_ff = wo.shape[0]
    ni, nj = t // bm, d_ff // bf
    wi_tile = (d // ni, 2 * d_ff // nj)
    wo_tile = (d_ff // nj, d // ni)
    assert d % ni == 0 and wi_tile[0] % BF16_ROWS == 0 and wi_tile[1] % LANES == 0 and wo_tile[1] % LANES == 0
    return pl.pallas_call(
        _ffn_kernel,
        out_shape=(jax.ShapeDtypeStruct((t, d), F32), jax.ShapeDtypeStruct(wi.shape, BF16),
                   jax.ShapeDtypeStruct(wo.shape, BF16)),
        grid=(ni, nj),
        in_specs=[
            pl.BlockSpec((bm, d), lambda i, j: (i, 0)),
            pl.BlockSpec((1, d), lambda i, j: (0, 0)),
            pl.BlockSpec((d, bf), lambda i, j: (0, j)),
            pl.BlockSpec((d, bf), lambda i, j: (0, j + nj)),
            pl.BlockSpec((bf, d), lambda i, j: (j, 0)),
            pl.BlockSpec((None,) + wi_tile, lambda i, j: (cast_layer, i, j)),
            pl.BlockSpec((None,) + wo_tile, lambda i, j: (cast_layer, j, i)),
        ],
        out_specs=(pl.BlockSpec((bm, d), lambda i, j: (i, 0)),
                   pl.BlockSpec(wi_tile, lambda i, j: (i, j)),
                   pl.BlockSpec(wo_tile, lambda i, j: (j, i))),
        scratch_shapes=[pltpu.VMEM((bm, d), BF16)],
        compiler_params=_cparams(2, vmem_limit=BIG_VMEM_LIMIT_V7X),
        name="ffn",
    )(h, g, wi, wi, wo, wi_f32, wo_f32)


def _inproj_kernel(x_ref, g_ref, w_ref, wdt_ref, w_src_ref, o_ref, dt_ref, w_cast_ref, xn_ref, *, nj_plain):
    j = pl.program_id(1)
    nt = (((1,), (1,)), ((), ()))

    @pl.when(j == 0)
    def _():
        xn = _rmsnorm_f32(x_ref[...], g_ref[...]).astype(BF16)
        xn_ref[...] = xn
        dt_ref[...] = lax.dot_general(xn, wdt_ref[...], nt, preferred_element_type=F32)

    @pl.when(j < nj_plain)
    def _():
        o_ref[...] = lax.dot_general(xn_ref[...], w_ref[...], nt, preferred_element_type=F32).astype(BF16)
        w_cast_ref[...] = w_src_ref[...].astype(BF16)

    @pl.when(j >= nj_plain)
    def _():
        half = 0.5 * lax.dot_general(xn_ref[...], w_ref[...], nt, preferred_element_type=F32)
        o_ref[...] = (0.5 * jnp.tanh(half) + 0.5).astype(BF16)
        w_cast_ref[...] = w_src_ref[...].astype(BF16)


def _inproj(h, g, w_t, w_src, cast_layer, *, bm, bn, n_dt):
    t, d = h.shape
    n_in = w_t.shape[0]
    n = n_in - n_dt
    ni, nj = t // bm, n // bn
    nj_plain = OFF_GATE // bn
    assert OFF_GATE % bn == 0 and OFF_GATE % LANES == 0 and n % bn == 0 and n_dt <= LANES and n_dt % BF16_ROWS == 0
    n_cast = max(c for c in range(1, ni * nj + 1) if n_in % c == 0 and (n_in // c) % BF16_ROWS == 0)
    cast_tile = lambda i, j: jnp.minimum(i * nj + j, n_cast - 1)
    return pl.pallas_call(
        functools.partial(_inproj_kernel, nj_plain=nj_plain),
        out_shape=(jax.ShapeDtypeStruct((t, n), BF16), jax.ShapeDtypeStruct((t, LANES), F32),
                   jax.ShapeDtypeStruct((n_in, d), BF16)),
        grid=(ni, nj),
        in_specs=[
            pl.BlockSpec((bm, d), lambda i, j: (i, 0)),
            pl.BlockSpec((1, d), lambda i, j: (0, 0)),
            pl.BlockSpec((pl.Element(bn), pl.Element(d)),
                         lambda i, j: (pl.multiple_of(j * bn + jnp.where(j >= nj_plain, n_dt, 0), BF16_ROWS), 0)),
            pl.BlockSpec((LANES, d), lambda i, j: (OFF_GATE // LANES, 0)),
            pl.BlockSpec((None, n_in // n_cast, d), lambda i, j: (cast_layer, cast_tile(i, j), 0)),
        ],
        out_specs=(pl.BlockSpec((bm, bn), lambda i, j: (i, j)),
                   pl.BlockSpec((bm, LANES), lambda i, j: (i, 0)),
                   pl.BlockSpec((n_in // n_cast, d), lambda i, j: (cast_tile(i, j), 0))),
        scratch_shapes=[pltpu.VMEM((bm, d), BF16)],
        compiler_params=_cparams(2, vmem_limit=BIG_VMEM_LIMIT_V7X),
        name="inproj",
    )(h, g, w_t, w_t, w_src)


def _sgu_kernel(u_ref, v_ref, lng_ref, lnb_ref, w_ref, bsb_ref, o_ref):
    rows = u_ref.shape[0]
    u = jax.nn.gelu(u_ref[...].astype(F32))
    v = jax.nn.gelu(v_ref[...].astype(F32))
    mu = jnp.mean(v, axis=-1, keepdims=True)
    var = jnp.mean(jnp.square(v - mu), axis=-1, keepdims=True)
    vn = (((v - mu) * lax.rsqrt(var + EPS)) * lng_ref[...] + lnb_ref[...]).astype(BF16)
    r_i = lax.broadcasted_iota(jnp.int32, (CHUNK, CHUNK), 0)
    c_i = lax.broadcasted_iota(jnp.int32, (CHUNK, CHUNK), 1)
    tri = r_i >= c_i
    for g in range(A_GROUPS):
        wm = jnp.where(tri, w_ref[g], 0.0).astype(BF16)
        cs = slice(g * LANES, (g + 1) * LANES)
        for c in range(rows // CHUNK):
            rs = slice(c * CHUNK, (c + 1) * CHUNK)
            mixed = jnp.dot(wm, vn[rs, cs], preferred_element_type=F32) + bsb_ref[:, cs]
            o_ref[rs, cs] = (u[rs, cs] * mixed).astype(BF16)


def _sgu(proj, ln_g, ln_b, w_s, bsb, *, rows):
    t = proj.shape[0]
    aw = ln_g.shape[1]
    return pl.pallas_call(
        _sgu_kernel,
        out_shape=jax.ShapeDtypeStruct((t, aw), BF16),
        grid=(t // rows,),
        in_specs=[
            pl.BlockSpec((rows, aw), lambda i: (i, OFF_AU // aw)),
            pl.BlockSpec((rows, aw), lambda i: (i, OFF_AV // aw)),
            pl.BlockSpec((1, aw), lambda i: (0, 0)),
            pl.BlockSpec((1, aw), lambda i: (0, 0)),
            pl.BlockSpec((A_GROUPS, CHUNK, CHUNK), lambda i: (0, 0, 0)),
            pl.BlockSpec((CHUNK, aw), lambda i: (0, 0)),
        ],
        out_specs=pl.BlockSpec((rows, aw), lambda i: (i, 0)),
        compiler_params=_cparams(1),
        name="sgu",
    )(proj, proj, ln_g, ln_b, w_s, bsb)


def _t5_bucket_np(dist):
    n = np.maximum(dist, 0)
    nf = np.maximum(n, 1).astype(np.float64)
    large = REL_EXACT + (np.log(nf / REL_EXACT) / math.log(REL_MAX_DIST / REL_EXACT)
                         * (REL_BUCKETS - REL_EXACT)).astype(np.int32)
    large = np.minimum(large, REL_BUCKETS - 1)
    return np.where(n < REL_EXACT, n, large).astype(np.int32)


def _bucket_tiles(bq, bk):
    last_bucket_from = int(np.argmax(_t5_bucket_np(np.arange(4 * REL_MAX_DIST)) == REL_BUCKETS - 1))
    n_tiles = -(-(bk - 1 + last_bucket_from) // bq)
    c = np.arange(bk)[:, None]
    r = np.arange(bq)[None, :]
    tiles = []
    for t in range(n_tiles):
        dist = t * bq + r - c
        tiles.append(np.where(dist >= 0, _t5_bucket_np(dist), REL_BUCKETS))
    assert _t5_bucket_np(np.array([n_tiles * bq - (bk - 1)]))[0] == REL_BUCKETS - 1
    return np.stack(tiles).astype(np.int32)


BIAS_ROWS = 64


def _bias_tiles_kernel(rb_ref, bkt_ref, o_ref, *, present):
    hd = pl.program_id(0)
    n_bias, bk, bq = o_ref.shape
    for t in range(n_bias - 1):
        for rc in range(bk // BIAS_ROWS):
            rs = slice(rc * BIAS_ROWS, (rc + 1) * BIAS_ROWS)
            bt = bkt_ref[t, rs, :]
            tile = jnp.where(bt == REL_BUCKETS, NEG_BIG, 0.0).astype(F32)
            for b in present[t][rc]:
                tile = jnp.where(bt == b, rb_ref[b, hd] * LOG2E, tile)
            o_ref[t, rs, :] = tile
    o_ref[n_bias - 1] = jnp.full((bk, bq), rb_ref[REL_BUCKETS - 1, hd] * LOG2E, F32)


def _bias_tiles(rel_bias, buckets_np):
    n_tiles, bk, bq = buckets_np.shape
    heads = rel_bias.shape[1]
    present = tuple(tuple(tuple(int(b) for b in np.unique(buckets_np[t, rc * BIAS_ROWS:(rc + 1) * BIAS_ROWS])
                                if b < REL_BUCKETS)
                          for rc in range(bk // BIAS_ROWS)) for t in range(n_tiles))
    return pl.pallas_call(
        functools.partial(_bias_tiles_kernel, present=present),
        out_shape=jax.ShapeDtypeStruct((heads, n_tiles + 1, bk, bq), F32),
        grid=(heads,),
        in_specs=[pl.BlockSpec(memory_space=pltpu.SMEM),
                  pl.BlockSpec((n_tiles, bk, bq), lambda h: (0, 0, 0))],
        out_specs=pl.BlockSpec((None, n_tiles + 1, bk, bq), lambda h: (h, 0, 0, 0)),
        compiler_params=_cparams(1),
        name="bias_tiles",
    )(rel_bias, jnp.asarray(buckets_np))


def _attn_kernel(lam_ref, sub_ref, bias_ref, q_ref, k_ref, v_ref, o_ref,
                 vt_ref, qqt_ref, s_ref, p_ref, cm_ref, m_ref, al_ref, acc_ref, *, lam_init):
    hw = 2 * B_HEAD_DIM
    hpb, bk = s_ref.shape[0], s_ref.shape[1]
    bq = s_ref.shape[2] // 2
    seq = k_ref.shape[0]
    nq = seq // bq
    n_bias = bias_ref.shape[1]
    cols = [slice(hh * hw, (hh + 1) * hw) for hh in range(hpb)]
    last_block = lambda i: (i * bq) // bk
    n_pairs = sum(last_block(i) + 1 for i in range(nq))

    pad_row = lax.broadcasted_iota(jnp.int32, (vt_ref.shape[2] - hw, bk), 0)
    ones_row = jnp.where(pad_row == 0, 1.0, 0.0).astype(BF16)
    dim = lax.broadcasted_iota(jnp.int32, (hw, bq), 0)
    for hh in range(hpb):
        for c in range(seq // LANES):
            blk, off = divmod(c * LANES, bk)
            vt_ref[hh, blk, 0:hw, off:off + LANES] = (
                v_ref[c * LANES:(c + 1) * LANES, cols[hh]].astype(F32).T.astype(BF16))
        for blk in range(seq // bk):
            vt_ref[hh, blk, hw:, :] = ones_row
        for i in range(nq):
            qt = (q_ref[i * bq:(i + 1) * bq, cols[hh]].astype(F32) * (B_HEAD_DIM ** -0.5 * LOG2E)).T
            qqt_ref[hh, i] = jnp.concatenate([jnp.where(dim < B_HEAD_DIM, qt, 0.0),
                                              jnp.where(dim >= B_HEAD_DIM, qt, 0.0)], axis=1).astype(BF16)
    m_ref[...] = jnp.full(m_ref.shape, NEG_BIG, F32)
    al_ref[...] = jnp.ones(al_ref.shape, F32)
    acc_ref[...] = jnp.zeros(acc_ref.shape, F32)
    p_ref[...] = jnp.zeros(p_ref.shape, BF16)

    far_tile = n_bias - 1
    bias_tile = lambda i, j: jnp.minimum((i * bq - j * bk) // bq, far_tile)
    far_bias = [jnp.concatenate([bias_ref[hh, far_tile, 0:1, :]] * 2, axis=1) for hh in range(hpb)]

    def scores(hh, i, j, far):
        kb = k_ref[pl.ds(pl.multiple_of(j * bk, bk), bk), cols[hh]]
        s = jnp.dot(kb, qqt_ref[hh, i], preferred_element_type=F32)
        if far:
            s_ref[hh] = s
            cm_ref[hh] = jnp.max(s, axis=0, keepdims=True) + far_bias[hh]
        else:
            bias = bias_ref[hh, bias_tile(i, j)]
            s = s + jnp.concatenate([bias, bias], axis=1)
            s_ref[hh] = s
            cm_ref[hh] = jnp.max(s, axis=0, keepdims=True)

    def probabilities(hh, i, far):
        m_prev = m_ref[hh, i]
        m_next = jnp.maximum(m_prev, cm_ref[hh])
        m_ref[hh, i] = m_next
        al_ref[hh] = jnp.exp2(m_prev - m_next)
        pending = jnp.where(far == 1, far_bias[hh], 0.0)
        p_ref[hh] = jnp.exp2(s_ref[hh] - (m_next - pending)).astype(BF16)

    def weighted_values(hh, i, j):
        acc_ref[hh, i] = al_ref[hh] * acc_ref[hh, i] + jnp.dot(vt_ref[hh, j], p_ref[hh],
                                                                preferred_element_type=F32)

    for hh in range(hpb):
        scores(hh, 0, 0, False)

    def body(t, carry):
        ip, jp, ic, jc, cur_far = carry
        wrap = jc >= last_block(ic)
        i_next = jnp.where(wrap, ic + 1, ic)
        j_next = jnp.where(wrap, 0, jc + 1)
        next_far = (bias_tile(i_next, j_next) == far_tile).astype(jnp.int32)

        def step(far):
            for hh in range(hpb):
                weighted_values(hh, ip, jp)
            for hh in range(hpb):
                probabilities(hh, ic, cur_far)
            for hh in range(hpb):
                scores(hh, i_next, j_next, far)

        pl.when(next_far == 1)(functools.partial(step, True))
        pl.when(next_far == 0)(functools.partial(step, False))
        return ic, jc, i_next, j_next, next_far

    zero = jnp.int32(0)
    ip, jp, ic, jc, cur_far = lax.fori_loop(0, n_pairs - 1, body, (zero, zero, zero, zero, zero))
    for hh in range(hpb):
        weighted_values(hh, ip, jp)
        probabilities(hh, ic, cur_far)
    for hh in range(hpb):
        weighted_values(hh, ic, jc)

    lam_p = lam_ref[...]
    lam = (jnp.exp(jnp.sum(lam_p[0:1] * lam_p[1:2], axis=1, keepdims=True))
           - jnp.exp(jnp.sum(lam_p[2:3] * lam_p[3:4], axis=1, keepdims=True)) + lam_init)
    for hh in range(hpb):
        for i in range(nq):
            o = acc_ref[hh, i, 0:hw, :] * (1.0 / acc_ref[hh, i, hw:hw + 1, :])
            attn = o[:, :bq] - lam * o[:, bq:]
            ms = jnp.mean(attn * attn, axis=0, keepdims=True)
            y = ((attn * lax.rsqrt(ms + EPS)) * sub_ref[...]) * (1.0 - lam_init)
            o_ref[i * bq:(i + 1) * bq, cols[hh]] = y.T.astype(BF16)


def _attn(proj, bias_tiles, lam_p, subln_t, *, batch, seq, bq, hpb, lam_init):
    t = proj.shape[0]
    hw = 2 * B_HEAD_DIM
    heads, n_bias, bk, _ = bias_tiles.shape
    nq = seq // bq
    bw = hpb * hw
    return pl.pallas_call(
        functools.partial(_attn_kernel, lam_init=lam_init),
        out_shape=jax.ShapeDtypeStruct((t, heads * hw), BF16),
        grid=(batch, heads // hpb),
        in_specs=[
            pl.BlockSpec((4, B_HEAD_DIM), lambda b, h: (0, 0)),
            pl.BlockSpec((hw, bq), lambda b, h: (0, 0)),
            pl.BlockSpec((hpb, n_bias, bk, bq), lambda b, h: (h, 0, 0, 0)),
            pl.BlockSpec((seq, bw), lambda b, h: (b, OFF_Q // bw + h)),
            pl.BlockSpec((seq, bw), lambda b, h: (b, OFF_K // bw + h)),
            pl.BlockSpec((seq, bw), lambda b, h: (b, OFF_V // bw + h)),
        ],
        out_specs=pl.BlockSpec((seq, bw), lambda b, h: (b, h)),
        scratch_shapes=[pltpu.VMEM((hpb, seq // bk, hw + BF16_ROWS, bk), BF16),
                        pltpu.VMEM((hpb, nq, hw, 2 * bq), BF16),
                        pltpu.VMEM((hpb, bk, 2 * bq), F32),
                        pltpu.VMEM((hpb, bk, 2 * bq), BF16),
                        pltpu.VMEM((hpb, 1, 2 * bq), F32),
                        pltpu.VMEM((hpb, nq, 1, 2 * bq), F32),
                        pltpu.VMEM((hpb, 1, 2 * bq), F32),
                        pltpu.VMEM((hpb, nq, hw + BF16_ROWS, 2 * bq), F32)],
        compiler_params=_cparams(2, vmem_limit=BIG_VMEM_LIMIT_V7X),
        name="diff_attn",
    )(lam_p, subln_t, bias_tiles, proj, proj, proj)


def _split_dot(x, e_bf16, passes):
    out = None
    r = x
    for _ in range(passes):
        hi = r.astype(BF16)
        part = jnp.dot(hi, e_bf16, preferred_element_type=F32)
        out = part if out is None else out + part
        r = r - hi.astype(F32)
    return out


def _mamba_kernel(*refs, n_z, n_xbc):
    z_refs, xbc_refs = refs[:n_z], refs[n_z:n_z + n_xbc]
    (dt_ref, cw_ref, cb_ref, dtb_ref, alog_ref, dsk_ref, ng_ref, e_ref, o_ref, tail_ref, st_ref) = refs[n_z + n_xbc:]
    z_rows = lambda sl: jnp.concatenate([r[sl, :] for r in z_refs], axis=1)
    xbc_rows = lambda sl: jnp.concatenate([r[sl, :] for r in xbc_refs], axis=1)
    rows, c_inner = o_ref.shape
    conv_ch = cw_ref.shape[1]
    heads = c_inner // C_HEAD_DIM
    gw = c_inner // C_GROUPS
    heads_per_group = gw // C_HEAD_DIM
    c = pl.program_id(1)

    @pl.when(c == 0)
    def _():
        tail_ref[...] = jnp.zeros_like(tail_ref)
        st_ref[...] = jnp.zeros_like(st_ref)

    r_i = lax.broadcasted_iota(jnp.int32, (CHUNK, CHUNK), 0)
    c_i = lax.broadcasted_iota(jnp.int32, (CHUNK, CHUNK), 1)
    tri = r_i >= c_i
    lower = tri.astype(F32)
    upper = (r_i <= c_i).astype(F32)
    shift = jnp.concatenate([(r_i - c_i == d).astype(BF16) for d in range(C_CONV - 1, 0, -1)], axis=0)
    row8 = lax.broadcasted_iota(jnp.int32, (8, conv_ch), 0)
    is_head = lax.broadcasted_iota(jnp.int32, (CHUNK, LANES), 1) < heads
    lane_lo = lax.broadcasted_iota(jnp.int32, (CHUNK, LANES), 1) < C_HEAD_DIM
    neg_a = -jnp.exp(alog_ref[...]) * LOG2E
    e = e_ref[...]
    hp = lax.Precision.HIGHEST
    prev = st_ref[...]

    for u in range(rows // CHUNK):
        rs = slice(u * CHUNK, (u + 1) * CHUNK)
        x_cur = xbc_rows(rs)
        delayed = jnp.dot(shift, x_cur, preferred_element_type=F32)
        if u == 0:
            tail = tail_ref[8:16, :]
        else:
            tail = xbc_rows(slice(u * CHUNK - 16, u * CHUNK)).astype(F32)[8:16, :]
        conv = cb_ref[...] + cw_ref[C_CONV - 1:C_CONV, :] * x_cur.astype(F32)
        conv_top = jnp.zeros_like(tail)
        for k in range(C_CONV - 1):
            d = C_CONV - 1 - k
            conv = conv + cw_ref[k:k + 1, :] * delayed[k * CHUNK:(k + 1) * CHUNK]
            conv_top = conv_top + cw_ref[k:k + 1, :] * jnp.where(row8 < d, pltpu.roll(tail, d, 0), 0.0)
        conv = jnp.concatenate([conv[0:8] + conv_top, conv[8:]], axis=0)
        xbc = conv * jax.nn.sigmoid(conv)
        xs = xbc[:, :c_inner]
        bm = xbc[:, c_inner:c_inner + C_GROUPS * C_STATE]
        cm = xbc[:, c_inner + C_GROUPS * C_STATE:]

        dt = jnp.where(is_head, jax.nn.softplus(dt_ref[rs, :] + dtb_ref[...]), 0.0)
        a = dt * neg_a
        a_cs = jnp.dot(lower, a, precision=hp, preferred_element_type=F32)
        a_cs_t = jnp.dot(a.T, upper, precision=hp, preferred_element_type=F32)

        dt_x = _split_dot(dt, e, 2)
        acs_x = _split_dot(a_cs, e, 3)
        last_x = acs_x[CHUNK - 1:CHUNK, :]
        xd = xs * dt_x
        xd_b = xd.astype(BF16)
        xdd_b = (xd * jnp.exp2(last_x - acs_x)).astype(BF16)
        prev_b = prev.astype(BF16)
        eacs = jnp.exp2(acs_x)

        y_parts = []
        st_parts = []
        for g in range(C_GROUPS):
            gs = slice(g * gw, (g + 1) * gw)
            b_g = bm[:, g * C_STATE:(g + 1) * C_STATE]
            c_g = cm[:, g * C_STATE:(g + 1) * C_STATE].astype(BF16)
            cb = lax.dot_general(c_g, b_g.astype(BF16), (((1,), (1,)), ((), ())), preferred_element_type=F32)
            st_parts.append(jnp.dot(b_g.T.astype(BF16), xdd_b[:, gs], preferred_element_type=F32))
            y_off = jnp.dot(c_g, prev_b[:, gs], preferred_element_type=F32) * eacs[:, gs]
            tiles = []
            for pr in range(heads_per_group // 2):
                t_idx = g * (heads_per_group // 2) + pr
                xt = xd_b[:, t_idx * LANES:(t_idx + 1) * LANES]
                res = []
                for half in range(2):
                    hd = 2 * t_idx + half
                    seg = a_cs[:, hd:hd + 1] - a_cs_t[hd:hd + 1, :]
                    decay = jnp.exp2(jnp.where(tri, seg, NEG_BIG))
                    res.append(jnp.dot((cb * decay).astype(BF16), xt, preferred_element_type=F32))
                tiles.append(jnp.where(lane_lo, res[0], res[1]))
            y_parts.append(jnp.concatenate(tiles, axis=1) + y_off)

        prev = prev * jnp.exp2(last_x) + jnp.concatenate(st_parts, axis=1)
        y = jnp.concatenate(y_parts, axis=1) + dsk_ref[...] * xs
        zf = z_rows(rs).astype(F32)
        gated = y * (zf * jax.nn.sigmoid(zf))
        o_ref[rs, :] = _rmsnorm_f32(gated, ng_ref[...]).astype(BF16)

    st_ref[...] = prev
    tail_ref[...] = xbc_rows(slice(rows - 16, rows)).astype(F32)


def _mamba(proj, dt_raw, conv_w, conv_b, dt_bias, a_log, d_skip_x, norm_g, expand, *, batch, seq, rows):
    t = proj.shape[0]
    c_inner = norm_g.shape[1]
    conv_ch = conv_w.shape[1]
    nc = seq // rows
    row = lambda b, c: (b * nc + c, 0)
    const = lambda b, c: (0, 0)
    n_z, n_xbc = c_inner // SPLIT_W, conv_ch // SPLIT_W
    col_block = lambda k: pl.BlockSpec((rows, SPLIT_W), lambda b, c: (b * nc + c, k))
    return pl.pallas_call(
        functools.partial(_mamba_kernel, n_z=n_z, n_xbc=n_xbc),
        out_shape=jax.ShapeDtypeStruct((t, c_inner), BF16),
        grid=(batch, nc),
        in_specs=[col_block(OFF_Z // SPLIT_W + k) for k in range(n_z)]
        + [col_block(OFF_XBC // SPLIT_W + k) for k in range(n_xbc)] + [
            pl.BlockSpec((rows, LANES), row),
            pl.BlockSpec((C_CONV, conv_ch), const),
            pl.BlockSpec((1, conv_ch), const),
            pl.BlockSpec((1, LANES), const),
            pl.BlockSpec((1, LANES), const),
            pl.BlockSpec((1, c_inner), const),
            pl.BlockSpec((1, c_inner), const),
            pl.BlockSpec((LANES, c_inner), const),
        ],
        out_specs=pl.BlockSpec((rows, c_inner), row),
        scratch_shapes=[pltpu.VMEM((16, conv_ch), F32), pltpu.VMEM((C_STATE, c_inner), F32)],
        compiler_params=_cparams(2),
        name="mamba_ssd",
    )(*([proj] * (n_z + n_xbc)), dt_raw, conv_w, conv_b, dt_bias, a_log, d_skip_x, norm_g, expand)


def _merge_kernel(ya_ref, yb_ref, yc_ref, g0_ref, g1_ref, g2_ref, wa_ref, wb_ref, wc_ref, o_ref):
    pa = jnp.dot(ya_ref[...], wa_ref[...], preferred_element_type=F32)
    pb = jnp.dot(yb_ref[...], wb_ref[...], preferred_element_type=F32)
    pc = jnp.dot(yc_ref[...], wc_ref[...], preferred_element_type=F32)
    o_ref[...] = (g0_ref[...].astype(F32) * pa + g1_ref[...].astype(F32) * pb
                  + g2_ref[...].astype(F32) * pc).astype(BF16)


def _merge(ya, yb, yc, proj, w_pa, w_pb, w_pc, layer, *, bm, bn):
    t = ya.shape[0]
    d = w_pa.shape[2]
    gate_spec = lambda k: pl.BlockSpec((bm, bn), lambda i, j: (i, (OFF_GATE + k * d) // bn + j))
    resident = pl.Buffered(1) if bn == d else None
    return pl.pallas_call(
        _merge_kernel,
        out_shape=jax.ShapeDtypeStruct((t, d), BF16),
        grid=(t // bm, d // bn),
        in_specs=[
            pl.BlockSpec((bm, ya.shape[1]), lambda i, j: (i, 0)),
            pl.BlockSpec((bm, yb.shape[1]), lambda i, j: (i, 0)),
            pl.BlockSpec((bm, yc.shape[1]), lambda i, j: (i, 0)),
            gate_spec(0), gate_spec(1), gate_spec(2),
            pl.BlockSpec((None, w_pa.shape[1], bn), lambda i, j: (layer, 0, j), pipeline_mode=resident),
            pl.BlockSpec((None, w_pb.shape[1], bn), lambda i, j: (layer, 0, j), pipeline_mode=resident),
            pl.BlockSpec((None, w_pc.shape[1], bn), lambda i, j: (layer, 0, j), pipeline_mode=resident),
        ],
        out_specs=pl.BlockSpec((bm, bn), lambda i, j: (i, j)),
        compiler_params=_cparams(2),
        name="merge",
    )(ya, yb, yc, proj, proj, proj, w_pa, w_pb, w_pc)


def _outproj_kernel(m_ref, w_ref, h_ref, o_ref):
    o_ref[...] = h_ref[...] + jnp.dot(m_ref[...], w_ref[...], preferred_element_type=F32)


def _outproj(merged, w_out, h, layer, *, bm, bn):
    t, d = h.shape
    return pl.pallas_call(
        _outproj_kernel,
        out_shape=jax.ShapeDtypeStruct((t, d), F32),
        grid=(t // bm, d // bn),
        in_specs=[
            pl.BlockSpec((bm, merged.shape[1]), lambda i, j: (i, 0)),
            pl.BlockSpec((None, merged.shape[1], bn), lambda i, j: (layer, 0, j),
                         pipeline_mode=pl.Buffered(1) if bn == d else None),
            pl.BlockSpec((bm, bn), lambda i, j: (i, j)),
        ],
        out_specs=pl.BlockSpec((bm, bn), lambda i, j: (i, j)),
        compiler_params=_cparams(2),
        name="outproj",
    )(merged, w_out, h)


def _final_norm_kernel(x_ref, g_ref, o_ref):
    o_ref[...] = _rmsnorm_f32(x_ref[...], g_ref[...])


def _final_norm(h, g, *, bm):
    t, d = h.shape
    return pl.pallas_call(
        _final_norm_kernel,
        out_shape=jax.ShapeDtypeStruct((t, d), F32),
        grid=(t // bm,),
        in_specs=[pl.BlockSpec((bm, d), lambda i: (i, 0)), pl.BlockSpec((1, d), lambda i: (0, 0))],
        out_specs=pl.BlockSpec((bm, d), lambda i: (i, 0)),
        compiler_params=_cparams(1),
        name="final_norm",
    )(h, g)


def kernel(x, rel_bias, final_norm, ffn1_norm, ffn1_wi, ffn1_wo, mix_norm, w_in, sgu_ln_g, sgu_ln_b, sgu_w, sgu_b, diff_lambda, diff_subln, conv_w, conv_b, dt_bias, a_log, d_skip, ssm_norm, w_pa, w_pb, w_pc, w_out, ffn2_norm, ffn2_wi, ffn2_wo):
    batch, seq, d = x.shape
    depth = ffn1_wi.shape[0]
    t = batch * seq
    heads_c = dt_bias.shape[1]
    a_width = sgu_ln_g.shape[1]
    c_inner = ssm_norm.shape[1]
    conv_ch = conv_w.shape[2]
    d_ff = ffn1_wo.shape[1]

    bm = min(1024, t)
    bf = 512
    bq = 512
    bk = 512
    assert t % bm == 0 and d_ff % bf == 0 and seq % bk == 0 and bk % bq == 0 and bq % LANES == 0
    assert w_in.shape[2] == OFF_GATE + heads_c + 3 * d
    assert (OFF_AV, OFF_Q, OFF_Z, OFF_XBC, OFF_GATE) == (a_width, 2 * a_width, 5 * a_width, 5 * a_width + c_inner,
                                                         5 * a_width + c_inner + conv_ch)

    w_src = jnp.swapaxes(w_in, 1, 2)
    w_t = w_src[0].astype(BF16)
    wi1, wo1 = ffn1_wi[0].astype(BF16), ffn1_wo[0].astype(BF16)
    wi2, wo2 = ffn2_wi[0].astype(BF16), ffn2_wo[0].astype(BF16)
    wpa, wpb, wpc, wout = w_pa.astype(BF16), w_pb.astype(BF16), w_pc.astype(BF16), w_out.astype(BF16)

    bsb = jnp.repeat(jnp.swapaxes(sgu_b, 1, 2), a_width // A_GROUPS, axis=2)
    pad_h = ((0, 0), (0, LANES - heads_c))
    dtb = jnp.pad(dt_bias, pad_h)
    alog = jnp.pad(a_log, pad_h)
    dsk_x = jnp.repeat(d_skip, C_HEAD_DIM, axis=1)
    expand = np.zeros((LANES, c_inner), np.float32)
    expand[np.arange(c_inner) // C_HEAD_DIM, np.arange(c_inner)] = 1.0
    expand = jnp.asarray(expand, BF16)
    bias_tiles = _bias_tiles(rel_bias, _bucket_tiles(bq, bk))
    subln_t = jnp.broadcast_to(diff_subln[:, :, None], diff_subln.shape + (bq,))

    h = x.reshape(t, d)
    for l in range(depth):
        lam_init = 0.8 - 0.6 * math.exp(-0.3 * l)
        nxt = min(l + 1, depth - 1)
        h, wi1, wo1 = _ffn(h, ffn1_norm[l][None], wi1, wo1, ffn1_wi, ffn1_wo, nxt, bm=bm, bf=bf)
        proj, dt_raw, w_t = _inproj(h, mix_norm[l][None], w_t, w_src, nxt, bm=bm, bn=2048, n_dt=heads_c)
        ya = _sgu(proj, sgu_ln_g[l][None], sgu_ln_b[l][None], sgu_w[l], bsb[l], rows=4 * CHUNK)
        yb = _attn(proj, bias_tiles, diff_lambda[l], subln_t[l],
                   batch=batch, seq=seq, bq=bq, hpb=2, lam_init=lam_init)
        yc = _mamba(proj, dt_raw, conv_w[l], conv_b[l][None], dtb[l][None], alog[l][None],
                    dsk_x[l][None], ssm_norm[l][None], expand, batch=batch, seq=seq, rows=4 * CHUNK)
        merged = _merge(ya, yb, yc, proj, wpa, wpb, wpc, l, bm=bm // 2, bn=d)
        h = _outproj(merged, wout, h, l, bm=bm, bn=d)
        h, wi2, wo2 = _ffn(h, ffn2_norm[l][None], wi2, wo2, ffn2_wi, ffn2_wo, nxt, bm=bm, bf=bf)
    return _final_norm(h, final_norm[None], bm=bm).reshape(batch, seq, d)
```
